```python
import jax
import jax.numpy as jnp
from jax import lax
import numpy as np

D_MODEL = 1024
BATCH = 16
SEQ = 2048
DEPTH = 1
DEC_BATCH = 8
DEC_SEQ = 16
PAST_LEN = 2048

CHUNK = 64
Q_BLOCK = 128
N_MEM = 256
H_MLA = 8
Q_LORA = 256
KV_LORA = 128
NOPE_DIM = 64
ROPE_DIM = 32
V_DIM = 64
ROPE_THETA = 10000.0
MLA_SCALE = (NOPE_DIM + ROPE_DIM) ** -0.5
H_FOX = 8
HD_FOX = 64
FOX_SCALE = HD_FOX ** -0.5
FORGET_BIAS_INIT = 3.0
H_MEM = 4
HD_MEM = 128
MEM_SCALE = HD_MEM ** -0.5
N_BRANCH = 3
N_EXPERTS = 64
TOP_K = 8
N_GROUPS = 8
TOPK_GROUPS = 4
D_EXPERT = 256
D_SHARED = 256
ROUTED_SCALE = 2.5
ALPHA = (2 * DEPTH) ** 0.25
BETA = (8 * DEPTH) ** -0.25
NORM_EPS = 1e-5
NEG_INF = -1e30
IN_WIDTHS = (Q_LORA, KV_LORA, ROPE_DIM, H_FOX * HD_FOX, H_FOX * HD_FOX, H_FOX * HD_FOX, H_FOX,
             H_MEM * HD_MEM, N_BRANCH * D_MODEL)
D_IN = sum(IN_WIDTHS)

kernel_name = "hybrid_mla_fox_memory_moe_stream_step"


def layer_norm(x, g, b):
    xf = x.astype(jnp.float32)
    mu = jnp.mean(xf, axis=-1, keepdims=True)
    var = jnp.mean(jnp.square(xf - mu), axis=-1, keepdims=True)
    y = (xf - mu) * lax.rsqrt(var + NORM_EPS) * g.astype(jnp.float32) + b.astype(jnp.float32)
    return y.astype(x.dtype)


def rms_norm(x, g):
    xf = x.astype(jnp.float32)
    ms = jnp.mean(jnp.square(xf), axis=-1, keepdims=True)
    return (xf * lax.rsqrt(ms + NORM_EPS) * g.astype(jnp.float32)).astype(x.dtype)


def rope(x, pos):
    half = x.shape[-1] // 2
    inv_freq = ROPE_THETA ** (-jnp.arange(half, dtype=jnp.float32) / half)
    ang = pos.astype(jnp.float32)[:, None] * inv_freq[None, :]
    shape = (pos.shape[0],) + (1,) * (x.ndim - 3) + (half,)
    cos = jnp.cos(ang).reshape(shape).astype(x.dtype)
    sin = jnp.sin(ang).reshape(shape).astype(x.dtype)
    x1, x2 = x[..., :half], x[..., half:]
    return jnp.concatenate([x1 * cos - x2 * sin, x1 * sin + x2 * cos], axis=-1)


def project_in(h, pos, w_in, b_f, q_norm_g, w_q_up, kv_norm_g, w_kv_up):
    b, s, _ = h.shape
    split_points = np.cumsum(IN_WIDTHS)[:-1].tolist()
    q_lat, c_kv, k_r, f_q, f_k, f_v, f_logit, m_q, g = jnp.split(h @ w_in, split_points, axis=-1)
    q = (rms_norm(q_lat, q_norm_g) @ w_q_up).reshape(b, s, H_MLA, NOPE_DIM + ROPE_DIM)
    q_nope = q[..., :NOPE_DIM]
    q_rope = rope(q[..., NOPE_DIM:], pos)
    latent = rms_norm(c_kv, kv_norm_g)
    k_rope = rope(k_r, pos)
    f_q = f_q.reshape(b, s, H_FOX, HD_FOX)
    f_k = f_k.reshape(b, s, H_FOX, HD_FOX)
    f_v = f_v.reshape(b, s, H_FOX, HD_FOX)
    log_f = jax.nn.log_sigmoid((f_logit + b_f).astype(jnp.float32))
    m_q = m_q.reshape(b, s, H_MEM, HD_MEM)
    gates = jax.nn.sigmoid(g.astype(jnp.float32)).astype(h.dtype).reshape(b, s, N_BRANCH, D_MODEL)
    return q_nope, q_rope, latent, k_rope, f_q, f_k, f_v, log_f, m_q, gates


def mla_expand(latent, w_kv_up):
    b, l, _ = latent.shape
    kv = (latent @ w_kv_up).reshape(b, l, H_MLA, NOPE_DIM + V_DIM)
    return kv[..., :NOPE_DIM], kv[..., NOPE_DIM:]


def masked_softmax(s, mask):
    return jax.nn.softmax(jnp.where(mask, s.astype(jnp.float32), NEG_INF), axis=-1)


def attend_block(q_nope, q_rope, f_q, c_q, m_q, q_pos,
                 k_nope, k_rope, v_mla, f_k, f_v, c_k, k_pos, m_k, m_v):
    s = (jnp.einsum('bqhd,bkhd->bhqk', q_nope, k_nope)
         + jnp.einsum('bqhr,bkr->bhqk', q_rope, k_rope)) * MLA_SCALE
    chunk_mask = (k_pos[None, :] // CHUNK) <= (q_pos[:, None] // CHUNK)
    p = masked_softmax(s, chunk_mask).astype(v_mla.dtype)
    o_mla = jnp.einsum('bhqk,bkhd->bqhd', p, v_mla)
    s = jnp.einsum('bqhd,bkhd->bhqk', f_q, f_k).astype(jnp.float32) * FOX_SCALE
    s = s + (jnp.swapaxes(c_q, 1, 2)[..., :, None] - jnp.swapaxes(c_k, 1, 2)[..., None, :])
    causal = k_pos[None, :] <= q_pos[:, None]
    p = masked_softmax(s, causal).astype(f_v.dtype)
    o_fox = jnp.einsum('bhqk,bkhd->bqhd', p, f_v)
    s = jnp.einsum('bqhd,bmhd->bhqm', m_q, m_k).astype(jnp.float32) * MEM_SCALE
    p = jax.nn.softmax(s, axis=-1).astype(m_v.dtype)
    o_mem = jnp.einsum('bhqm,bmhd->bqhd', p, m_v)
    return o_mla, o_fox, o_mem


def merge_branches(o_mla, o_fox, o_mem, gates, w_br_mla, w_br_fox, w_br_mem, w_out):
    b, s = gates.shape[:2]
    y = (gates[:, :, 0] * (o_mla.reshape(b, s, -1) @ w_br_mla)
         + gates[:, :, 1] * (o_fox.reshape(b, s, -1) @ w_br_fox)
         + gates[:, :, 2] * (o_mem.reshape(b, s, -1) @ w_br_mem))
    return y @ w_out


def moe_ffn(x, w_router, router_bias, w_e_gate, w_e_up, w_e_down, w_s_gate, w_s_up, w_s_down):
    b, s, d = x.shape
    t = x.reshape(-1, d)
    scores = jax.nn.sigmoid((t @ w_router).astype(jnp.float32))
    sel = scores + router_bias.astype(jnp.float32)
    grp = sel.reshape(-1, N_GROUPS, N_EXPERTS // N_GROUPS)
    group_score = lax.top_k(grp, 2)[0].sum(-1)
    _, g_idx = lax.top_k(group_score, TOPK_GROUPS)
    g_mask = jax.nn.one_hot(g_idx, N_GROUPS, dtype=jnp.float32).sum(1) > 0
    e_mask = jnp.repeat(g_mask, N_EXPERTS // N_GROUPS, axis=1)
    _, e_idx = lax.top_k(jnp.where(e_mask, sel, -jnp.inf), TOP_K)
    w = jnp.take_along_axis(scores, e_idx, axis=1)
    w = w / jnp.sum(w, axis=-1, keepdims=True) * ROUTED_SCALE
    gate = jnp.sum(jax.nn.one_hot(e_idx, N_EXPERTS, dtype=jnp.float32) * w[..., None], axis=1)
    gate = gate.astype(x.dtype)

    def expert_step(acc, ew):
        wg, wu, wd, g_col = ew
        hid = jax.nn.silu(t @ wg) * (t @ wu)
        return acc + g_col[:, None] * (hid @ wd), None

    routed, _ = lax.scan(expert_step, jnp.zeros_like(t), (w_e_gate, w_e_up, w_e_down, gate.T))
    shared = (jax.nn.silu(t @ w_s_gate) * (t @ w_s_up)) @ w_s_down
    return (routed + shared).reshape(b, s, d)


def post_norm_block(h, mixer_out, ln1_g, ln1_b, ln2_g, ln2_b, moe_w):
    x1 = layer_norm(ALPHA * h + mixer_out, ln1_g, ln1_b)
    return layer_norm(ALPHA * x1 + moe_ffn(x1, *moe_w), ln2_g, ln2_b)


def to_blocks(a):
    b, s = a.shape[:2]
    return jnp.swapaxes(a.reshape(b, s // Q_BLOCK, Q_BLOCK, *a.shape[2:]), 0, 1)


def from_blocks(a):
    nb, b, qb = a.shape[:3]
    return jnp.swapaxes(a, 0, 1).reshape(b, nb * qb, *a.shape[3:])


def setup_inputs(seed: int = 0) -> dict:
    key = jax.random.key(seed)
    ks = jax.random.split(key, 40)
    f32 = jnp.float32

    def nrm(k, shape, scale=1.0):
        return jax.random.normal(k, shape, f32) * scale

    L = DEPTH
    return {
        "x_prompt": nrm(ks[0], (BATCH, SEQ, D_MODEL)),
        "x_sample": nrm(ks[1], (DEC_BATCH, DEC_SEQ, D_MODEL)),
        "mem_prompt": nrm(ks[2], (BATCH, N_MEM, D_MODEL)),
        "cache_mla_latent": nrm(ks[3], (L, DEC_BATCH, PAST_LEN, KV_LORA)),
        "cache_mla_krope": nrm(ks[4], (L, DEC_BATCH, PAST_LEN, ROPE_DIM)),
        "cache_fox_k": nrm(ks[5], (L, DEC_BATCH, PAST_LEN, H_FOX, HD_FOX)),
        "cache_fox_v": nrm(ks[6], (L, DEC_BATCH, PAST_LEN, H_FOX, HD_FOX)),
        "cache_fox_logf": jax.nn.log_sigmoid(FORGET_BIAS_INIT + nrm(ks[7], (L, DEC_BATCH, PAST_LEN, H_FOX))),
        "cache_mem_k": nrm(ks[8], (L, DEC_BATCH, N_MEM, H_MEM, HD_MEM)),
        "cache_mem_v": nrm(ks[9], (L, DEC_BATCH, N_MEM, H_MEM, HD_MEM)),
        "ln1_g": 1.0 + nrm(ks[10], (L, D_MODEL), 0.01),
        "ln1_b": nrm(ks[11], (L, D_MODEL), 0.01),
        "w_in": nrm(ks[12], (L, D_MODEL, D_IN), D_MODEL ** -0.5),
        "b_f": FORGET_BIAS_INIT + nrm(ks[13], (L, H_FOX), 0.1),
        "q_norm_g": 1.0 + nrm(ks[14], (L, Q_LORA), 0.01),
        "w_q_up": nrm(ks[15], (L, Q_LORA, H_MLA * (NOPE_DIM + ROPE_DIM)), Q_LORA ** -0.5),
        "kv_norm_g": 1.0 + nrm(ks[16], (L, KV_LORA), 0.01),
        "w_kv_up": nrm(ks[17], (L, KV_LORA, H_MLA * (NOPE_DIM + V_DIM)), KV_LORA ** -0.5),
        "w_mem_k": nrm(ks[18], (L, D_MODEL, H_MEM * HD_MEM), D_MODEL ** -0.5),
        "w_mem_v": nrm(ks[19], (L, D_MODEL, H_MEM * HD_MEM), D_MODEL ** -0.5),
        "w_br_mla": nrm(ks[20], (L, H_MLA * V_DIM, D_MODEL), BETA * (H_MLA * V_DIM) ** -0.5),
        "w_br_fox": nrm(ks[21], (L, H_FOX * HD_FOX, D_MODEL), BETA * (H_FOX * HD_FOX) ** -0.5),
        "w_br_mem": nrm(ks[22], (L, H_MEM * HD_MEM, D_MODEL), BETA * (H_MEM * HD_MEM) ** -0.5),
        "w_out": nrm(ks[23], (L, D_MODEL, D_MODEL), BETA * D_MODEL ** -0.5),
        "ln2_g": 1.0 + nrm(ks[24], (L, D_MODEL), 0.01),
        "ln2_b": nrm(ks[25], (L, D_MODEL), 0.01),
        "w_router": nrm(ks[26], (L, D_MODEL, N_EXPERTS), D_MODEL ** -0.5),
        "router_bias": nrm(ks[27], (L, N_EXPERTS), 0.01),
        "w_e_gate": nrm(ks[28], (L, N_EXPERTS, D_MODEL, D_EXPERT), D_MODEL ** -0.5),
        "w_e_up": nrm(ks[29], (L, N_EXPERTS, D_MODEL, D_EXPERT), D_MODEL ** -0.5),
        "w_e_down": nrm(ks[30], (L, N_EXPERTS, D_EXPERT, D_MODEL), BETA * D_EXPERT ** -0.5),
        "w_s_gate": nrm(ks[31], (L, D_MODEL, D_SHARED), D_MODEL ** -0.5),
        "w_s_up": nrm(ks[32], (L, D_MODEL, D_SHARED), D_MODEL ** -0.5),
        "w_s_down": nrm(ks[33], (L, D_SHARED, D_MODEL), BETA * D_SHARED ** -0.5),
    }


def reference(x_prompt, x_sample, mem_prompt, cache_mla_latent, cache_mla_krope, cache_fox_k,
              cache_fox_v, cache_fox_logf, cache_mem_k, cache_mem_v, ln1_g, ln1_b, w_in, b_f,
              q_norm_g, w_q_up, kv_norm_g, w_kv_up, w_mem_k, w_mem_v, w_br_mla, w_br_fox,
              w_br_mem, w_out, ln2_g, ln2_b, w_router, router_bias, w_e_gate, w_e_up, w_e_down,
              w_s_gate, w_s_up, w_s_down):
    pos_p = jnp.arange(SEQ, dtype=jnp.int32)
    q_pos_s = PAST_LEN + jnp.arange(DEC_SEQ, dtype=jnp.int32)
    k_pos_s = jnp.arange(PAST_LEN + DEC_SEQ, dtype=jnp.int32)
    h_p, h_s = x_prompt, x_sample
    p_lat, p_kr, p_fk, p_fv, p_lf, p_mk, p_mv = [], [], [], [], [], [], []
    s_lat, s_kr, s_fk, s_fv, s_lf = [], [], [], [], []

    for l in range(DEPTH):
        mix_w = (w_in[l], b_f[l], q_norm_g[l], w_q_up[l], kv_norm_g[l], w_kv_up[l])
        merge_w = (w_br_mla[l], w_br_fox[l], w_br_mem[l], w_out[l])
        moe_w = (w_router[l], router_bias[l], w_e_gate[l], w_e_up[l], w_e_down[l],
                 w_s_gate[l], w_s_up[l], w_s_down[l])
        norm_w = (ln1_g[l], ln1_b[l], ln2_g[l], ln2_b[l])

        q_nope, q_rope, lat, k_rope, f_q, f_k, f_v, log_f, m_q, gates = project_in(h_p, pos_p, *mix_w)
        k_nope, v_mla = mla_expand(lat, w_kv_up[l])
        c_f = jnp.cumsum(log_f, axis=1)
        m_k = (mem_prompt @ w_mem_k[l]).reshape(BATCH, N_MEM, H_MEM, HD_MEM)
        m_v = (mem_prompt @ w_mem_v[l]).reshape(BATCH, N_MEM, H_MEM, HD_MEM)
        outs = lax.map(
            lambda qb: attend_block(*qb, k_nope, k_rope, v_mla, f_k, f_v, c_f, pos_p, m_k, m_v),
            (to_blocks(q_nope), to_blocks(q_rope), to_blocks(f_q), to_blocks(c_f), to_blocks(m_q),
             pos_p.reshape(-1, Q_BLOCK)))
        o_mla, o_fox, o_mem = (from_blocks(o) for o in outs)
        mix = merge_branches(o_mla, o_fox, o_mem, gates, *merge_w)
        p_lat.append(lat)
        p_kr.append(k_rope)
        p_fk.append(f_k)
        p_fv.append(f_v)
        p_lf.append(log_f.astype(h_p.dtype))
        p_mk.append(m_k)
        p_mv.append(m_v)
        h_p = post_norm_block(h_p, mix, *norm_w, moe_w)

        q_nope, q_rope, lat, k_rope, f_q, f_k, f_v, log_f, m_q, gates = project_in(h_s, q_pos_s, *mix_w)
        lat_all = jnp.concatenate([cache_mla_latent[l], lat], axis=1)
        kr_all = jnp.concatenate([cache_mla_krope[l], k_rope], axis=1)
        k_nope, v_mla = mla_expand(lat_all, w_kv_up[l])
        fk_all = jnp.concatenate([cache_fox_k[l], f_k], axis=1)
        fv_all = jnp.concatenate([cache_fox_v[l], f_v], axis=1)
        c_all = jnp.cumsum(jnp.concatenate([cache_fox_logf[l].astype(jnp.float32), log_f], axis=1), axis=1)
        o_mla, o_fox, o_mem = attend_block(q_nope, q_rope, f_q, c_all[:, PAST_LEN:], m_q, q_pos_s,
                                           k_nope, kr_all, v_mla, fk_all, fv_all, c_all, k_pos_s,
                                           cache_mem_k[l], cache_mem_v[l])
        mix = merge_branches(o_mla, o_fox, o_mem, gates, *merge_w)
        s_lat.append(lat)
        s_kr.append(k_rope)
        s_fk.append(f_k)
        s_fv.append(f_v)
        s_lf.append(log_f.astype(h_s.dtype))
        h_s = post_norm_block(h_s, mix, *norm_w, moe_w)

    return (h_p, h_s,
            jnp.stack(p_lat), jnp.stack(p_kr), jnp.stack(p_fk), jnp.stack(p_fv), jnp.stack(p_lf),
            jnp.stack(p_mk), jnp.stack(p_mv),
            jnp.stack(s_lat), jnp.stack(s_kr), jnp.stack(s_fk), jnp.stack(s_fv), jnp.stack(s_lf))
```

```python
import functools

import jax
import jax.numpy as jnp
import numpy as np
from jax import lax
from jax.experimental import pallas as pl
from jax.experimental.pallas import tpu as pltpu

F32 = jnp.float32
BF16 = jnp.bfloat16

D_MODEL = 1024
CHUNK = 64
H_MLA = 8
Q_LORA = 256
KV_LORA = 128
NOPE_DIM = 64
ROPE_DIM = 32
V_DIM = 64
ROPE_THETA = 10000.0
MLA_SCALE = (NOPE_DIM + ROPE_DIM) ** -0.5
H_FOX = 8
HD_FOX = 64
FOX_SCALE = HD_FOX ** -0.5
H_MEM = 4
HD_MEM = 128
MEM_SCALE = HD_MEM ** -0.5
N_BRANCH = 3
N_EXPERTS = 64
TOP_K = 8
N_GROUPS = 8
GROUP_SIZE = N_EXPERTS // N_GROUPS
TOPK_GROUPS = 4
D_EXPERT = 256
ROUTED_SCALE = 2.5
DEPTH = 1
ALPHA = (2 * DEPTH) ** 0.25
NORM_EPS = 1e-5
NEG_INF = -1e30

LANES = 128
VMEM_LIMIT = 56 * 1024 * 1024

C_QLAT = 0
C_CKV = C_QLAT + Q_LORA
C_KR = C_CKV + KV_LORA
C_KRR = C_KR + LANES
C_FQ = C_KRR + LANES
C_FK = C_FQ + H_FOX * HD_FOX
C_FV = C_FK + H_FOX * HD_FOX
C_MQ = C_FV + H_FOX * HD_FOX
C_FL = C_MQ + H_MEM * HD_MEM
C_END = C_FL + LANES


def _params(n_grid_axes):
    return pltpu.CompilerParams(
        dimension_semantics=("arbitrary",) * n_grid_axes, vmem_limit_bytes=VMEM_LIMIT)


def _full(shape):
    return pl.BlockSpec(shape, lambda *_: (0,) * len(shape))


def _dot(a, b):
    return jnp.dot(a, b, preferred_element_type=F32)


def _dot_nt(a, b):
    return lax.dot_general(a, b, (((1,), (1,)), ((), ())), preferred_element_type=F32)


def _sigmoid(x):
    return 1.0 / (1.0 + jnp.exp(-x))


def _rms(x, g):
    ms = jnp.mean(x * x, axis=-1, keepdims=True)
    return x * lax.rsqrt(ms + NORM_EPS) * g


def _layer_norm(x, g, b):
    mu = jnp.mean(x, axis=-1, keepdims=True)
    xc = x - mu
    var = jnp.mean(xc * xc, axis=-1, keepdims=True)
    return xc * lax.rsqrt(var + NORM_EPS) * g + b


def _proj_kernel(x_ref, w_ref, wq_ref, wabs_ref, qg_ref, kvg_ref, bf_ref, cos_ref, sin_ref,
                 qcat_ref, kcat_ref, lat_ref, latb_ref, kr_ref, fq_ref, fk_ref, fv_ref,
                 fkb_ref, fvb_ref, mq_ref, logf_ref, logfp_ref):
    xb = x_ref[...].astype(BF16)
    tm = xb.shape[0]

    def proj(c0, width):
        return _dot(xb, w_ref[:, c0:c0 + width])

    cos = cos_ref[...]
    sin = sin_ref[...]
    qn = _rms(proj(C_QLAT, Q_LORA), qg_ref[...]).astype(BF16)
    q2 = _dot(qn, wq_ref[...])
    nq = H_MLA * NOPE_DIM
    nr = H_MLA * ROPE_DIM
    q_rope = (q2[:, nq:nq + nr] * cos + q2[:, nq + nr:nq + 2 * nr] * sin) * MLA_SCALE
    lane = lax.broadcasted_iota(jnp.int32, (tm, LANES), 1)
    heads_per_group = LANES // ROPE_DIM
    for j in range(H_MLA // 2):
        q_abs = _dot(q2[:, j * LANES:(j + 1) * LANES].astype(BF16), wabs_ref[j]) * MLA_SCALE
        for half in range(2):
            h = 2 * j + half
            grp = h // heads_per_group
            qr = q_rope[:, grp * LANES:(grp + 1) * LANES]
            qr = jnp.where(lane // ROPE_DIM == h % heads_per_group, qr, 0.0)
            qcat_ref[h, :, 0:LANES] = q_abs[:, half * LANES:(half + 1) * LANES].astype(BF16)
            qcat_ref[h, :, LANES:2 * LANES] = qr.astype(BF16)
    lat = _rms(proj(C_CKV, KV_LORA), kvg_ref[...])
    lat_ref[...] = lat
    latb_ref[...] = lat.astype(BF16)
    kr = proj(C_KR, LANES) * cos[:, :LANES] + proj(C_KRR, LANES) * sin[:, :LANES]
    kr_ref[...] = kr[:, :ROPE_DIM]
    kcat_ref[:, 0:KV_LORA] = lat.astype(BF16)
    kcat_ref[:, KV_LORA:KV_LORA + LANES] = kr.astype(BF16)
    fq_ref[...] = (proj(C_FQ, H_FOX * HD_FOX) * FOX_SCALE).astype(BF16)
    fk = proj(C_FK, H_FOX * HD_FOX)
    fk_ref[...] = fk
    fkb_ref[...] = fk.astype(BF16)
    fv = proj(C_FV, H_FOX * HD_FOX)
    fv_ref[...] = fv
    fvb_ref[...] = fv.astype(BF16)
    mq_ref[...] = proj(C_MQ, H_MEM * HD_MEM).astype(BF16)
    z = proj(C_FL, LANES) + bf_ref[...]
    log_f = jnp.minimum(z, 0.0) - jnp.log1p(jnp.exp(-jnp.abs(z)))
    logfp_ref[...] = log_f
    logf_ref[...] = log_f[:, :H_FOX]


def _proj(x, w_main, w_q, w_abs, q_g, kv_g, b_f, cos_t, sin_t, tm):
    t = x.shape[0]
    n_tab = cos_t.shape[0] // tm
    row = lambda w: pl.BlockSpec((tm, w), lambda i: (i, 0))
    tab = pl.BlockSpec((tm, 2 * LANES), lambda i: (i % n_tab, 0))
    hw = H_FOX * HD_FOX
    out_shape = (
        jax.ShapeDtypeStruct((H_MLA, t, 2 * LANES), BF16),
        jax.ShapeDtypeStruct((t, 2 * LANES), BF16),
        jax.ShapeDtypeStruct((t, KV_LORA), F32),
        jax.ShapeDtypeStruct((t, KV_LORA), BF16),
        jax.ShapeDtypeStruct((t, ROPE_DIM), F32),
        jax.ShapeDtypeStruct((t, hw), BF16),
        jax.ShapeDtypeStruct((t, hw), F32),
        jax.ShapeDtypeStruct((t, hw), F32),
        jax.ShapeDtypeStruct((t, hw), BF16),
        jax.ShapeDtypeStruct((t, hw), BF16),
        jax.ShapeDtypeStruct((t, H_MEM * HD_MEM), BF16),
        jax.ShapeDtypeStruct((t, H_FOX), F32),
        jax.ShapeDtypeStruct((t, LANES), F32),
    )
    out_specs = (
        pl.BlockSpec((H_MLA, tm, 2 * LANES), lambda i: (0, i, 0)),
        row(2 * LANES), row(KV_LORA), row(KV_LORA), row(ROPE_DIM), row(hw), row(hw), row(hw),
        row(hw), row(hw), row(H_MEM * HD_MEM), row(H_FOX), row(LANES),
    )
    return pl.pallas_call(
        _proj_kernel,
        grid=(t // tm,),
        in_specs=[row(D_MODEL), _full(w_main.shape), _full(w_q.shape), _full(w_abs.shape),
                  _full(q_g.shape), _full(kv_g.shape), _full(b_f.shape), tab, tab],
        out_specs=out_specs,
        out_shape=out_shape,
        compiler_params=_params(1),
    )(x, w_main, w_q, w_abs, q_g, kv_g, b_f, cos_t, sin_t)


def _memkv_kernel(x_ref, wk_ref, wv_ref, k_ref, v_ref, kb_ref, vb_ref):
    xb = x_ref[...].astype(BF16)
    k = _dot(xb, wk_ref[...])
    v = _dot(xb, wv_ref[...])
    k_ref[...] = k
    v_ref[...] = v
    kb_ref[...] = k.astype(BF16)
    vb_ref[...] = v.astype(BF16)


def _memkv(x, wk, wv, tm):
    t = x.shape[0]
    w = H_MEM * HD_MEM
    row = lambda n: pl.BlockSpec((tm, n), lambda i: (i, 0))
    return pl.pallas_call(
        _memkv_kernel,
        grid=(t // tm,),
        in_specs=[row(D_MODEL), _full(wk.shape), _full(wv.shape)],
        out_specs=(row(w), row(w), row(w), row(w)),
        out_shape=(jax.ShapeDtypeStruct((t, w), F32), jax.ShapeDtypeStruct((t, w), F32),
                   jax.ShapeDtypeStruct((t, w), BF16), jax.ShapeDtypeStruct((t, w), BF16)),
        compiler_params=_params(1),
    )(x, wk, wv)


CUM_BLOCK = 256


def _cumsum_kernel(x_ref, c_ref):
    s = x_ref.shape[1]
    r = lax.broadcasted_iota(jnp.int32, (CUM_BLOCK, CUM_BLOCK), 0)
    c = lax.broadcasted_iota(jnp.int32, (CUM_BLOCK, CUM_BLOCK), 1)
    tri = jnp.where(r >= c, 1.0, 0.0).astype(BF16)

    def body(i, carry):
        r0 = pl.multiple_of(i * CUM_BLOCK, CUM_BLOCK)
        x = x_ref[0, pl.ds(r0, CUM_BLOCK), :]
        hi = x.astype(BF16)
        r1 = x - hi.astype(F32)
        mid = r1.astype(BF16)
        lo = (r1 - mid.astype(F32)).astype(BF16)
        cs = _dot(tri, hi) + _dot(tri, mid) + _dot(tri, lo) + carry
        c_ref[0, pl.ds(r0, CUM_BLOCK), :] = cs
        return cs[CUM_BLOCK - 1:CUM_BLOCK, :]

    lax.fori_loop(0, s // CUM_BLOCK, body, jnp.zeros((1, LANES), F32))


def _cumsum(x):
    b, s, _ = x.shape
    spec = pl.BlockSpec((1, s, LANES), lambda i: (i, 0, 0))
    return pl.pallas_call(
        _cumsum_kernel, grid=(b,), in_specs=[spec], out_specs=spec,
        out_shape=jax.ShapeDtypeStruct(x.shape, F32), compiler_params=_params(1),
    )(x)


def _online_softmax_step(s, v, m, l, acc):
    m_new = jnp.maximum(m, jnp.max(s, axis=-1, keepdims=True))
    alpha = jnp.exp(m - m_new)
    p = jnp.exp(s - m_new)
    l = alpha * l + jnp.sum(p, axis=-1, keepdims=True)
    acc = alpha * acc + _dot(p.astype(BF16), v)
    return m_new, l, acc


def _n_kv_tiles(last_key_excl, tk, sk):
    return jnp.minimum((last_key_excl + tk - 1) // tk, sk // tk)


def _mla_kernel(q_ref, k_ref, v_ref, o_ref, *, tq, tk, kv_len, q_pos0):
    sk = k_ref.shape[1]
    q_start = q_pos0 + pl.program_id(1) * tq
    chunk_end = ((q_start + tq - 1) // CHUNK + 1) * CHUNK
    n_kv = _n_kv_tiles(jnp.minimum(chunk_end, kv_len), tk, sk)
    q_chunk = (q_start + lax.broadcasted_iota(jnp.int32, (tq, 1), 0)) // CHUNK
    for h in range(H_MLA):
        q = q_ref[h]

        def body(kt, carry):
            k0 = pl.multiple_of(kt * tk, tk)
            k = k_ref[0, pl.ds(k0, tk), :]
            v = v_ref[0, pl.ds(k0, tk), :]
            s = _dot_nt(q, k)
            k_pos = k0 + lax.broadcasted_iota(jnp.int32, (1, tk), 1)
            mask = jnp.logical_and(k_pos // CHUNK <= q_chunk, k_pos < kv_len)
            s = jnp.where(mask, s, NEG_INF)
            return _online_softmax_step(s, v, *carry)

        init = (jnp.full((tq, 1), NEG_INF, F32), jnp.zeros((tq, 1), F32),
                jnp.zeros((tq, KV_LORA), F32))
        _, l, acc = lax.fori_loop(0, n_kv, body, init)
        o_ref[:, h * KV_LORA:(h + 1) * KV_LORA] = (acc / l).astype(BF16)


def _mla_attention(qcat, kcat, latb, batch, sq, tq, tk, kv_len, q_pos0):
    sk = kcat.shape[1]
    nq = sq // tq
    kern = functools.partial(_mla_kernel, tq=tq, tk=tk, kv_len=kv_len, q_pos0=q_pos0)
    return pl.pallas_call(
        kern,
        grid=(batch, nq),
        in_specs=[pl.BlockSpec((H_MLA, tq, 2 * LANES), lambda b, i: (0, b * nq + i, 0)),
                  pl.BlockSpec((1, sk, 2 * LANES), lambda b, i: (b, 0, 0)),
                  pl.BlockSpec((1, sk, KV_LORA), lambda b, i: (b, 0, 0))],
        out_specs=pl.BlockSpec((tq, H_MLA * KV_LORA), lambda b, i: (b * nq + i, 0)),
        out_shape=jax.ShapeDtypeStruct((batch * sq, H_MLA * KV_LORA), BF16),
        compiler_params=_params(2),
    )(qcat, kcat, latb)


def _fox_kernel(q_ref, k_ref, v_ref, cq_ref, ck_ref, o_ref, *, tq, tk, kv_len, q_pos0):
    sk = k_ref.shape[1]
    q_start = q_pos0 + pl.program_id(1) * tq
    n_kv = _n_kv_tiles(jnp.minimum(q_start + tq, kv_len), tk, sk)
    q_pos = q_start + lax.broadcasted_iota(jnp.int32, (tq, 1), 0)
    lane = lax.broadcasted_iota(jnp.int32, (tq, LANES), 1)
    heads_per_group = LANES // HD_FOX
    for j in range(H_FOX // heads_per_group):
        cols = slice(j * LANES, (j + 1) * LANES)
        q_pair = q_ref[:, cols]
        outs = []
        for half in range(heads_per_group):
            h = heads_per_group * j + half
            q = jnp.where(lane // HD_FOX == half, q_pair, jnp.zeros_like(q_pair))
            c_q = cq_ref[0, :, h:h + 1]

            def body(kt, carry):
                k0 = pl.multiple_of(kt * tk, tk)
                k = k_ref[0, pl.ds(k0, tk), cols]
                v = v_ref[0, pl.ds(k0, tk), cols]
                c_k = ck_ref[0, h:h + 1, pl.ds(k0, tk)]
                s = _dot_nt(q, k) + (c_q - c_k)
                k_pos = k0 + lax.broadcasted_iota(jnp.int32, (1, tk), 1)
                mask = jnp.logical_and(k_pos <= q_pos, k_pos < kv_len)
                s = jnp.where(mask, s, NEG_INF)
                return _online_softmax_step(s, v, *carry)

            init = (jnp.full((tq, 1), NEG_INF, F32), jnp.zeros((tq, 1), F32),
                    jnp.zeros((tq, LANES), F32))
            _, l, acc = lax.fori_loop(0, n_kv, body, init)
            outs.append(acc / l)
        o = outs[0]
        for half in range(1, heads_per_group):
            o = jnp.where(lane // HD_FOX == half, outs[half], o)
        o_ref[:, cols] = o.astype(BF16)


def _fox_attention(fq, fkb, fvb, c_q, c_k, batch, sq, tq, tk, kv_len, q_pos0):
    sk = fkb.shape[1]
    nq = sq // tq
    w = H_FOX * HD_FOX
    kern = functools.partial(_fox_kernel, tq=tq, tk=tk, kv_len=kv_len, q_pos0=q_pos0)
    return pl.pallas_call(
        kern,
        grid=(batch, nq),
        in_specs=[pl.BlockSpec((tq, w), lambda b, i: (b * nq + i, 0)),
                  pl.BlockSpec((1, sk, w), lambda b, i: (b, 0, 0)),
                  pl.BlockSpec((1, sk, w), lambda b, i: (b, 0, 0)),
                  pl.BlockSpec((1, tq, LANES), lambda b, i: (b, i, 0)),
                  pl.BlockSpec((1, H_FOX, sk), lambda b, i: (b, 0, 0))],
        out_specs=pl.BlockSpec((tq, w), lambda b, i: (b * nq + i, 0)),
        out_shape=jax.ShapeDtypeStruct((batch * sq, w), BF16),
        compiler_params=_params(2),
    )(fq, fkb, fvb, c_q, c_k)


def _mem_kernel(q_ref, k_ref, v_ref, o_ref):
    for h in range(H_MEM):
        cols = slice(h * HD_MEM, (h + 1) * HD_MEM)
        s = _dot_nt(q_ref[:, cols], k_ref[0, :, cols]) * MEM_SCALE
        m = jnp.max(s, axis=-1, keepdims=True)
        p = jnp.exp(s - m)
        l = jnp.sum(p, axis=-1, keepdims=True)
        o = _dot(p.astype(BF16), v_ref[0, :, cols]) / l
        o_ref[:, cols] = o.astype(BF16)


def _mem_attention(mq, mkb, mvb, batch, sq, tq):
    nq = sq // tq
    n_mem = mkb.shape[1]
    w = H_MEM * HD_MEM
    return pl.pallas_call(
        _mem_kernel,
        grid=(batch, nq),
        in_specs=[pl.BlockSpec((tq, w), lambda b, i: (b * nq + i, 0)),
                  pl.BlockSpec((1, n_mem, w), lambda b, i: (b, 0, 0)),
                  pl.BlockSpec((1, n_mem, w), lambda b, i: (b, 0, 0))],
        out_specs=pl.BlockSpec((tq, w), lambda b, i: (b * nq + i, 0)),
        out_shape=jax.ShapeDtypeStruct((batch * sq, w), BF16),
        compiler_params=_params(2),
    )(mq, mkb, mvb)


def _route(x1b, wr_t, bias):
    tm = x1b.shape[0]
    scores = _sigmoid(_dot_nt(wr_t, x1b))
    sel = scores + bias
    sub = lax.broadcasted_iota(jnp.int32, (GROUP_SIZE, tm), 0)
    gs = []
    for g in range(N_GROUPS):
        blk = sel[g * GROUP_SIZE:(g + 1) * GROUP_SIZE, :]
        m1 = jnp.max(blk, axis=0, keepdims=True)
        first = jnp.min(jnp.where(blk == m1, sub, GROUP_SIZE), axis=0, keepdims=True)
        m2 = jnp.max(jnp.where(sub == first, -jnp.inf, blk), axis=0, keepdims=True)
        gs.append(m1 + m2)
    e_idx = lax.broadcasted_iota(jnp.int32, (N_EXPERTS, tm), 0)
    allowed = jnp.zeros((N_EXPERTS, tm), jnp.bool_)
    for g in range(N_GROUPS):
        rank = jnp.zeros((1, tm), jnp.int32)
        for o in range(N_GROUPS):
            if o == g:
                continue
            beats = (gs[o] >= gs[g]) if o < g else (gs[o] > gs[g])
            rank = rank + beats.astype(jnp.int32)
        keep = rank < TOPK_GROUPS
        allowed = jnp.logical_or(allowed, jnp.logical_and(e_idx // GROUP_SIZE == g, keep))
    cand = jnp.where(allowed, sel, -jnp.inf)
    chosen = jnp.zeros((N_EXPERTS, tm), jnp.bool_)
    for _ in range(TOP_K):
        m = jnp.max(cand, axis=0, keepdims=True)
        first = jnp.min(jnp.where(cand == m, e_idx, N_EXPERTS), axis=0, keepdims=True)
        pick = e_idx == first
        chosen = jnp.logical_or(chosen, pick)
        cand = jnp.where(pick, -jnp.inf, cand)
    w = jnp.where(chosen, scores, 0.0)
    return w / jnp.sum(w, axis=0, keepdims=True) * ROUTED_SCALE


def _merge_kernel(x_ref, olat_ref, ofox_ref, omem_ref, wg_ref, wuv_ref, wbm_ref, wbf_ref, wbc_ref,
                  wo_ref, g1_ref, b1_ref, wr_ref, rb_ref, x1_ref, x1b_ref, gate_ref):
    x = x_ref[...]
    xb = x.astype(BF16)
    tm = x.shape[0]
    o_mla = jnp.concatenate(
        [_dot(olat_ref[:, j * 2 * KV_LORA:(j + 1) * 2 * KV_LORA], wuv_ref[j]).astype(BF16)
         for j in range(H_MLA // 2)], axis=-1)

    def gate(n):
        return _sigmoid(_dot(xb, wg_ref[:, n * D_MODEL:(n + 1) * D_MODEL]))

    y = gate(0) * _dot(o_mla, wbm_ref[...])
    y = y + gate(1) * _dot(ofox_ref[...], wbf_ref[...])
    y = y + gate(2) * _dot(omem_ref[...], wbc_ref[...])
    mix = _dot(y.astype(BF16), wo_ref[...])
    x1 = _layer_norm(ALPHA * x + mix, g1_ref[...], b1_ref[...])
    x1_ref[...] = x1
    x1b = x1.astype(BF16)
    x1b_ref[...] = x1b
    gate_t = _route(x1b, wr_ref[...], rb_ref[...])
    gate_t = jnp.concatenate([gate_t, jnp.zeros((LANES - N_EXPERTS, tm), F32)], axis=0)
    gate_ref[...] = gate_t.T


def _merge(x, olat, ofox, omem, wg, wuv, wbm, wbf, wbc, wo, g1, b1, wr_t, rb, tm):
    t = x.shape[0]
    row = lambda n: pl.BlockSpec((tm, n), lambda i: (i, 0))
    ws = (wg, wuv, wbm, wbf, wbc, wo, g1, b1, wr_t, rb)
    return pl.pallas_call(
        _merge_kernel,
        grid=(t // tm,),
        in_specs=[row(D_MODEL), row(H_MLA * KV_LORA), row(H_FOX * HD_FOX), row(H_MEM * HD_MEM)]
        + [_full(w.shape) for w in ws],
        out_specs=(row(D_MODEL), row(D_MODEL), row(LANES)),
        out_shape=(jax.ShapeDtypeStruct((t, D_MODEL), F32), jax.ShapeDtypeStruct((t, D_MODEL), BF16),
                   jax.ShapeDtypeStruct((t, LANES), F32)),
        compiler_params=_params(1),
    )(x, olat, ofox, omem, *ws)


EXPERTS_PER_STEP = 4


def _swiglu(xb, wg, wu):
    a = _dot(xb, wg)
    return (a * _sigmoid(a)) * _dot(xb, wu)


def _moe_kernel(x1_ref, x1b_ref, gate_ref, weg_ref, weu_ref, wed_ref, wsg_ref, wsu_ref, wsd_ref,
                g2_ref, b2_ref, y_ref, acc_ref):
    step = pl.program_id(1)
    xb = x1b_ref[...]
    tm = xb.shape[0]

    @pl.when(step == 0)
    def _():
        hid = _swiglu(xb, wsg_ref[...], wsu_ref[...])
        acc_ref[...] = _dot(hid.astype(BF16), wsd_ref[...])

    gate = gate_ref[...]
    lane = lax.broadcasted_iota(jnp.int32, (tm, LANES), 1)
    for j in range(EXPERTS_PER_STEP):
        e = step * EXPERTS_PER_STEP + j
        g_col = jnp.sum(jnp.where(lane == e, gate, 0.0), axis=-1, keepdims=True)
        hid = _swiglu(xb, weg_ref[j], weu_ref[j])
        acc_ref[...] += g_col * _dot(hid.astype(BF16), wed_ref[j])

    @pl.when(step == pl.num_programs(1) - 1)
    def _():
        y_ref[...] = _layer_norm(ALPHA * x1_ref[...] + acc_ref[...], g2_ref[...], b2_ref[...])


def _moe(x1, x1b, gate, weg, weu, wed, wsg, wsu, wsd, g2, b2, tm):
    t = x1.shape[0]
    row = lambda n: pl.BlockSpec((tm, n), lambda i, s: (i, 0))
    eb = EXPERTS_PER_STEP
    return pl.pallas_call(
        _moe_kernel,
        grid=(t // tm, N_EXPERTS // eb),
        in_specs=[row(D_MODEL), row(D_MODEL), row(LANES),
                  pl.BlockSpec((eb, D_MODEL, D_EXPERT), lambda i, s: (s, 0, 0)),
                  pl.BlockSpec((eb, D_MODEL, D_EXPERT), lambda i, s: (s, 0, 0)),
                  pl.BlockSpec((eb, D_EXPERT, D_MODEL), lambda i, s: (s, 0, 0)),
                  _full(wsg.shape), _full(wsu.shape), _full(wsd.shape), _full(g2.shape),
                  _full(b2.shape)],
        out_specs=row(D_MODEL),
        out_shape=jax.ShapeDtypeStruct((t, D_MODEL), F32),
        scratch_shapes=[pltpu.VMEM((tm, D_MODEL), F32)],
        compiler_params=_params(2),
    )(x1, x1b, gate, weg, weu, wed, wsg, wsu, wsd, g2, b2)


def _rot_half_cols(w):
    half = w.shape[-1] // 2
    return jnp.concatenate([-w[..., half:], w[..., :half]], axis=-1)


def _block_diag2(a, b):
    za = jnp.zeros((a.shape[0], b.shape[1]), a.dtype)
    zb = jnp.zeros((b.shape[0], a.shape[1]), a.dtype)
    return jnp.concatenate([jnp.concatenate([a, za], 1), jnp.concatenate([zb, b], 1)], 0)


def _prep_weights(w_in, b_f, q_norm_g, w_q_up, kv_norm_g, w_kv_up):
    widths = (Q_LORA, KV_LORA, ROPE_DIM, H_FOX * HD_FOX, H_FOX * HD_FOX, H_FOX * HD_FOX, H_FOX,
              H_MEM * HD_MEM, N_BRANCH * D_MODEL)
    offs = np.cumsum((0,) + widths)
    w_ql, w_ckv, w_kr, w_fq, w_fk, w_fv, w_fl, w_mq, w_g = (
        w_in[:, offs[i]:offs[i + 1]] for i in range(len(widths)))
    rep = LANES // ROPE_DIM
    w_main = jnp.concatenate(
        [w_ql, w_ckv, jnp.tile(w_kr, (1, rep)), jnp.tile(_rot_half_cols(w_kr), (1, rep)),
         w_fq, w_fk, w_fv, w_mq, jnp.pad(w_fl, ((0, 0), (0, LANES - H_FOX)))], axis=1).astype(BF16)
    assert w_main.shape[1] == C_END
    wq3 = w_q_up.reshape(Q_LORA, H_MLA, NOPE_DIM + ROPE_DIM)
    wq_nope = wq3[:, :, :NOPE_DIM].reshape(Q_LORA, H_MLA * NOPE_DIM)
    wq_rope = wq3[:, :, NOPE_DIM:]
    w_q = jnp.concatenate(
        [wq_nope, wq_rope.reshape(Q_LORA, -1), _rot_half_cols(wq_rope).reshape(Q_LORA, -1)],
        axis=1).astype(BF16)
    wkv3 = w_kv_up.reshape(KV_LORA, H_MLA, NOPE_DIM + V_DIM)
    w_uk = wkv3[:, :, :NOPE_DIM]
    w_uv = wkv3[:, :, NOPE_DIM:]
    w_abs = jnp.stack([_block_diag2(w_uk[:, 2 * j].T, w_uk[:, 2 * j + 1].T)
                       for j in range(H_MLA // 2)]).astype(BF16)
    w_uvp = jnp.stack([_block_diag2(w_uv[:, 2 * j], w_uv[:, 2 * j + 1])
                       for j in range(H_MLA // 2)]).astype(BF16)
    b_fp = jnp.pad(b_f, (0, LANES - H_FOX)).reshape(1, LANES)
    return (w_main, w_q, w_abs, q_norm_g.reshape(1, -1), kv_norm_g.reshape(1, -1), b_fp,
            w_g.astype(BF16), w_uvp)


def _rope_tables(pos):
    half = ROPE_DIM // 2
    inv_freq = ROPE_THETA ** (-jnp.arange(half, dtype=F32) / half)
    ang = pos.astype(F32)[:, None] * inv_freq[None, :]
    rep = 2 * LANES // ROPE_DIM
    cos = jnp.tile(jnp.concatenate([jnp.cos(ang)] * 2, axis=-1), (1, rep))
    sin = jnp.tile(jnp.concatenate([jnp.sin(ang)] * 2, axis=-1), (1, rep))
    return cos, sin


def _round_up(n, m):
    return (n + m - 1) // m * m


def kernel(x_prompt, x_sample, mem_prompt, cache_mla_latent, cache_mla_krope, cache_fox_k,
           cache_fox_v, cache_fox_logf, cache_mem_k, cache_mem_v, ln1_g, ln1_b, w_in, b_f,
           q_norm_g, w_q_up, kv_norm_g, w_kv_up, w_mem_k, w_mem_v, w_br_mla, w_br_fox,
           w_br_mem, w_out, ln2_g, ln2_b, w_router, router_bias, w_e_gate, w_e_up, w_e_down,
           w_s_gate, w_s_up, w_s_down):
    assert w_in.shape[0] == DEPTH == 1
    batch, seq, _ = x_prompt.shape
    dec_batch, dec_seq, _ = x_sample.shape
    past = cache_mla_latent.shape[2]
    n_mem = mem_prompt.shape[1]
    l = 0

    (w_main, w_q, w_abs, q_g, kv_g, b_fp, w_gate, w_uvp) = _prep_weights(
        w_in[l], b_f[l], q_norm_g[l], w_q_up[l], kv_norm_g[l], w_kv_up[l])
    merge_w = (w_gate, w_uvp, w_br_mla[l].astype(BF16), w_br_fox[l].astype(BF16),
               w_br_mem[l].astype(BF16), w_out[l].astype(BF16), ln1_g[l].reshape(1, -1),
               ln1_b[l].reshape(1, -1), w_router[l].T.astype(BF16), router_bias[l].reshape(-1, 1))
    moe_w = (w_e_gate[l].astype(BF16), w_e_up[l].astype(BF16), w_e_down[l].astype(BF16),
             w_s_gate[l].astype(BF16), w_s_up[l].astype(BF16), w_s_down[l].astype(BF16),
             ln2_g[l].reshape(1, -1), ln2_b[l].reshape(1, -1))

    def tail(x, olat, ofox, omem, tm):
        x1, x1b, gate = _merge(x, olat, ofox, omem, *merge_w, tm)
        return _moe(x1, x1b, gate, *moe_w, tm)

    tp = batch * seq
    xp = x_prompt.reshape(tp, D_MODEL)
    tm_p = min(512, seq)
    tq_p = min(256, seq)
    tk_p = min(256, seq)
    cos_p, sin_p = _rope_tables(jnp.arange(seq, dtype=jnp.int32))
    (qcat, kcat, lat, latb, kr, fq, fk, fv, fkb, fvb, mq, logf, logfp) = _proj(
        xp, w_main, w_q, w_abs, q_g, kv_g, b_fp, cos_p, sin_p, tm_p)
    tmem = batch * n_mem
    m_k, m_v, m_kb, m_vb = _memkv(mem_prompt.reshape(tmem, D_MODEL), w_mem_k[l].astype(BF16),
                                  w_mem_v[l].astype(BF16), min(512, tmem))
    c_p = _cumsum(logfp.reshape(batch, seq, LANES))
    c_row = jnp.swapaxes(c_p[:, :, :H_FOX], 1, 2)
    olat = _mla_attention(qcat, kcat.reshape(batch, seq, -1), latb.reshape(batch, seq, -1),
                          batch, seq, tq_p, tk_p, seq, 0)
    ofox = _fox_attention(fq, fkb.reshape(batch, seq, -1), fvb.reshape(batch, seq, -1), c_p, c_row,
                          batch, seq, tq_p, tk_p, seq, 0)
    omem = _mem_attention(mq, m_kb.reshape(batch, n_mem, -1), m_vb.reshape(batch, n_mem, -1),
                          batch, seq, tq_p)
    y_p = tail(xp, olat, ofox, omem, tm_p).reshape(batch, seq, D_MODEL)

    ts = dec_batch * dec_seq
    xs = x_sample.reshape(ts, D_MODEL)
    cos_s, sin_s = _rope_tables(jnp.tile(past + jnp.arange(dec_seq, dtype=jnp.int32), dec_batch))
    (qcat_s, kcat_s, lat_s, latb_s, kr_s, fq_s, fk_s, fv_s, fkb_s, fvb_s, mq_s, logf_s,
     logfp_s) = _proj(xs, w_main, w_q, w_abs, q_g, kv_g, b_fp, cos_s, sin_s, ts)
    kv_len = past + dec_seq
    tk_s = 256
    sk = _round_up(kv_len, tk_s)

    def with_cache(cache, new):
        new = new.reshape(dec_batch, dec_seq, -1)
        pad = jnp.zeros((dec_batch, sk - kv_len, new.shape[-1]), new.dtype)
        return jnp.concatenate([cache.astype(new.dtype), new, pad], axis=1)

    rep = LANES // ROPE_DIM
    kcat_all = with_cache(
        jnp.concatenate([cache_mla_latent[l], jnp.tile(cache_mla_krope[l], (1, 1, rep))], axis=-1),
        kcat_s)
    lat_all = with_cache(cache_mla_latent[l], latb_s)
    fk_all = with_cache(cache_fox_k[l].reshape(dec_batch, past, -1), fkb_s)
    fv_all = with_cache(cache_fox_v[l].reshape(dec_batch, past, -1), fvb_s)
    logf_all = with_cache(jnp.pad(cache_fox_logf[l], ((0, 0), (0, 0), (0, LANES - H_FOX))), logfp_s)
    c_s = _cumsum(logf_all)
    c_row_s = jnp.swapaxes(c_s[:, :, :H_FOX], 1, 2)
    c_q_s = c_s[:, past:kv_len]
    olat_s = _mla_attention(qcat_s, kcat_all, lat_all, dec_batch, dec_seq, dec_seq, tk_s, kv_len, past)
    ofox_s = _fox_attention(fq_s, fk_all, fv_all, c_q_s, c_row_s, dec_batch, dec_seq, dec_seq, tk_s,
                            kv_len, past)
    omem_s = _mem_attention(mq_s, cache_mem_k[l].reshape(dec_batch, n_mem, -1).astype(BF16),
                            cache_mem_v[l].reshape(dec_batch, n_mem, -1).astype(BF16),
                            dec_batch, dec_seq, dec_seq)
    y_s = tail(xs, olat_s, ofox_s, omem_s, ts).reshape(dec_batch, dec_seq, D_MODEL)

    def stack(a, b, s, *tail_shape):
        return a.reshape(1, b, s, *tail_shape)

    return (y_p, y_s,
            stack(lat, batch, seq, KV_LORA), stack(kr, batch, seq, ROPE_DIM),
            stack(fk, batch, seq, H_FOX, HD_FOX), stack(fv, batch, seq, H_FOX, HD_FOX),
            stack(logf, batch, seq, H_FOX),
            stack(m_k, batch, n_mem, H_MEM, HD_MEM), stack(m_v, batch, n_mem, H_MEM, HD_MEM),
            stack(lat_s, dec_batch, dec_seq, KV_LORA), stack(kr_s, dec_batch, dec_seq, ROPE_DIM),
            stack(fk_s, dec_batch, dec_seq, H_FOX, HD_FOX), stack(fv_s, dec_batch, dec_seq, H_FOX, HD_FOX),
            stack(logf_s, dec_batch, dec_seq, H_FOX))
```

```python
import functools

import jax
import jax.numpy as jnp
import numpy as np
from jax import lax
from jax.experimental import pallas as pl
from jax.experimental.pallas import tpu as pltpu

F32 = jnp.float32
BF16 = jnp.bfloat16

D_MODEL = 1024
CHUNK = 64
H_MLA = 8
Q_LORA = 256
KV_LORA = 128
NOPE_DIM = 64
ROPE_DIM = 32
V_DIM = 64
ROPE_THETA = 10000.0
MLA_SCALE = (NOPE_DIM + ROPE_DIM) ** -0.5
H_FOX = 8
HD_FOX = 64
FOX_SCALE = HD_FOX ** -0.5
H_MEM = 4
HD_MEM = 128
MEM_SCALE = HD_MEM ** -0.5
N_BRANCH = 3
N_EXPERTS = 64
TOP_K = 8
N_GROUPS = 8
GROUP_SIZE = N_EXPERTS // N_GROUPS
TOPK_GROUPS = 4
D_EXPERT = 256
ROUTED_SCALE = 2.5
DEPTH = 1
ALPHA = (2 * DEPTH) ** 0.25
NORM_EPS = 1e-5
NEG_INF = -1e30

LANES = 128
VMEM_LIMIT = 56 * 1024 * 1024

C_QLAT = 0
C_CKV = C_QLAT + Q_LORA
C_KR = C_CKV + KV_LORA
C_KRR = C_KR + LANES
C_FQ = C_KRR + LANES
C_FK = C_FQ + H_FOX * HD_FOX
C_FV = C_FK + H_FOX * HD_FOX
C_MQ = C_FV + H_FOX * HD_FOX
C_FL = C_MQ + H_MEM * HD_MEM
C_END = C_FL + LANES


def _params(n_grid_axes):
    return pltpu.CompilerParams(
        dimension_semantics=("arbitrary",) * n_grid_axes, vmem_limit_bytes=VMEM_LIMIT)


def _full(shape):
    return pl.BlockSpec(shape, lambda *_: (0,) * len(shape))


def _dot(a, b):
    return jnp.dot(a, b, preferred_element_type=F32)


def _dot_nt(a, b):
    return lax.dot_general(a, b, (((1,), (1,)), ((), ())), preferred_element_type=F32)


def _sigmoid(x):
    return 1.0 / (1.0 + jnp.exp(-x))


def _rms(x, g):
    ms = jnp.mean(x * x, axis=-1, keepdims=True)
    return x * lax.rsqrt(ms + NORM_EPS) * g


def _layer_norm(x, g, b):
    mu = jnp.mean(x, axis=-1, keepdims=True)
    xc = x - mu
    var = jnp.mean(xc * xc, axis=-1, keepdims=True)
    return xc * lax.rsqrt(var + NORM_EPS) * g + b


def _proj_kernel(x_ref, w_ref, wq_ref, wabs_ref, qg_ref, kvg_ref, bf_ref, cos_ref, sin_ref,
                 qcat_ref, kcat_ref, lat_ref, latb_ref, kr_ref, fq_ref, fk_ref, fv_ref,
                 fkb_ref, fvb_ref, mq_ref, logf_ref, logfp_ref):
    xb = x_ref[...].astype(BF16)
    tm = xb.shape[0]

    def proj(c0, width):
        return _dot(xb, w_ref[:, c0:c0 + width])

    cos = cos_ref[...]
    sin = sin_ref[...]
    qn = _rms(proj(C_QLAT, Q_LORA), qg_ref[...]).astype(BF16)
    q2 = _dot(qn, wq_ref[...])
    nq = H_MLA * NOPE_DIM
    nr = H_MLA * ROPE_DIM
    q_rope = (q2[:, nq:nq + nr] * cos + q2[:, nq + nr:nq + 2 * nr] * sin) * MLA_SCALE
    lane = lax.broadcasted_iota(jnp.int32, (tm, LANES), 1)
    heads_per_group = LANES // ROPE_DIM
    for j in range(H_MLA // 2):
        q_abs = _dot(q2[:, j * LANES:(j + 1) * LANES].astype(BF16), wabs_ref[j]) * MLA_SCALE
        for half in range(2):
            h = 2 * j + half
            grp = h // heads_per_group
            qr = q_rope[:, grp * LANES:(grp + 1) * LANES]
            qr = jnp.where(lane // ROPE_DIM == h % heads_per_group, qr, 0.0)
            qcat_ref[h, :, 0:LANES] = q_abs[:, half * LANES:(half + 1) * LANES].astype(BF16)
            qcat_ref[h, :, LANES:2 * LANES] = qr.astype(BF16)
    lat = _rms(proj(C_CKV, KV_LORA), kvg_ref[...])
    lat_ref[...] = lat
    ones = jnp.ones((tm, LANES), BF16)
    latb_ref[:, 0:KV_LORA] = lat.astype(BF16)
    latb_ref[:, KV_LORA:KV_LORA + LANES] = ones
    kr = proj(C_KR, LANES) * cos[:, :LANES] + proj(C_KRR, LANES) * sin[:, :LANES]
    kr_ref[...] = kr[:, :ROPE_DIM]
    kcat_ref[:, 0:KV_LORA] = lat.astype(BF16)
    kcat_ref[:, KV_LORA:KV_LORA + LANES] = kr.astype(BF16)
    fq_ref[...] = (proj(C_FQ, H_FOX * HD_FOX) * FOX_SCALE).astype(BF16)
    fk = proj(C_FK, H_FOX * HD_FOX)
    fk_ref[...] = fk
    fkb_ref[...] = fk.astype(BF16)
    fv = proj(C_FV, H_FOX * HD_FOX)
    fv_ref[...] = fv
    for j in range(H_FOX * HD_FOX // LANES):
        fvb_ref[:, 2 * j * LANES:(2 * j + 1) * LANES] = fv[:, j * LANES:(j + 1) * LANES].astype(BF16)
        fvb_ref[:, (2 * j + 1) * LANES:(2 * j + 2) * LANES] = ones
    mq_ref[...] = proj(C_MQ, H_MEM * HD_MEM).astype(BF16)
    z = proj(C_FL, LANES) + bf_ref[...]
    log_f = jnp.minimum(z, 0.0) - jnp.log1p(jnp.exp(-jnp.abs(z)))
    logfp_ref[...] = log_f
    logf_ref[...] = log_f[:, :H_FOX]


def _proj(x, w_main, w_q, w_abs, q_g, kv_g, b_f, cos_t, sin_t, tm):
    t = x.shape[0]
    n_tab = cos_t.shape[0] // tm
    row = lambda w: pl.BlockSpec((tm, w), lambda i: (i, 0))
    tab = pl.BlockSpec((tm, 2 * LANES), lambda i: (i % n_tab, 0))
    hw = H_FOX * HD_FOX
    out_shape = (
        jax.ShapeDtypeStruct((H_MLA, t, 2 * LANES), BF16),
        jax.ShapeDtypeStruct((t, 2 * LANES), BF16),
        jax.ShapeDtypeStruct((t, KV_LORA), F32),
        jax.ShapeDtypeStruct((t, KV_LORA + LANES), BF16),
        jax.ShapeDtypeStruct((t, ROPE_DIM), F32),
        jax.ShapeDtypeStruct((t, hw), BF16),
        jax.ShapeDtypeStruct((t, hw), F32),
        jax.ShapeDtypeStruct((t, hw), F32),
        jax.ShapeDtypeStruct((t, hw), BF16),
        jax.ShapeDtypeStruct((t, 2 * hw), BF16),
        jax.ShapeDtypeStruct((t, H_MEM * HD_MEM), BF16),
        jax.ShapeDtypeStruct((t, H_FOX), F32),
        jax.ShapeDtypeStruct((t, LANES), F32),
    )
    out_specs = (
        pl.BlockSpec((H_MLA, tm, 2 * LANES), lambda i: (0, i, 0)),
        row(2 * LANES), row(KV_LORA), row(KV_LORA + LANES), row(ROPE_DIM), row(hw), row(hw), row(hw),
        row(hw), row(2 * hw), row(H_MEM * HD_MEM), row(H_FOX), row(LANES),
    )
    return pl.pallas_call(
        _proj_kernel,
        grid=(t // tm,),
        in_specs=[row(D_MODEL), _full(w_main.shape), _full(w_q.shape), _full(w_abs.shape),
                  _full(q_g.shape), _full(kv_g.shape), _full(b_f.shape), tab, tab],
        out_specs=out_specs,
        out_shape=out_shape,
        compiler_params=_params(1),
    )(x, w_main, w_q, w_abs, q_g, kv_g, b_f, cos_t, sin_t)


def _memkv_kernel(x_ref, wk_ref, wv_ref, k_ref, v_ref, kb_ref, vb_ref):
    xb = x_ref[...].astype(BF16)
    k = _dot(xb, wk_ref[...])
    v = _dot(xb, wv_ref[...])
    k_ref[...] = k
    v_ref[...] = v
    kb_ref[...] = k.astype(BF16)
    vb_ref[...] = v.astype(BF16)


def _memkv(x, wk, wv, tm):
    t = x.shape[0]
    w = H_MEM * HD_MEM
    row = lambda n: pl.BlockSpec((tm, n), lambda i: (i, 0))
    return pl.pallas_call(
        _memkv_kernel,
        grid=(t // tm,),
        in_specs=[row(D_MODEL), _full(wk.shape), _full(wv.shape)],
        out_specs=(row(w), row(w), row(w), row(w)),
        out_shape=(jax.ShapeDtypeStruct((t, w), F32), jax.ShapeDtypeStruct((t, w), F32),
                   jax.ShapeDtypeStruct((t, w), BF16), jax.ShapeDtypeStruct((t, w), BF16)),
        compiler_params=_params(1),
    )(x, wk, wv)


CUM_BLOCK = 256


def _cumsum_kernel(x_ref, c_ref):
    s = x_ref.shape[1]
    r = lax.broadcasted_iota(jnp.int32, (CUM_BLOCK, CUM_BLOCK), 0)
    c = lax.broadcasted_iota(jnp.int32, (CUM_BLOCK, CUM_BLOCK), 1)
    tri = jnp.where(r >= c, 1.0, 0.0).astype(BF16)

    def body(i, carry):
        r0 = pl.multiple_of(i * CUM_BLOCK, CUM_BLOCK)
        x = x_ref[0, pl.ds(r0, CUM_BLOCK), :]
        hi = x.astype(BF16)
        r1 = x - hi.astype(F32)
        mid = r1.astype(BF16)
        lo = (r1 - mid.astype(F32)).astype(BF16)
        cs = _dot(tri, hi) + _dot(tri, mid) + _dot(tri, lo) + carry
        c_ref[0, pl.ds(r0, CUM_BLOCK), :] = cs
        return cs[CUM_BLOCK - 1:CUM_BLOCK, :]

    lax.fori_loop(0, s // CUM_BLOCK, body, jnp.zeros((1, LANES), F32))


def _cumsum(x):
    b, s, _ = x.shape
    spec = pl.BlockSpec((1, s, LANES), lambda i: (i, 0, 0))
    return pl.pallas_call(
        _cumsum_kernel, grid=(b,), in_specs=[spec], out_specs=spec,
        out_shape=jax.ShapeDtypeStruct(x.shape, F32), compiler_params=_params(1),
    )(x)


def _online_softmax_step(s, v_ones, m, acc, shift=None):
    r = jnp.max(s, axis=-1, keepdims=True)
    m_new = jnp.maximum(m, r if shift is None else r + shift)
    alpha = jnp.exp(m - m_new)
    p = jnp.exp(s - (m_new if shift is None else m_new - shift))
    acc = alpha * acc + _dot(p.astype(BF16), v_ones)
    return m_new, acc


def _normalised(acc):
    return acc[:, :LANES] / acc[:, LANES:]


def _kv_tile_counts(first_visible_end, last_visible_end, tk, kv_len, sk):
    n_full = jnp.minimum(first_visible_end // tk, kv_len // tk)
    n_any = jnp.minimum((jnp.minimum(last_visible_end, kv_len) + tk - 1) // tk, sk // tk)
    return n_full, n_any


def _kv_loop(n_full, n_any, scores, consume, state):
    def body(kt, st, masked):
        return consume(kt, scores(kt), st, masked)

    state = lax.fori_loop(0, n_full, functools.partial(body, masked=False), state)
    return lax.fori_loop(n_full, n_any, functools.partial(body, masked=True), state)


def _mla_kernel(q_ref, k_ref, v_ref, o_ref, *, tq, tk, kv_len, q_pos0):
    sk = k_ref.shape[1]
    rows = H_MLA * tq
    q_start = q_pos0 + pl.program_id(1) * tq
    n_full, n_any = _kv_tile_counts((q_start // CHUNK + 1) * CHUNK,
                                    ((q_start + tq - 1) // CHUNK + 1) * CHUNK, tk, kv_len, sk)
    q = q_ref[...].reshape(rows, q_ref.shape[-1])
    q_chunk = (q_start + lax.broadcasted_iota(jnp.int32, (rows, 1), 0) % tq) // CHUNK

    def scores(kt):
        return _dot_nt(q, k_ref[0, pl.ds(pl.multiple_of(kt * tk, tk), tk), :])

    def consume(kt, s, state, masked):
        k0 = pl.multiple_of(kt * tk, tk)
        if masked:
            k_pos = k0 + lax.broadcasted_iota(jnp.int32, (1, tk), 1)
            mask = k_pos // CHUNK <= q_chunk
            if kv_len < sk:
                mask = jnp.logical_and(mask, k_pos < kv_len)
            s = jnp.where(mask, s, NEG_INF)
        return _online_softmax_step(s, v_ref[0, pl.ds(k0, tk), :], *state)

    state = (jnp.full((rows, 1), NEG_INF, F32), jnp.zeros((rows, 2 * LANES), F32))
    _, acc = _kv_loop(n_full, n_any, scores, consume, state)
    o = _normalised(acc).astype(BF16)
    for h in range(H_MLA):
        o_ref[:, h * KV_LORA:(h + 1) * KV_LORA] = o[h * tq:(h + 1) * tq, :]


def _mla_attention(qcat, kcat, latb, batch, sq, tq, tk, kv_len, q_pos0):
    sk = kcat.shape[1]
    nq = sq // tq
    kern = functools.partial(_mla_kernel, tq=tq, tk=tk, kv_len=kv_len, q_pos0=q_pos0)
    return pl.pallas_call(
        kern,
        grid=(batch, nq),
        in_specs=[pl.BlockSpec((H_MLA, tq, 2 * LANES), lambda b, i: (0, b * nq + i, 0)),
                  pl.BlockSpec((1, sk, 2 * LANES), lambda b, i: (b, 0, 0)),
                  pl.BlockSpec((1, sk, 2 * LANES), lambda b, i: (b, 0, 0))],
        out_specs=pl.BlockSpec((tq, H_MLA * KV_LORA), lambda b, i: (b * nq + i, 0)),
        out_shape=jax.ShapeDtypeStruct((batch * sq, H_MLA * KV_LORA), BF16),
        compiler_params=_params(2),
    )(qcat, kcat, latb)


def _fox_kernel(q_ref, k_ref, v_ref, cq_ref, ck_ref, o_ref, *, tq, tk, kv_len, q_pos0):
    sk = k_ref.shape[1]
    q_start = q_pos0 + pl.program_id(1) * tq
    n_full, n_any = _kv_tile_counts(q_start + 1, q_start + tq, tk, kv_len, sk)
    lane = lax.broadcasted_iota(jnp.int32, (tq, LANES), 1)
    n_pairs = H_FOX * HD_FOX // LANES
    q_pos = q_start + lax.broadcasted_iota(jnp.int32, (2 * tq, 1), 0) % tq
    q2, c_q = [], []
    for j in range(n_pairs):
        q_pair = q_ref[:, j * LANES:(j + 1) * LANES]
        zero = jnp.zeros_like(q_pair)
        q2.append(jnp.concatenate([jnp.where(lane < HD_FOX, q_pair, zero),
                                   jnp.where(lane < HD_FOX, zero, q_pair)], axis=0))
        c_q.append(jnp.concatenate(
            [jnp.sum(jnp.where(lane == 2 * j + half, cq_ref[0], 0.0), axis=-1, keepdims=True)
             for half in range(2)], axis=0))

    def scores(kt):
        k0 = pl.multiple_of(kt * tk, tk)
        out = []
        for j in range(n_pairs):
            s = _dot_nt(q2[j], k_ref[0, pl.ds(k0, tk), j * LANES:(j + 1) * LANES])
            c_k = jnp.concatenate(
                [jnp.broadcast_to(ck_ref[0, 2 * j:2 * j + 1, pl.ds(k0, tk)], (tq, tk)),
                 jnp.broadcast_to(ck_ref[0, 2 * j + 1:2 * j + 2, pl.ds(k0, tk)], (tq, tk))], axis=0)
            out.append(s - c_k)
        return tuple(out)

    def consume(kt, s, state, masked):
        k0 = pl.multiple_of(kt * tk, tk)
        if masked:
            k_pos = k0 + lax.broadcasted_iota(jnp.int32, (1, tk), 1)
            mask = k_pos <= q_pos
            if kv_len < sk:
                mask = jnp.logical_and(mask, k_pos < kv_len)
        out = []
        for j in range(n_pairs):
            s_j = jnp.where(mask, s[j], NEG_INF) if masked else s[j]
            v_ones = v_ref[0, pl.ds(k0, tk), 2 * j * LANES:2 * (j + 1) * LANES]
            out.append(_online_softmax_step(s_j, v_ones, *state[j], shift=c_q[j]))
        return tuple(out)

    state = tuple((jnp.full((2 * tq, 1), NEG_INF, F32), jnp.zeros((2 * tq, 2 * LANES), F32))
                  for _ in range(n_pairs))
    state = _kv_loop(n_full, n_any, scores, consume, state)
    for j in range(n_pairs):
        o = _normalised(state[j][1])
        o_ref[:, j * LANES:(j + 1) * LANES] = jnp.where(lane < HD_FOX, o[:tq], o[tq:]).astype(BF16)


def _fox_attention(fq, fkb, fvb, c_q, c_k, batch, sq, tq, tk, kv_len, q_pos0):
    sk = fkb.shape[1]
    nq = sq // tq
    w = H_FOX * HD_FOX
    kern = functools.partial(_fox_kernel, tq=tq, tk=tk, kv_len=kv_len, q_pos0=q_pos0)
    return pl.pallas_call(
        kern,
        grid=(batch, nq),
        in_specs=[pl.BlockSpec((tq, w), lambda b, i: (b * nq + i, 0)),
                  pl.BlockSpec((1, sk, w), lambda b, i: (b, 0, 0)),
                  pl.BlockSpec((1, sk, 2 * w), lambda b, i: (b, 0, 0)),
                  pl.BlockSpec((1, tq, LANES), lambda b, i: (b, i, 0)),
                  pl.BlockSpec((1, H_FOX, sk), lambda b, i: (b, 0, 0))],
        out_specs=pl.BlockSpec((tq, w), lambda b, i: (b * nq + i, 0)),
        out_shape=jax.ShapeDtypeStruct((batch * sq, w), BF16),
        compiler_params=_params(2),
    )(fq, fkb, fvb, c_q, c_k)


def _mem_kernel(q_ref, k_ref, v_ref, o_ref):
    for h in range(H_MEM):
        cols = slice(h * HD_MEM, (h + 1) * HD_MEM)
        s = _dot_nt(q_ref[:, cols], k_ref[0, :, cols]) * MEM_SCALE
        m = jnp.max(s, axis=-1, keepdims=True)
        p = jnp.exp(s - m)
        l = jnp.sum(p, axis=-1, keepdims=True)
        o = _dot(p.astype(BF16), v_ref[0, :, cols]) / l
        o_ref[:, cols] = o.astype(BF16)


def _mem_attention(mq, mkb, mvb, batch, sq, tq):
    nq = sq // tq
    n_mem = mkb.shape[1]
    w = H_MEM * HD_MEM
    return pl.pallas_call(
        _mem_kernel,
        grid=(batch, nq),
        in_specs=[pl.BlockSpec((tq, w), lambda b, i: (b * nq + i, 0)),
                  pl.BlockSpec((1, n_mem, w), lambda b, i: (b, 0, 0)),
                  pl.BlockSpec((1, n_mem, w), lambda b, i: (b, 0, 0))],
        out_specs=pl.BlockSpec((tq, w), lambda b, i: (b * nq + i, 0)),
        out_shape=jax.ShapeDtypeStruct((batch * sq, w), BF16),
        compiler_params=_params(2),
    )(mq, mkb, mvb)


def _route(x1b, wr_t, bias):
    tm = x1b.shape[0]
    scores = _sigmoid(_dot_nt(wr_t, x1b))
    sel = scores + bias
    sub = lax.broadcasted_iota(jnp.int32, (GROUP_SIZE, tm), 0)
    gs = []
    for g in range(N_GROUPS):
        blk = sel[g * GROUP_SIZE:(g + 1) * GROUP_SIZE, :]
        m1 = jnp.max(blk, axis=0, keepdims=True)
        first = jnp.min(jnp.where(blk == m1, sub, GROUP_SIZE), axis=0, keepdims=True)
        m2 = jnp.max(jnp.where(sub == first, -jnp.inf, blk), axis=0, keepdims=True)
        gs.append(m1 + m2)
    e_idx = lax.broadcasted_iota(jnp.int32, (N_EXPERTS, tm), 0)
    allowed = jnp.zeros((N_EXPERTS, tm), jnp.bool_)
    for g in range(N_GROUPS):
        rank = jnp.zeros((1, tm), jnp.int32)
        for o in range(N_GROUPS):
            if o == g:
                continue
            beats = (gs[o] >= gs[g]) if o < g else (gs[o] > gs[g])
            rank = rank + beats.astype(jnp.int32)
        keep = rank < TOPK_GROUPS
        allowed = jnp.logical_or(allowed, jnp.logical_and(e_idx // GROUP_SIZE == g, keep))
    cand = jnp.where(allowed, sel, -jnp.inf)
    chosen = jnp.zeros((N_EXPERTS, tm), jnp.bool_)
    for _ in range(TOP_K):
        m = jnp.max(cand, axis=0, keepdims=True)
        first = jnp.min(jnp.where(cand == m, e_idx, N_EXPERTS), axis=0, keepdims=True)
        pick = e_idx == first
        chosen = jnp.logical_or(chosen, pick)
        cand = jnp.where(pick, -jnp.inf, cand)
    w = jnp.where(chosen, scores, 0.0)
    return w / jnp.sum(w, axis=0, keepdims=True) * ROUTED_SCALE


def _merge_kernel(x_ref, olat_ref, ofox_ref, omem_ref, wg_ref, wuv_ref, wbm_ref, wbf_ref, wbc_ref,
                  wo_ref, g1_ref, b1_ref, wr_ref, rb_ref, x1_ref, x1b_ref, gate_ref):
    x = x_ref[...]
    xb = x.astype(BF16)
    tm = x.shape[0]
    o_mla = jnp.concatenate(
        [_dot(olat_ref[:, j * 2 * KV_LORA:(j + 1) * 2 * KV_LORA], wuv_ref[j]).astype(BF16)
         for j in range(H_MLA // 2)], axis=-1)

    def gate(n):
        return _sigmoid(_dot(xb, wg_ref[:, n * D_MODEL:(n + 1) * D_MODEL]))

    y = gate(0) * _dot(o_mla, wbm_ref[...])
    y = y + gate(1) * _dot(ofox_ref[...], wbf_ref[...])
    y = y + gate(2) * _dot(omem_ref[...], wbc_ref[...])
    mix = _dot(y.astype(BF16), wo_ref[...])
    x1 = _layer_norm(ALPHA * x + mix, g1_ref[...], b1_ref[...])
    x1_ref[...] = x1
    x1b = x1.astype(BF16)
    x1b_ref[...] = x1b
    gate_t = _route(x1b, wr_ref[...], rb_ref[...])
    gate_t = jnp.concatenate([gate_t, jnp.zeros((LANES - N_EXPERTS, tm), F32)], axis=0)
    gate_ref[...] = gate_t.T


def _merge(x, olat, ofox, omem, wg, wuv, wbm, wbf, wbc, wo, g1, b1, wr_t, rb, tm):
    t = x.shape[0]
    row = lambda n: pl.BlockSpec((tm, n), lambda i: (i, 0))
    ws = (wg, wuv, wbm, wbf, wbc, wo, g1, b1, wr_t, rb)
    return pl.pallas_call(
        _merge_kernel,
        grid=(t // tm,),
        in_specs=[row(D_MODEL), row(H_MLA * KV_LORA), row(H_FOX * HD_FOX), row(H_MEM * HD_MEM)]
        + [_full(w.shape) for w in ws],
        out_specs=(row(D_MODEL), row(D_MODEL), row(LANES)),
        out_shape=(jax.ShapeDtypeStruct((t, D_MODEL), F32), jax.ShapeDtypeStruct((t, D_MODEL), BF16),
                   jax.ShapeDtypeStruct((t, LANES), F32)),
        compiler_params=_params(1),
    )(x, olat, ofox, omem, *ws)


EXPERTS_PER_STEP = 4


def _swiglu(xb, wg, wu):
    a = _dot(xb, wg)
    return (a * _sigmoid(a)) * _dot(xb, wu)


def _moe_kernel(x1_ref, x1b_ref, gate_ref, weg_ref, weu_ref, wed_ref, wsg_ref, wsu_ref, wsd_ref,
                g2_ref, b2_ref, y_ref, acc_ref):
    step = pl.program_id(1)
    xb = x1b_ref[...]
    tm = xb.shape[0]

    @pl.when(step == 0)
    def _():
        hid = _swiglu(xb, wsg_ref[...], wsu_ref[...])
        acc_ref[...] = _dot(hid.astype(BF16), wsd_ref[...])

    gate = gate_ref[...]
    lane = lax.broadcasted_iota(jnp.int32, (tm, LANES), 1)
    for j in range(EXPERTS_PER_STEP):
        e = step * EXPERTS_PER_STEP + j
        g_col = jnp.sum(jnp.where(lane == e, gate, 0.0), axis=-1, keepdims=True)
        hid = _swiglu(xb, weg_ref[j], weu_ref[j])
        acc_ref[...] += g_col * _dot(hid.astype(BF16), wed_ref[j])

    @pl.when(step == pl.num_programs(1) - 1)
    def _():
        y_ref[...] = _layer_norm(ALPHA * x1_ref[...] + acc_ref[...], g2_ref[...], b2_ref[...])


def _moe(x1, x1b, gate, weg, weu, wed, wsg, wsu, wsd, g2, b2, tm):
    t = x1.shape[0]
    row = lambda n: pl.BlockSpec((tm, n), lambda i, s: (i, 0))
    eb = EXPERTS_PER_STEP
    return pl.pallas_call(
        _moe_kernel,
        grid=(t // tm, N_EXPERTS // eb),
        in_specs=[row(D_MODEL), row(D_MODEL), row(LANES),
                  pl.BlockSpec((eb, D_MODEL, D_EXPERT), lambda i, s: (s, 0, 0)),
                  pl.BlockSpec((eb, D_MODEL, D_EXPERT), lambda i, s: (s, 0, 0)),
                  pl.BlockSpec((eb, D_EXPERT, D_MODEL), lambda i, s: (s, 0, 0)),
                  _full(wsg.shape), _full(wsu.shape), _full(wsd.shape), _full(g2.shape),
                  _full(b2.shape)],
        out_specs=row(D_MODEL),
        out_shape=jax.ShapeDtypeStruct((t, D_MODEL), F32),
        scratch_shapes=[pltpu.VMEM((tm, D_MODEL), F32)],
        compiler_params=_params(2),
    )(x1, x1b, gate, weg, weu, wed, wsg, wsu, wsd, g2, b2)


def _rot_half_cols(w):
    half = w.shape[-1] // 2
    return jnp.concatenate([-w[..., half:], w[..., :half]], axis=-1)


def _block_diag2(a, b):
    za = jnp.zeros((a.shape[0], b.shape[1]), a.dtype)
    zb = jnp.zeros((b.shape[0], a.shape[1]), a.dtype)
    return jnp.concatenate([jnp.concatenate([a, za], 1), jnp.concatenate([zb, b], 1)], 0)


def _prep_weights(w_in, b_f, q_norm_g, w_q_up, kv_norm_g, w_kv_up):
    widths = (Q_LORA, KV_LORA, ROPE_DIM, H_FOX * HD_FOX, H_FOX * HD_FOX, H_FOX * HD_FOX, H_FOX,
              H_MEM * HD_MEM, N_BRANCH * D_MODEL)
    offs = np.cumsum((0,) + widths)
    w_ql, w_ckv, w_kr, w_fq, w_fk, w_fv, w_fl, w_mq, w_g = (
        w_in[:, offs[i]:offs[i + 1]] for i in range(len(widths)))
    rep = LANES // ROPE_DIM
    w_main = jnp.concatenate(
        [w_ql, w_ckv, jnp.tile(w_kr, (1, rep)), jnp.tile(_rot_half_cols(w_kr), (1, rep)),
         w_fq, w_fk, w_fv, w_mq, jnp.pad(w_fl, ((0, 0), (0, LANES - H_FOX)))], axis=1).astype(BF16)
    assert w_main.shape[1] == C_END
    wq3 = w_q_up.reshape(Q_LORA, H_MLA, NOPE_DIM + ROPE_DIM)
    wq_nope = wq3[:, :, :NOPE_DIM].reshape(Q_LORA, H_MLA * NOPE_DIM)
    wq_rope = wq3[:, :, NOPE_DIM:]
    w_q = jnp.concatenate(
        [wq_nope, wq_rope.reshape(Q_LORA, -1), _rot_half_cols(wq_rope).reshape(Q_LORA, -1)],
        axis=1).astype(BF16)
    wkv3 = w_kv_up.reshape(KV_LORA, H_MLA, NOPE_DIM + V_DIM)
    w_uk = wkv3[:, :, :NOPE_DIM]
    w_uv = wkv3[:, :, NOPE_DIM:]
    w_abs = jnp.stack([_block_diag2(w_uk[:, 2 * j].T, w_uk[:, 2 * j + 1].T)
                       for j in range(H_MLA // 2)]).astype(BF16)
    w_uvp = jnp.stack([_block_diag2(w_uv[:, 2 * j], w_uv[:, 2 * j + 1])
                       for j in range(H_MLA // 2)]).astype(BF16)
    b_fp = jnp.pad(b_f, (0, LANES - H_FOX)).reshape(1, LANES)
    return (w_main, w_q, w_abs, q_norm_g.reshape(1, -1), kv_norm_g.reshape(1, -1), b_fp,
            w_g.astype(BF16), w_uvp)


def _rope_tables(pos):
    half = ROPE_DIM // 2
    inv_freq = ROPE_THETA ** (-jnp.arange(half, dtype=F32) / half)
    ang = pos.astype(F32)[:, None] * inv_freq[None, :]
    rep = 2 * LANES // ROPE_DIM
    cos = jnp.tile(jnp.concatenate([jnp.cos(ang)] * 2, axis=-1), (1, rep))
    sin = jnp.tile(jnp.concatenate([jnp.sin(ang)] * 2, axis=-1), (1, rep))
    return cos, sin


def _round_up(n, m):
    return (n + m - 1) // m * m


def kernel(x_prompt, x_sample, mem_prompt, cache_mla_latent, cache_mla_krope, cache_fox_k,
           cache_fox_v, cache_fox_logf, cache_mem_k, cache_mem_v, ln1_g, ln1_b, w_in, b_f,
           q_norm_g, w_q_up, kv_norm_g, w_kv_up, w_mem_k, w_mem_v, w_br_mla, w_br_fox,
           w_br_mem, w_out, ln2_g, ln2_b, w_router, router_bias, w_e_gate, w_e_up, w_e_down,
           w_s_gate, w_s_up, w_s_down):
    assert w_in.shape[0] == DEPTH == 1
    batch, seq, _ = x_prompt.shape
    dec_batch, dec_seq, _ = x_sample.shape
    past = cache_mla_latent.shape[2]
    n_mem = mem_prompt.shape[1]
    l = 0

    (w_main, w_q, w_abs, q_g, kv_g, b_fp, w_gate, w_uvp) = _prep_weights(
        w_in[l], b_f[l], q_norm_g[l], w_q_up[l], kv_norm_g[l], w_kv_up[l])
    merge_w = (w_gate, w_uvp, w_br_mla[l].astype(BF16), w_br_fox[l].astype(BF16),
               w_br_mem[l].astype(BF16), w_out[l].astype(BF16), ln1_g[l].reshape(1, -1),
               ln1_b[l].reshape(1, -1), w_router[l].T.astype(BF16), router_bias[l].reshape(-1, 1))
    moe_w = (w_e_gate[l].astype(BF16), w_e_up[l].astype(BF16), w_e_down[l].astype(BF16),
             w_s_gate[l].astype(BF16), w_s_up[l].astype(BF16), w_s_down[l].astype(BF16),
             ln2_g[l].reshape(1, -1), ln2_b[l].reshape(1, -1))

    def tail(x, olat, ofox, omem, tm):
        x1, x1b, gate = _merge(x, olat, ofox, omem, *merge_w, tm)
        return _moe(x1, x1b, gate, *moe_w, tm)

    tp = batch * seq
    xp = x_prompt.reshape(tp, D_MODEL)
    tm_p = min(512, seq)
    tq_p = min(256, seq)
    tk_p = min(512, seq)
    cos_p, sin_p = _rope_tables(jnp.arange(seq, dtype=jnp.int32))
    (qcat, kcat, lat, latb, kr, fq, fk, fv, fkb, fvb, mq, logf, logfp) = _proj(
        xp, w_main, w_q, w_abs, q_g, kv_g, b_fp, cos_p, sin_p, tm_p)
    tmem = batch * n_mem
    m_k, m_v, m_kb, m_vb = _memkv(mem_prompt.reshape(tmem, D_MODEL), w_mem_k[l].astype(BF16),
                                  w_mem_v[l].astype(BF16), min(512, tmem))
    c_p = _cumsum(logfp.reshape(batch, seq, LANES))
    c_row = jnp.swapaxes(c_p[:, :, :H_FOX], 1, 2)
    olat = _mla_attention(qcat, kcat.reshape(batch, seq, -1), latb.reshape(batch, seq, -1),
                          batch, seq, tq_p, tk_p, seq, 0)
    ofox = _fox_attention(fq, fkb.reshape(batch, seq, -1), fvb.reshape(batch, seq, -1), c_p, c_row,
                          batch, seq, tq_p, tk_p, seq, 0)
    omem = _mem_attention(mq, m_kb.reshape(batch, n_mem, -1), m_vb.reshape(batch, n_mem, -1),
                          batch, seq, tq_p)
    y_p = tail(xp, olat, ofox, omem, tm_p).reshape(batch, seq, D_MODEL)

    ts = dec_batch * dec_seq
    xs = x_sample.reshape(ts, D_MODEL)
    cos_s, sin_s = _rope_tables(jnp.tile(past + jnp.arange(dec_seq, dtype=jnp.int32), dec_batch))
    (qcat_s, kcat_s, lat_s, latb_s, kr_s, fq_s, fk_s, fv_s, fkb_s, fvb_s, mq_s, logf_s,
     logfp_s) = _proj(xs, w_main, w_q, w_abs, q_g, kv_g, b_fp, cos_s, sin_s, ts)
    kv_len = past + dec_seq
    tk_s = 256
    sk = _round_up(kv_len, tk_s)

    def with_cache(cache, new):
        new = new.reshape(dec_batch, dec_seq, -1)
        pad = jnp.zeros((dec_batch, sk - kv_len, new.shape[-1]), new.dtype)
        return jnp.concatenate([cache.astype(new.dtype), new, pad], axis=1)

    rep = LANES // ROPE_DIM
    kcat_all = with_cache(
        jnp.concatenate([cache_mla_latent[l], jnp.tile(cache_mla_krope[l], (1, 1, rep))], axis=-1),
        kcat_s)
    lat_all = with_cache(
        jnp.concatenate([cache_mla_latent[l], jnp.ones((dec_batch, past, LANES), F32)], axis=-1), latb_s)
    fk_all = with_cache(cache_fox_k[l].reshape(dec_batch, past, -1), fkb_s)
    n_pairs = H_FOX * HD_FOX // LANES
    fv_pairs = cache_fox_v[l].reshape(dec_batch, past, n_pairs, LANES)
    fv_all = with_cache(
        jnp.concatenate([fv_pairs, jnp.ones_like(fv_pairs)], axis=-1).reshape(dec_batch, past, -1),
        fvb_s)
    logf_all = with_cache(jnp.pad(cache_fox_logf[l], ((0, 0), (0, 0), (0, LANES - H_FOX))), logfp_s)
    c_s = _cumsum(logf_all)
    c_row_s = jnp.swapaxes(c_s[:, :, :H_FOX], 1, 2)
    c_q_s = c_s[:, past:kv_len]
    olat_s = _mla_attention(qcat_s, kcat_all, lat_all, dec_batch, dec_seq, dec_seq, tk_s, kv_len, past)
    ofox_s = _fox_attention(fq_s, fk_all, fv_all, c_q_s, c_row_s, dec_batch, dec_seq, dec_seq, tk_s,
                            kv_len, past)
    omem_s = _mem_attention(mq_s, cache_mem_k[l].reshape(dec_batch, n_mem, -1).astype(BF16),
                            cache_mem_v[l].reshape(dec_batch, n_mem, -1).astype(BF16),
                            dec_batch, dec_seq, dec_seq)
    y_s = tail(xs, olat_s, ofox_s, omem_s, ts).reshape(dec_batch, dec_seq, D_MODEL)

    def stack(a, b, s, *tail_shape):
        return a.reshape(1, b, s, *tail_shape)

    return (y_p, y_s,
            stack(lat, batch, seq, KV_LORA), stack(kr, batch, seq, ROPE_DIM),
            stack(fk, batch, seq, H_FOX, HD_FOX), stack(fv, batch, seq, H_FOX, HD_FOX),
            stack(logf, batch, seq, H_FOX),
            stack(m_k, batch, n_mem, H_MEM, HD_MEM), stack(m_v, batch, n_mem, H_MEM, HD_MEM),
            stack(lat_s, dec_batch, dec_seq, KV_LORA), stack(kr_s, dec_batch, dec_seq, ROPE_DIM),
            stack(fk_s, dec_batch, dec_seq, H_FOX, HD_FOX), stack(fv_s, dec_batch, dec_seq, H_FOX, HD_FOX),
            stack(logf_s, dec_batch, dec_seq, H_FOX))
```

```python
import functools

import jax
import jax.numpy as jnp
import numpy as np
from jax import lax
from jax.experimental import pallas as pl
from jax.experimental.pallas import tpu as pltpu

F32 = jnp.float32
BF16 = jnp.bfloat16

D_MODEL = 1024
CHUNK = 64
H_MLA = 8
Q_LORA = 256
KV_LORA = 128
NOPE_DIM = 64
ROPE_DIM = 32
V_DIM = 64
ROPE_THETA = 10000.0
MLA_SCALE = (NOPE_DIM + ROPE_DIM) ** -0.5
H_FOX = 8
HD_FOX = 64
FOX_SCALE = HD_FOX ** -0.5
H_MEM = 4
HD_MEM = 128
MEM_SCALE = HD_MEM ** -0.5
N_BRANCH = 3
N_EXPERTS = 64
TOP_K = 8
N_GROUPS = 8
GROUP_SIZE = N_EXPERTS // N_GROUPS
TOPK_GROUPS = 4
D_EXPERT = 256
ROUTED_SCALE = 2.5
DEPTH = 1
ALPHA = (2 * DEPTH) ** 0.25
NORM_EPS = 1e-5
NEG_INF = -1e30

LANES = 128
VMEM_LIMIT = 56 * 1024 * 1024

C_QLAT = 0
C_CKV = C_QLAT + Q_LORA
C_KR = C_CKV + KV_LORA
C_KRR = C_KR + LANES
C_FQ = C_KRR + LANES
C_FK = C_FQ + H_FOX * HD_FOX
C_FV = C_FK + H_FOX * HD_FOX
C_MQ = C_FV + H_FOX * HD_FOX
C_FL = C_MQ + H_MEM * HD_MEM
C_END = C_FL + LANES


def _params(n_grid_axes):
    return pltpu.CompilerParams(
        dimension_semantics=("arbitrary",) * n_grid_axes, vmem_limit_bytes=VMEM_LIMIT)


def _full(shape):
    return pl.BlockSpec(shape, lambda *_: (0,) * len(shape))


def _dot(a, b):
    return jnp.dot(a, b, preferred_element_type=F32)


def _dot_nt(a, b):
    return lax.dot_general(a, b, (((1,), (1,)), ((), ())), preferred_element_type=F32)


def _sigmoid(x):
    return 1.0 / (1.0 + jnp.exp(-x))


def _rms(x, g):
    ms = jnp.mean(x * x, axis=-1, keepdims=True)
    return x * lax.rsqrt(ms + NORM_EPS) * g


def _layer_norm(x, g, b):
    mu = jnp.mean(x, axis=-1, keepdims=True)
    xc = x - mu
    var = jnp.mean(xc * xc, axis=-1, keepdims=True)
    return xc * lax.rsqrt(var + NORM_EPS) * g + b


def _proj_kernel(x_ref, w_ref, wq_ref, wabs_ref, qg_ref, kvg_ref, bf_ref, cos_ref, sin_ref,
                 qcat_ref, kcat_ref, lat_ref, latb_ref, kr_ref, fq_ref, fk_ref, fv_ref,
                 fkb_ref, fvb_ref, mq_ref, logf_ref, logfp_ref):
    xb = x_ref[...].astype(BF16)
    tm = xb.shape[0]

    def proj(c0, width):
        return _dot(xb, w_ref[:, c0:c0 + width])

    cos = cos_ref[...]
    sin = sin_ref[...]
    qn = _rms(proj(C_QLAT, Q_LORA), qg_ref[...]).astype(BF16)
    q2 = _dot(qn, wq_ref[...])
    nq = H_MLA * NOPE_DIM
    nr = H_MLA * ROPE_DIM
    q_rope = (q2[:, nq:nq + nr] * cos + q2[:, nq + nr:nq + 2 * nr] * sin) * MLA_SCALE
    lane = lax.broadcasted_iota(jnp.int32, (tm, LANES), 1)
    heads_per_group = LANES // ROPE_DIM
    for j in range(H_MLA // 2):
        q_abs = _dot(q2[:, j * LANES:(j + 1) * LANES].astype(BF16), wabs_ref[j]) * MLA_SCALE
        for half in range(2):
            h = 2 * j + half
            grp = h // heads_per_group
            qr = q_rope[:, grp * LANES:(grp + 1) * LANES]
            qr = jnp.where(lane // ROPE_DIM == h % heads_per_group, qr, 0.0)
            qcat_ref[h, :, 0:LANES] = q_abs[:, half * LANES:(half + 1) * LANES].astype(BF16)
            qcat_ref[h, :, LANES:2 * LANES] = qr.astype(BF16)
    lat = _rms(proj(C_CKV, KV_LORA), kvg_ref[...])
    lat_ref[...] = lat
    ones = jnp.ones((tm, LANES), BF16)
    latb_ref[:, 0:KV_LORA] = lat.astype(BF16)
    latb_ref[:, KV_LORA:KV_LORA + LANES] = ones
    kr = proj(C_KR, LANES) * cos[:, :LANES] + proj(C_KRR, LANES) * sin[:, :LANES]
    kr_ref[...] = kr[:, :ROPE_DIM]
    kcat_ref[:, 0:KV_LORA] = lat.astype(BF16)
    kcat_ref[:, KV_LORA:KV_LORA + LANES] = kr.astype(BF16)
    fq_ref[...] = (proj(C_FQ, H_FOX * HD_FOX) * FOX_SCALE).astype(BF16)
    fk = proj(C_FK, H_FOX * HD_FOX)
    fk_ref[...] = fk
    fkb_ref[...] = fk.astype(BF16)
    fv = proj(C_FV, H_FOX * HD_FOX)
    fv_ref[...] = fv
    for j in range(H_FOX * HD_FOX // LANES):
        fvb_ref[:, 2 * j * LANES:(2 * j + 1) * LANES] = fv[:, j * LANES:(j + 1) * LANES].astype(BF16)
        fvb_ref[:, (2 * j + 1) * LANES:(2 * j + 2) * LANES] = ones
    mq_ref[...] = proj(C_MQ, H_MEM * HD_MEM).astype(BF16)
    z = proj(C_FL, LANES) + bf_ref[...]
    log_f = jnp.minimum(z, 0.0) - jnp.log1p(jnp.exp(-jnp.abs(z)))
    logfp_ref[...] = log_f
    logf_ref[...] = log_f[:, :H_FOX]


def _proj(x, w_main, w_q, w_abs, q_g, kv_g, b_f, cos_t, sin_t, tm):
    t = x.shape[0]
    n_tab = cos_t.shape[0] // tm
    row = lambda w: pl.BlockSpec((tm, w), lambda i: (i, 0))
    tab = pl.BlockSpec((tm, 2 * LANES), lambda i: (i % n_tab, 0))
    hw = H_FOX * HD_FOX
    out_shape = (
        jax.ShapeDtypeStruct((H_MLA, t, 2 * LANES), BF16),
        jax.ShapeDtypeStruct((t, 2 * LANES), BF16),
        jax.ShapeDtypeStruct((t, KV_LORA), F32),
        jax.ShapeDtypeStruct((t, KV_LORA + LANES), BF16),
        jax.ShapeDtypeStruct((t, ROPE_DIM), F32),
        jax.ShapeDtypeStruct((t, hw), BF16),
        jax.ShapeDtypeStruct((t, hw), F32),
        jax.ShapeDtypeStruct((t, hw), F32),
        jax.ShapeDtypeStruct((t, hw), BF16),
        jax.ShapeDtypeStruct((t, 2 * hw), BF16),
        jax.ShapeDtypeStruct((t, H_MEM * HD_MEM), BF16),
        jax.ShapeDtypeStruct((t, H_FOX), F32),
        jax.ShapeDtypeStruct((t, LANES), F32),
    )
    out_specs = (
        pl.BlockSpec((H_MLA, tm, 2 * LANES), lambda i: (0, i, 0)),
        row(2 * LANES), row(KV_LORA), row(KV_LORA + LANES), row(ROPE_DIM), row(hw), row(hw), row(hw),
        row(hw), row(2 * hw), row(H_MEM * HD_MEM), row(H_FOX), row(LANES),
    )
    return pl.pallas_call(
        _proj_kernel,
        grid=(t // tm,),
        in_specs=[row(D_MODEL), _full(w_main.shape), _full(w_q.shape), _full(w_abs.shape),
                  _full(q_g.shape), _full(kv_g.shape), _full(b_f.shape), tab, tab],
        out_specs=out_specs,
        out_shape=out_shape,
        compiler_params=_params(1),
    )(x, w_main, w_q, w_abs, q_g, kv_g, b_f, cos_t, sin_t)


def _memkv_kernel(x_ref, wk_ref, wv_ref, k_ref, v_ref, kb_ref, vb_ref):
    xb = x_ref[...].astype(BF16)
    k = _dot(xb, wk_ref[...])
    v = _dot(xb, wv_ref[...])
    k_ref[...] = k
    v_ref[...] = v
    kb_ref[...] = k.astype(BF16)
    vb_ref[...] = v.astype(BF16)


def _memkv(x, wk, wv, tm):
    t = x.shape[0]
    w = H_MEM * HD_MEM
    row = lambda n: pl.BlockSpec((tm, n), lambda i: (i, 0))
    return pl.pallas_call(
        _memkv_kernel,
        grid=(t // tm,),
        in_specs=[row(D_MODEL), _full(wk.shape), _full(wv.shape)],
        out_specs=(row(w), row(w), row(w), row(w)),
        out_shape=(jax.ShapeDtypeStruct((t, w), F32), jax.ShapeDtypeStruct((t, w), F32),
                   jax.ShapeDtypeStruct((t, w), BF16), jax.ShapeDtypeStruct((t, w), BF16)),
        compiler_params=_params(1),
    )(x, wk, wv)


CUM_BLOCK = 256


def _cumsum_kernel(x_ref, c_ref):
    s = x_ref.shape[1]
    r = lax.broadcasted_iota(jnp.int32, (CUM_BLOCK, CUM_BLOCK), 0)
    c = lax.broadcasted_iota(jnp.int32, (CUM_BLOCK, CUM_BLOCK), 1)
    tri = jnp.where(r >= c, 1.0, 0.0).astype(BF16)

    def body(i, carry):
        r0 = pl.multiple_of(i * CUM_BLOCK, CUM_BLOCK)
        x = x_ref[0, pl.ds(r0, CUM_BLOCK), :]
        hi = x.astype(BF16)
        r1 = x - hi.astype(F32)
        mid = r1.astype(BF16)
        lo = (r1 - mid.astype(F32)).astype(BF16)
        cs = _dot(tri, hi) + _dot(tri, mid) + _dot(tri, lo) + carry
        c_ref[0, pl.ds(r0, CUM_BLOCK), :] = cs
        return cs[CUM_BLOCK - 1:CUM_BLOCK, :]

    lax.fori_loop(0, s // CUM_BLOCK, body, jnp.zeros((1, LANES), F32))


def _cumsum(x):
    b, s, _ = x.shape
    spec = pl.BlockSpec((1, s, LANES), lambda i: (i, 0, 0))
    return pl.pallas_call(
        _cumsum_kernel, grid=(b,), in_specs=[spec], out_specs=spec,
        out_shape=jax.ShapeDtypeStruct(x.shape, F32), compiler_params=_params(1),
    )(x)


def _online_softmax_step(s, v_ones, m, acc, shift=None):
    r = jnp.max(s, axis=-1, keepdims=True)
    m_new = jnp.maximum(m, r if shift is None else r + shift)
    alpha = jnp.exp(m - m_new)
    p = jnp.exp(s - (m_new if shift is None else m_new - shift))
    acc = alpha * acc + _dot(p.astype(BF16), v_ones)
    return m_new, acc


def _normalised(acc):
    return acc[:, :LANES] / acc[:, LANES:]


def _kv_tile_counts(first_visible_end, last_visible_end, tk, kv_len, sk):
    n_full = jnp.minimum(first_visible_end // tk, kv_len // tk)
    n_any = jnp.minimum((jnp.minimum(last_visible_end, kv_len) + tk - 1) // tk, sk // tk)
    return n_full, n_any


def _kv_loop(n_full, n_any, scores, consume, state):
    def body(kt, st, masked):
        return consume(kt, scores(kt), st, masked)

    state = lax.fori_loop(0, n_full, functools.partial(body, masked=False), state)
    return lax.fori_loop(n_full, n_any, functools.partial(body, masked=True), state)


def _mla_kernel(q_ref, k_ref, v_ref, o_ref, *, tq, tk, kv_len, q_pos0):
    sk = k_ref.shape[1]
    rows = H_MLA * tq
    q_start = q_pos0 + pl.program_id(1) * tq
    n_full, n_any = _kv_tile_counts((q_start // CHUNK + 1) * CHUNK,
                                    ((q_start + tq - 1) // CHUNK + 1) * CHUNK, tk, kv_len, sk)
    q = q_ref[...].reshape(rows, q_ref.shape[-1])
    q_chunk = (q_start + lax.broadcasted_iota(jnp.int32, (rows, 1), 0) % tq) // CHUNK

    def scores(kt):
        return _dot_nt(q, k_ref[0, pl.ds(pl.multiple_of(kt * tk, tk), tk), :])

    def consume(kt, s, state, masked):
        k0 = pl.multiple_of(kt * tk, tk)
        if masked:
            k_pos = k0 + lax.broadcasted_iota(jnp.int32, (1, tk), 1)
            mask = k_pos // CHUNK <= q_chunk
            if kv_len < sk:
                mask = jnp.logical_and(mask, k_pos < kv_len)
            s = jnp.where(mask, s, NEG_INF)
        return _online_softmax_step(s, v_ref[0, pl.ds(k0, tk), :], *state)

    state = (jnp.full((rows, 1), NEG_INF, F32), jnp.zeros((rows, 2 * LANES), F32))
    _, acc = _kv_loop(n_full, n_any, scores, consume, state)
    o = _normalised(acc).astype(BF16)
    for h in range(H_MLA):
        o_ref[:, h * KV_LORA:(h + 1) * KV_LORA] = o[h * tq:(h + 1) * tq, :]


def _mla_attention(qcat, kcat, latb, batch, sq, tq, tk, kv_len, q_pos0):
    sk = kcat.shape[1]
    nq = sq // tq
    kern = functools.partial(_mla_kernel, tq=tq, tk=tk, kv_len=kv_len, q_pos0=q_pos0)
    return pl.pallas_call(
        kern,
        grid=(batch, nq),
        in_specs=[pl.BlockSpec((H_MLA, tq, 2 * LANES), lambda b, i: (0, b * nq + i, 0)),
                  pl.BlockSpec((1, sk, 2 * LANES), lambda b, i: (b, 0, 0)),
                  pl.BlockSpec((1, sk, 2 * LANES), lambda b, i: (b, 0, 0))],
        out_specs=pl.BlockSpec((tq, H_MLA * KV_LORA), lambda b, i: (b * nq + i, 0)),
        out_shape=jax.ShapeDtypeStruct((batch * sq, H_MLA * KV_LORA), BF16),
        compiler_params=_params(2),
    )(qcat, kcat, latb)


def _fox_kernel(q_ref, k_ref, v_ref, cq_ref, ck_ref, o_ref, *, tq, tk, kv_len, q_pos0):
    sk = k_ref.shape[1]
    q_start = q_pos0 + pl.program_id(1) * tq
    n_full, n_any = _kv_tile_counts(q_start + 1, q_start + tq, tk, kv_len, sk)
    lane = lax.broadcasted_iota(jnp.int32, (tq, LANES), 1)
    n_pairs = H_FOX * HD_FOX // LANES
    q_pos = q_start + lax.broadcasted_iota(jnp.int32, (2 * tq, 1), 0) % tq
    q2, c_q = [], []
    for j in range(n_pairs):
        q_pair = q_ref[:, j * LANES:(j + 1) * LANES]
        zero = jnp.zeros_like(q_pair)
        q2.append(jnp.concatenate([jnp.where(lane < HD_FOX, q_pair, zero),
                                   jnp.where(lane < HD_FOX, zero, q_pair)], axis=0))
        c_q.append(jnp.concatenate(
            [jnp.sum(jnp.where(lane == 2 * j + half, cq_ref[0], 0.0), axis=-1, keepdims=True)
             for half in range(2)], axis=0))

    def scores(kt):
        k0 = pl.multiple_of(kt * tk, tk)
        out = []
        for j in range(n_pairs):
            s = _dot_nt(q2[j], k_ref[0, pl.ds(k0, tk), j * LANES:(j + 1) * LANES])
            c_k = jnp.concatenate(
                [jnp.broadcast_to(ck_ref[0, 2 * j:2 * j + 1, pl.ds(k0, tk)], (tq, tk)),
                 jnp.broadcast_to(ck_ref[0, 2 * j + 1:2 * j + 2, pl.ds(k0, tk)], (tq, tk))], axis=0)
            out.append(s - c_k)
        return tuple(out)

    def consume(kt, s, state, masked):
        k0 = pl.multiple_of(kt * tk, tk)
        if masked:
            k_pos = k0 + lax.broadcasted_iota(jnp.int32, (1, tk), 1)
            mask = k_pos <= q_pos
            if kv_len < sk:
                mask = jnp.logical_and(mask, k_pos < kv_len)
        out = []
        for j in range(n_pairs):
            s_j = jnp.where(mask, s[j], NEG_INF) if masked else s[j]
            v_ones = v_ref[0, pl.ds(k0, tk), 2 * j * LANES:2 * (j + 1) * LANES]
            out.append(_online_softmax_step(s_j, v_ones, *state[j], shift=c_q[j]))
        return tuple(out)

    state = tuple((jnp.full((2 * tq, 1), NEG_INF, F32), jnp.zeros((2 * tq, 2 * LANES), F32))
                  for _ in range(n_pairs))
    state = _kv_loop(n_full, n_any, scores, consume, state)
    for j in range(n_pairs):
        o = _normalised(state[j][1])
        o_ref[:, j * LANES:(j + 1) * LANES] = jnp.where(lane < HD_FOX, o[:tq], o[tq:]).astype(BF16)


def _fox_attention(fq, fkb, fvb, c_q, c_k, batch, sq, tq, tk, kv_len, q_pos0):
    sk = fkb.shape[1]
    nq = sq // tq
    w = H_FOX * HD_FOX
    kern = functools.partial(_fox_kernel, tq=tq, tk=tk, kv_len=kv_len, q_pos0=q_pos0)
    return pl.pallas_call(
        kern,
        grid=(batch, nq),
        in_specs=[pl.BlockSpec((tq, w), lambda b, i: (b * nq + i, 0)),
                  pl.BlockSpec((1, sk, w), lambda b, i: (b, 0, 0)),
                  pl.BlockSpec((1, sk, 2 * w), lambda b, i: (b, 0, 0)),
                  pl.BlockSpec((1, tq, LANES), lambda b, i: (b, i, 0)),
                  pl.BlockSpec((1, H_FOX, sk), lambda b, i: (b, 0, 0))],
        out_specs=pl.BlockSpec((tq, w), lambda b, i: (b * nq + i, 0)),
        out_shape=jax.ShapeDtypeStruct((batch * sq, w), BF16),
        compiler_params=_params(2),
    )(fq, fkb, fvb, c_q, c_k)


def _mem_kernel(q_ref, k_ref, v_ref, o_ref):
    for h in range(H_MEM):
        cols = slice(h * HD_MEM, (h + 1) * HD_MEM)
        s = _dot_nt(q_ref[:, cols], k_ref[0, :, cols]) * MEM_SCALE
        m = jnp.max(s, axis=-1, keepdims=True)
        p = jnp.exp(s - m)
        l = jnp.sum(p, axis=-1, keepdims=True)
        o = _dot(p.astype(BF16), v_ref[0, :, cols]) / l
        o_ref[:, cols] = o.astype(BF16)


def _mem_attention(mq, mkb, mvb, batch, sq, tq):
    nq = sq // tq
    n_mem = mkb.shape[1]
    w = H_MEM * HD_MEM
    return pl.pallas_call(
        _mem_kernel,
        grid=(batch, nq),
        in_specs=[pl.BlockSpec((tq, w), lambda b, i: (b * nq + i, 0)),
                  pl.BlockSpec((1, n_mem, w), lambda b, i: (b, 0, 0)),
                  pl.BlockSpec((1, n_mem, w), lambda b, i: (b, 0, 0))],
        out_specs=pl.BlockSpec((tq, w), lambda b, i: (b * nq + i, 0)),
        out_shape=jax.ShapeDtypeStruct((batch * sq, w), BF16),
        compiler_params=_params(2),
    )(mq, mkb, mvb)


def _route(x1b, wr_t, bias):
    tm = x1b.shape[0]
    scores = _sigmoid(_dot_nt(wr_t, x1b))
    sel = scores + bias
    sub = lax.broadcasted_iota(jnp.int32, (GROUP_SIZE, tm), 0)
    gs = []
    for g in range(N_GROUPS):
        blk = sel[g * GROUP_SIZE:(g + 1) * GROUP_SIZE, :]
        m1 = jnp.max(blk, axis=0, keepdims=True)
        first = jnp.min(jnp.where(blk == m1, sub, GROUP_SIZE), axis=0, keepdims=True)
        m2 = jnp.max(jnp.where(sub == first, -jnp.inf, blk), axis=0, keepdims=True)
        gs.append(m1 + m2)
    e_idx = lax.broadcasted_iota(jnp.int32, (N_EXPERTS, tm), 0)
    allowed = jnp.zeros((N_EXPERTS, tm), jnp.bool_)
    for g in range(N_GROUPS):
        rank = jnp.zeros((1, tm), jnp.int32)
        for o in range(N_GROUPS):
            if o == g:
                continue
            beats = (gs[o] >= gs[g]) if o < g else (gs[o] > gs[g])
            rank = rank + beats.astype(jnp.int32)
        keep = rank < TOPK_GROUPS
        allowed = jnp.logical_or(allowed, jnp.logical_and(e_idx // GROUP_SIZE == g, keep))
    cand = jnp.where(allowed, sel, -jnp.inf)
    chosen = jnp.zeros((N_EXPERTS, tm), jnp.bool_)
    for _ in range(TOP_K):
        m = jnp.max(cand, axis=0, keepdims=True)
        first = jnp.min(jnp.where(cand == m, e_idx, N_EXPERTS), axis=0, keepdims=True)
        pick = e_idx == first
        chosen = jnp.logical_or(chosen, pick)
        cand = jnp.where(pick, -jnp.inf, cand)
    w = jnp.where(chosen, scores, 0.0)
    return w / jnp.sum(w, axis=0, keepdims=True) * ROUTED_SCALE, chosen


def _swiglu(xb, wg, wu):
    a = _dot(xb, wg)
    return (a * _sigmoid(a)) * _dot(xb, wu)


def _merge_kernel(x_ref, olat_ref, ofox_ref, omem_ref, wg_ref, wuv_ref, wbm_ref, wbf_ref, wbc_ref,
                  wo_ref, g1_ref, b1_ref, wsg_ref, wsu_ref, wsd_ref, x1b_ref, base_ref):
    x = x_ref[...]
    xb = x.astype(BF16)
    o_mla = jnp.concatenate(
        [_dot(olat_ref[:, j * 2 * KV_LORA:(j + 1) * 2 * KV_LORA], wuv_ref[j]).astype(BF16)
         for j in range(H_MLA // 2)], axis=-1)

    def gate(n):
        return _sigmoid(_dot(xb, wg_ref[:, n * D_MODEL:(n + 1) * D_MODEL]))

    y = gate(0) * _dot(o_mla, wbm_ref[...])
    y = y + gate(1) * _dot(ofox_ref[...], wbf_ref[...])
    y = y + gate(2) * _dot(omem_ref[...], wbc_ref[...])
    mix = _dot(y.astype(BF16), wo_ref[...])
    x1 = _layer_norm(ALPHA * x + mix, g1_ref[...], b1_ref[...])
    x1b = x1.astype(BF16)
    x1b_ref[...] = x1b
    shared = _dot(_swiglu(x1b, wsg_ref[...], wsu_ref[...]).astype(BF16), wsd_ref[...])
    base_ref[...] = ALPHA * x1 + shared


def _merge(x, olat, ofox, omem, wg, wuv, wbm, wbf, wbc, wo, g1, b1, wsg, wsu, wsd, tm):
    t = x.shape[0]
    row = lambda n: pl.BlockSpec((tm, n), lambda i: (i, 0))
    ws = (wg, wuv, wbm, wbf, wbc, wo, g1, b1, wsg, wsu, wsd)
    return pl.pallas_call(
        _merge_kernel,
        grid=(t // tm,),
        in_specs=[row(D_MODEL), row(H_MLA * KV_LORA), row(H_FOX * HD_FOX), row(H_MEM * HD_MEM)]
        + [_full(w.shape) for w in ws],
        out_specs=(row(D_MODEL), row(D_MODEL)),
        out_shape=(jax.ShapeDtypeStruct((t, D_MODEL), BF16), jax.ShapeDtypeStruct((t, D_MODEL), F32)),
        compiler_params=_params(1),
    )(x, olat, ofox, omem, *ws)


BF16_ROWS = 16
PLACE_ROWS = 512
EXPERT_ROWS = 2048
EXPERT_CHUNK = 512
EXPERTS_PER_STEP = 4


def _capacity(tm):
    return min(tm, _round_up(2 * tm * TOP_K // N_EXPERTS, BF16_ROWS))


def _dispatch_kernel(x1b_ref, wr_ref, rb_ref, xs_ref, gate_ref, slot_ref, govf_ref, flag_ref,
                     pm_ref, *, cap):
    xb = x1b_ref[...]
    tm = xb.shape[0]
    gate, chosen = _route(xb, wr_ref[...], rb_ref[...])
    r = lax.broadcasted_iota(jnp.int32, (tm, tm), 0)
    c = lax.broadcasted_iota(jnp.int32, (tm, tm), 1)
    earlier = jnp.where(r < c, 1.0, 0.0).astype(BF16)
    rank = _dot(jnp.where(chosen, 1.0, 0.0).astype(BF16), earlier)
    fits = jnp.logical_and(chosen, rank < cap)
    slot = jnp.where(fits, rank, -1.0)
    gate_ref[...] = jnp.where(fits, gate, 0.0)
    slot_ref[...] = slot
    g_ovf = jnp.where(fits, 0.0, gate)
    govf_ref[...] = jnp.concatenate([g_ovf, jnp.zeros((LANES - N_EXPERTS, tm), F32)], axis=0).T
    any_ovf = jnp.max(jnp.max(g_ovf, axis=0, keepdims=True), axis=1, keepdims=True)
    flag_ref[0] = jnp.broadcast_to(jnp.where(any_ovf > 0.0, 1, 0), flag_ref.shape[1:]).astype(jnp.int32)
    s_iota = lax.broadcasted_iota(jnp.int32, (cap, tm), 0).astype(F32)
    for e in range(N_EXPERTS):
        pm_ref[e * cap:(e + 1) * cap, :] = jnp.where(s_iota == slot[e:e + 1, :], 1.0, 0.0).astype(BF16)
    per = PLACE_ROWS // cap
    for g in range(N_EXPERTS // per):
        xs = _dot(pm_ref[g * PLACE_ROWS:(g + 1) * PLACE_ROWS, :], xb).astype(BF16)
        xs_ref[g * per:(g + 1) * per] = xs.reshape(per, cap, D_MODEL)


def _dispatch(x1b, wr_t, rb, tm, cap):
    t = x1b.shape[0]
    nt = t // tm
    col = pl.BlockSpec((N_EXPERTS, tm), lambda i: (0, i))
    return pl.pallas_call(
        functools.partial(_dispatch_kernel, cap=cap),
        grid=(nt,),
        in_specs=[pl.BlockSpec((tm, D_MODEL), lambda i: (i, 0)), _full(wr_t.shape), _full(rb.shape)],
        out_specs=(pl.BlockSpec((N_EXPERTS, cap, D_MODEL), lambda i: (0, i, 0)), col, col,
                   pl.BlockSpec((tm, LANES), lambda i: (i, 0)),
                   pl.BlockSpec((1, 8, LANES), lambda i: (i, 0, 0))),
        out_shape=(jax.ShapeDtypeStruct((N_EXPERTS, nt * cap, D_MODEL), BF16),
                   jax.ShapeDtypeStruct((N_EXPERTS, t), F32),
                   jax.ShapeDtypeStruct((N_EXPERTS, t), F32),
                   jax.ShapeDtypeStruct((t, LANES), F32),
                   jax.ShapeDtypeStruct((nt, 8, LANES), jnp.int32)),
        scratch_shapes=[pltpu.VMEM((N_EXPERTS * cap, tm), BF16)],
        compiler_params=_params(1),
    )(x1b, wr_t, rb)


def _expert_kernel(xs_ref, wg_ref, wu_ref, wd_ref, ys_ref, *, chunk):
    def body(c, carry):
        r0 = pl.multiple_of(c * chunk, chunk)
        hid = _swiglu(xs_ref[0, pl.ds(r0, chunk), :], wg_ref[0], wu_ref[0])
        ys_ref[0, pl.ds(r0, chunk), :] = _dot(hid.astype(BF16), wd_ref[0]).astype(BF16)
        return carry

    lax.fori_loop(0, xs_ref.shape[1] // chunk, body, 0)


def _experts(xs, weg, weu, wed):
    n_e, rows, _ = xs.shape
    rb = min(EXPERT_ROWS, rows)
    blk = pl.BlockSpec((1, rb, D_MODEL), lambda e, i: (e, i, 0))
    return pl.pallas_call(
        functools.partial(_expert_kernel, chunk=min(EXPERT_CHUNK, rb)),
        grid=(n_e, rows // rb),
        in_specs=[blk,
                  pl.BlockSpec((1, D_MODEL, D_EXPERT), lambda e, i: (e, 0, 0)),
                  pl.BlockSpec((1, D_MODEL, D_EXPERT), lambda e, i: (e, 0, 0)),
                  pl.BlockSpec((1, D_EXPERT, D_MODEL), lambda e, i: (e, 0, 0))],
        out_specs=blk,
        out_shape=jax.ShapeDtypeStruct(xs.shape, BF16),
        compiler_params=_params(2),
    )(xs, weg, weu, wed)


def _dense_routed_kernel(flag_ref, x1b_ref, gate_ref, weg_ref, weu_ref, wed_ref, out_ref):
    step = pl.program_id(1)

    @pl.when(step == 0)
    def _():
        out_ref[...] = jnp.zeros_like(out_ref)

    @pl.when(flag_ref[pl.program_id(0)] > 0)
    def _():
        xb = x1b_ref[...]
        gate = gate_ref[...]
        lane = lax.broadcasted_iota(jnp.int32, gate.shape, 1)
        for j in range(EXPERTS_PER_STEP):
            e = step * EXPERTS_PER_STEP + j
            g_col = jnp.sum(jnp.where(lane == e, gate, 0.0), axis=-1, keepdims=True)
            hid = _swiglu(xb, weg_ref[j], weu_ref[j])
            out_ref[...] += g_col * _dot(hid.astype(BF16), wed_ref[j])


def _dense_routed(flags, x1b, gate, weg, weu, wed, tm):
    t = x1b.shape[0]
    eb = EXPERTS_PER_STEP
    row = lambda n: pl.BlockSpec((tm, n), lambda i, s, f: (i, 0))
    wspec = lambda a, b: pl.BlockSpec((eb, a, b), lambda i, s, f: (jnp.where(f[i] > 0, s, 0), 0, 0))
    return pl.pallas_call(
        _dense_routed_kernel,
        grid_spec=pltpu.PrefetchScalarGridSpec(
            num_scalar_prefetch=1,
            grid=(t // tm, N_EXPERTS // eb),
            in_specs=[row(D_MODEL), row(LANES), wspec(D_MODEL, D_EXPERT), wspec(D_MODEL, D_EXPERT),
                      wspec(D_EXPERT, D_MODEL)],
            out_specs=row(D_MODEL)),
        out_shape=jax.ShapeDtypeStruct((t, D_MODEL), F32),
        compiler_params=_params(2),
    )(flags, x1b, gate, weg, weu, wed)


def _combine_kernel(gate_ref, slot_ref, ys_ref, base_ref, rovf_ref, g2_ref, b2_ref, y_ref, pw_ref,
                    *, cap):
    tm = base_ref.shape[0]
    gate = gate_ref[...]
    slot = slot_ref[...]
    s_iota = lax.broadcasted_iota(jnp.int32, (cap, tm), 0).astype(F32)
    for e in range(N_EXPERTS):
        pw_ref[e * cap:(e + 1) * cap, :] = jnp.where(
            s_iota == slot[e:e + 1, :], gate[e:e + 1, :], 0.0).astype(BF16)
    routed = rovf_ref[...]
    per = PLACE_ROWS // cap
    for g in range(N_EXPERTS // per):
        ys = ys_ref[g * per:(g + 1) * per].reshape(PLACE_ROWS, D_MODEL)
        routed = routed + lax.dot_general(pw_ref[g * PLACE_ROWS:(g + 1) * PLACE_ROWS, :], ys,
                                          (((0,), (0,)), ((), ())), preferred_element_type=F32)
    y_ref[...] = _layer_norm(base_ref[...] + routed, g2_ref[...], b2_ref[...])


def _combine(gate_t, slot_t, ys, base, rovf, g2, b2, tm, cap):
    t = base.shape[0]
    col = pl.BlockSpec((N_EXPERTS, tm), lambda i: (0, i))
    row = pl.BlockSpec((tm, D_MODEL), lambda i: (i, 0))
    return pl.pallas_call(
        functools.partial(_combine_kernel, cap=cap),
        grid=(t // tm,),
        in_specs=[col, col, pl.BlockSpec((N_EXPERTS, cap, D_MODEL), lambda i: (0, i, 0)), row, row,
                  _full(g2.shape), _full(b2.shape)],
        out_specs=row,
        out_shape=jax.ShapeDtypeStruct((t, D_MODEL), F32),
        scratch_shapes=[pltpu.VMEM((N_EXPERTS * cap, tm), BF16)],
        compiler_params=_params(1),
    )(gate_t, slot_t, ys, base, rovf, g2, b2)


def _moe_ln2(x1b, base, wr_t, rb, weg, weu, wed, g2, b2, tm):
    t = x1b.shape[0]
    cap = _capacity(tm)
    assert PLACE_ROWS % cap == 0 and (N_EXPERTS * cap) % PLACE_ROWS == 0
    xs, gate_t, slot_t, gate_ovf, flags = _dispatch(x1b, wr_t, rb, tm, cap)
    ys = _experts(xs, weg, weu, wed)
    flags = flags[:, 0, 0]
    rovf = lax.cond(jnp.any(flags > 0),
                    lambda: _dense_routed(flags, x1b, gate_ovf, weg, weu, wed, tm),
                    lambda: jnp.zeros((t, D_MODEL), F32))
    return _combine(gate_t, slot_t, ys, base, rovf, g2, b2, tm, cap)


def _rot_half_cols(w):
    half = w.shape[-1] // 2
    return jnp.concatenate([-w[..., half:], w[..., :half]], axis=-1)


def _block_diag2(a, b):
    za = jnp.zeros((a.shape[0], b.shape[1]), a.dtype)
    zb = jnp.zeros((b.shape[0], a.shape[1]), a.dtype)
    return jnp.concatenate([jnp.concatenate([a, za], 1), jnp.concatenate([zb, b], 1)], 0)


def _prep_weights(w_in, b_f, q_norm_g, w_q_up, kv_norm_g, w_kv_up):
    widths = (Q_LORA, KV_LORA, ROPE_DIM, H_FOX * HD_FOX, H_FOX * HD_FOX, H_FOX * HD_FOX, H_FOX,
              H_MEM * HD_MEM, N_BRANCH * D_MODEL)
    offs = np.cumsum((0,) + widths)
    w_ql, w_ckv, w_kr, w_fq, w_fk, w_fv, w_fl, w_mq, w_g = (
        w_in[:, offs[i]:offs[i + 1]] for i in range(len(widths)))
    rep = LANES // ROPE_DIM
    w_main = jnp.concatenate(
        [w_ql, w_ckv, jnp.tile(w_kr, (1, rep)), jnp.tile(_rot_half_cols(w_kr), (1, rep)),
         w_fq, w_fk, w_fv, w_mq, jnp.pad(w_fl, ((0, 0), (0, LANES - H_FOX)))], axis=1).astype(BF16)
    assert w_main.shape[1] == C_END
    wq3 = w_q_up.reshape(Q_LORA, H_MLA, NOPE_DIM + ROPE_DIM)
    wq_nope = wq3[:, :, :NOPE_DIM].reshape(Q_LORA, H_MLA * NOPE_DIM)
    wq_rope = wq3[:, :, NOPE_DIM:]
    w_q = jnp.concatenate(
        [wq_nope, wq_rope.reshape(Q_LORA, -1), _rot_half_cols(wq_rope).reshape(Q_LORA, -1)],
        axis=1).astype(BF16)
    wkv3 = w_kv_up.reshape(KV_LORA, H_MLA, NOPE_DIM + V_DIM)
    w_uk = wkv3[:, :, :NOPE_DIM]
    w_uv = wkv3[:, :, NOPE_DIM:]
    w_abs = jnp.stack([_block_diag2(w_uk[:, 2 * j].T, w_uk[:, 2 * j + 1].T)
                       for j in range(H_MLA // 2)]).astype(BF16)
    w_uvp = jnp.stack([_block_diag2(w_uv[:, 2 * j], w_uv[:, 2 * j + 1])
                       for j in range(H_MLA // 2)]).astype(BF16)
    b_fp = jnp.pad(b_f, (0, LANES - H_FOX)).reshape(1, LANES)
    return (w_main, w_q, w_abs, q_norm_g.reshape(1, -1), kv_norm_g.reshape(1, -1), b_fp,
            w_g.astype(BF16), w_uvp)


def _rope_tables(pos):
    half = ROPE_DIM // 2
    inv_freq = ROPE_THETA ** (-jnp.arange(half, dtype=F32) / half)
    ang = pos.astype(F32)[:, None] * inv_freq[None, :]
    rep = 2 * LANES // ROPE_DIM
    cos = jnp.tile(jnp.concatenate([jnp.cos(ang)] * 2, axis=-1), (1, rep))
    sin = jnp.tile(jnp.concatenate([jnp.sin(ang)] * 2, axis=-1), (1, rep))
    return cos, sin


def _round_up(n, m):
    return (n + m - 1) // m * m


def kernel(x_prompt, x_sample, mem_prompt, cache_mla_latent, cache_mla_krope, cache_fox_k,
           cache_fox_v, cache_fox_logf, cache_mem_k, cache_mem_v, ln1_g, ln1_b, w_in, b_f,
           q_norm_g, w_q_up, kv_norm_g, w_kv_up, w_mem_k, w_mem_v, w_br_mla, w_br_fox,
           w_br_mem, w_out, ln2_g, ln2_b, w_router, router_bias, w_e_gate, w_e_up, w_e_down,
           w_s_gate, w_s_up, w_s_down):
    assert w_in.shape[0] == DEPTH == 1
    batch, seq, _ = x_prompt.shape
    dec_batch, dec_seq, _ = x_sample.shape
    past = cache_mla_latent.shape[2]
    n_mem = mem_prompt.shape[1]
    l = 0

    (w_main, w_q, w_abs, q_g, kv_g, b_fp, w_gate, w_uvp) = _prep_weights(
        w_in[l], b_f[l], q_norm_g[l], w_q_up[l], kv_norm_g[l], w_kv_up[l])
    merge_w = (w_gate, w_uvp, w_br_mla[l].astype(BF16), w_br_fox[l].astype(BF16),
               w_br_mem[l].astype(BF16), w_out[l].astype(BF16), ln1_g[l].reshape(1, -1),
               ln1_b[l].reshape(1, -1), w_s_gate[l].astype(BF16), w_s_up[l].astype(BF16),
               w_s_down[l].astype(BF16))
    moe_w = (w_router[l].T.astype(BF16), router_bias[l].reshape(-1, 1),
             w_e_gate[l].astype(BF16), w_e_up[l].astype(BF16), w_e_down[l].astype(BF16),
             ln2_g[l].reshape(1, -1), ln2_b[l].reshape(1, -1))

    def tail(x, olat, ofox, omem, tm, tm_moe):
        x1b, base = _merge(x, olat, ofox, omem, *merge_w, tm)
        return _moe_ln2(x1b, base, *moe_w, tm_moe)

    tp = batch * seq
    xp = x_prompt.reshape(tp, D_MODEL)
    tm_p = min(512, seq)
    tq_p = min(256, seq)
    tk_p = min(512, seq)
    cos_p, sin_p = _rope_tables(jnp.arange(seq, dtype=jnp.int32))
    (qcat, kcat, lat, latb, kr, fq, fk, fv, fkb, fvb, mq, logf, logfp) = _proj(
        xp, w_main, w_q, w_abs, q_g, kv_g, b_fp, cos_p, sin_p, tm_p)
    tmem = batch * n_mem
    m_k, m_v, m_kb, m_vb = _memkv(mem_prompt.reshape(tmem, D_MODEL), w_mem_k[l].astype(BF16),
                                  w_mem_v[l].astype(BF16), min(512, tmem))
    c_p = _cumsum(logfp.reshape(batch, seq, LANES))
    c_row = jnp.swapaxes(c_p[:, :, :H_FOX], 1, 2)
    olat = _mla_attention(qcat, kcat.reshape(batch, seq, -1), latb.reshape(batch, seq, -1),
                          batch, seq, tq_p, tk_p, seq, 0)
    ofox = _fox_attention(fq, fkb.reshape(batch, seq, -1), fvb.reshape(batch, seq, -1), c_p, c_row,
                          batch, seq, tq_p, tk_p, seq, 0)
    omem = _mem_attention(mq, m_kb.reshape(batch, n_mem, -1), m_vb.reshape(batch, n_mem, -1),
                          batch, seq, tq_p)
    y_p = tail(xp, olat, ofox, omem, tm_p, min(256, seq)).reshape(batch, seq, D_MODEL)

    ts = dec_batch * dec_seq
    xs = x_sample.reshape(ts, D_MODEL)
    cos_s, sin_s = _rope_tables(jnp.tile(past + jnp.arange(dec_seq, dtype=jnp.int32), dec_batch))
    (qcat_s, kcat_s, lat_s, latb_s, kr_s, fq_s, fk_s, fv_s, fkb_s, fvb_s, mq_s, logf_s,
     logfp_s) = _proj(xs, w_main, w_q, w_abs, q_g, kv_g, b_fp, cos_s, sin_s, ts)
    kv_len = past + dec_seq
    tk_s = 256
    sk = _round_up(kv_len, tk_s)

    def with_cache(cache, new):
        new = new.reshape(dec_batch, dec_seq, -1)
        pad = jnp.zeros((dec_batch, sk - kv_len, new.shape[-1]), new.dtype)
        return jnp.concatenate([cache.astype(new.dtype), new, pad], axis=1)

    rep = LANES // ROPE_DIM
    kcat_all = with_cache(
        jnp.concatenate([cache_mla_latent[l], jnp.tile(cache_mla_krope[l], (1, 1, rep))], axis=-1),
        kcat_s)
    lat_all = with_cache(
        jnp.concatenate([cache_mla_latent[l], jnp.ones((dec_batch, past, LANES), F32)], axis=-1), latb_s)
    fk_all = with_cache(cache_fox_k[l].reshape(dec_batch, past, -1), fkb_s)
    n_pairs = H_FOX * HD_FOX // LANES
    fv_pairs = cache_fox_v[l].reshape(dec_batch, past, n_pairs, LANES)
    fv_all = with_cache(
        jnp.concatenate([fv_pairs, jnp.ones_like(fv_pairs)], axis=-1).reshape(dec_batch, past, -1),
        fvb_s)
    logf_all = with_cache(jnp.pad(cache_fox_logf[l], ((0, 0), (0, 0), (0, LANES - H_FOX))), logfp_s)
    c_s = _cumsum(logf_all)
    c_row_s = jnp.swapaxes(c_s[:, :, :H_FOX], 1, 2)
    c_q_s = c_s[:, past:kv_len]
    olat_s = _mla_attention(qcat_s, kcat_all, lat_all, dec_batch, dec_seq, dec_seq, tk_s, kv_len, past)
    ofox_s = _fox_attention(fq_s, fk_all, fv_all, c_q_s, c_row_s, dec_batch, dec_seq, dec_seq, tk_s,
                            kv_len, past)
    omem_s = _mem_attention(mq_s, cache_mem_k[l].reshape(dec_batch, n_mem, -1).astype(BF16),
                            cache_mem_v[l].reshape(dec_batch, n_mem, -1).astype(BF16),
                            dec_batch, dec_seq, dec_seq)
    y_s = tail(xs, olat_s, ofox_s, omem_s, ts, ts).reshape(dec_batch, dec_seq, D_MODEL)

    def stack(a, b, s, *tail_shape):
        return a.reshape(1, b, s, *tail_shape)

    return (y_p, y_s,
            stack(lat, batch, seq, KV_LORA), stack(kr, batch, seq, ROPE_DIM),
            stack(fk, batch, seq, H_FOX, HD_FOX), stack(fv, batch, seq, H_FOX, HD_FOX),
            stack(logf, batch, seq, H_FOX),
            stack(m_k, batch, n_mem, H_MEM, HD_MEM), stack(m_v, batch, n_mem, H_MEM, HD_MEM),
            stack(lat_s, dec_batch, dec_seq, KV_LORA), stack(kr_s, dec_batch, dec_seq, ROPE_DIM),
            stack(fk_s, dec_batch, dec_seq, H_FOX, HD_FOX), stack(fv_s, dec_batch, dec_seq, H_FOX, HD_FOX),
            stack(logf_s, dec_batch, dec_seq, H_FOX))
```

```python
import functools

import jax
import jax.numpy as jnp
import numpy as np
from jax import lax
from jax.experimental import pallas as pl
from jax.experimental.pallas import tpu as pltpu

F32 = jnp.float32
BF16 = jnp.bfloat16

D_MODEL = 1024
CHUNK = 64
H_MLA = 8
Q_LORA = 256
KV_LORA = 128
NOPE_DIM = 64
ROPE_DIM = 32
V_DIM = 64
ROPE_THETA = 10000.0
MLA_SCALE = (NOPE_DIM + ROPE_DIM) ** -0.5
H_FOX = 8
HD_FOX = 64
FOX_SCALE = HD_FOX ** -0.5
H_MEM = 4
HD_MEM = 128
MEM_SCALE = HD_MEM ** -0.5
N_BRANCH = 3
N_EXPERTS = 64
TOP_K = 8
N_GROUPS = 8
GROUP_SIZE = N_EXPERTS // N_GROUPS
TOPK_GROUPS = 4
D_EXPERT = 256
ROUTED_SCALE = 2.5
DEPTH = 1
ALPHA = (2 * DEPTH) ** 0.25
NORM_EPS = 1e-5
NEG_INF = -1e30

LANES = 128
VMEM_LIMIT = 56 * 1024 * 1024

C_QLAT = 0
C_CKV = C_QLAT + Q_LORA
C_KR = C_CKV + KV_LORA
C_KRR = C_KR + LANES
C_FQ = C_KRR + LANES
C_FK = C_FQ + H_FOX * HD_FOX
C_FV = C_FK + H_FOX * HD_FOX
C_MQ = C_FV + H_FOX * HD_FOX
C_FL = C_MQ + H_MEM * HD_MEM
C_END = C_FL + LANES


def _params(n_grid_axes):
    return pltpu.CompilerParams(
        dimension_semantics=("arbitrary",) * n_grid_axes, vmem_limit_bytes=VMEM_LIMIT)


def _full(shape):
    return pl.BlockSpec(shape, lambda *_: (0,) * len(shape))


def _dot(a, b):
    return jnp.dot(a, b, preferred_element_type=F32)


def _dot_nt(a, b):
    return lax.dot_general(a, b, (((1,), (1,)), ((), ())), preferred_element_type=F32)


def _sigmoid(x):
    return 1.0 / (1.0 + jnp.exp(-x))


def _rms(x, g):
    ms = jnp.mean(x * x, axis=-1, keepdims=True)
    return x * lax.rsqrt(ms + NORM_EPS) * g


def _layer_norm(x, g, b):
    mu = jnp.mean(x, axis=-1, keepdims=True)
    xc = x - mu
    var = jnp.mean(xc * xc, axis=-1, keepdims=True)
    return xc * lax.rsqrt(var + NORM_EPS) * g + b


def _proj_kernel(x_ref, w_ref, wq_ref, wabs_ref, qg_ref, kvg_ref, bf_ref, cos_ref, sin_ref,
                 qcat_ref, kcat_ref, lat_ref, latb_ref, kr_ref, fq_ref, fk_ref, fv_ref,
                 fkb_ref, fvb_ref, mq_ref, logf_ref, logfp_ref):
    xb = x_ref[...].astype(BF16)
    tm = xb.shape[0]

    def proj(c0, width):
        return _dot(xb, w_ref[:, c0:c0 + width])

    cos = cos_ref[...]
    sin = sin_ref[...]
    qn = _rms(proj(C_QLAT, Q_LORA), qg_ref[...]).astype(BF16)
    q2 = _dot(qn, wq_ref[...])
    nq = H_MLA * NOPE_DIM
    nr = H_MLA * ROPE_DIM
    q_rope = (q2[:, nq:nq + nr] * cos + q2[:, nq + nr:nq + 2 * nr] * sin) * MLA_SCALE
    lane = lax.broadcasted_iota(jnp.int32, (tm, LANES), 1)
    heads_per_group = LANES // ROPE_DIM
    for j in range(H_MLA // 2):
        q_abs = _dot(q2[:, j * LANES:(j + 1) * LANES].astype(BF16), wabs_ref[j]) * MLA_SCALE
        for half in range(2):
            h = 2 * j + half
            grp = h // heads_per_group
            qr = q_rope[:, grp * LANES:(grp + 1) * LANES]
            qr = jnp.where(lane // ROPE_DIM == h % heads_per_group, qr, 0.0)
            qcat_ref[h, :, 0:LANES] = q_abs[:, half * LANES:(half + 1) * LANES].astype(BF16)
            qcat_ref[h, :, LANES:2 * LANES] = qr.astype(BF16)
    lat = _rms(proj(C_CKV, KV_LORA), kvg_ref[...])
    lat_ref[...] = lat
    ones = jnp.ones((tm, LANES), BF16)
    latb_ref[:, 0:KV_LORA] = lat.astype(BF16)
    latb_ref[:, KV_LORA:KV_LORA + LANES] = ones
    kr = proj(C_KR, LANES) * cos[:, :LANES] + proj(C_KRR, LANES) * sin[:, :LANES]
    kr_ref[...] = kr[:, :ROPE_DIM]
    kcat_ref[:, 0:KV_LORA] = lat.astype(BF16)
    kcat_ref[:, KV_LORA:KV_LORA + LANES] = kr.astype(BF16)
    fq_ref[...] = (proj(C_FQ, H_FOX * HD_FOX) * FOX_SCALE).astype(BF16)
    fk = proj(C_FK, H_FOX * HD_FOX)
    fk_ref[...] = fk
    fkb_ref[...] = fk.astype(BF16)
    fv = proj(C_FV, H_FOX * HD_FOX)
    fv_ref[...] = fv
    for j in range(H_FOX * HD_FOX // LANES):
        fvb_ref[:, 2 * j * LANES:(2 * j + 1) * LANES] = fv[:, j * LANES:(j + 1) * LANES].astype(BF16)
        fvb_ref[:, (2 * j + 1) * LANES:(2 * j + 2) * LANES] = ones
    mq_ref[...] = proj(C_MQ, H_MEM * HD_MEM).astype(BF16)
    z = proj(C_FL, LANES) + bf_ref[...]
    log_f = jnp.minimum(z, 0.0) - jnp.log1p(jnp.exp(-jnp.abs(z)))
    logfp_ref[...] = log_f
    logf_ref[...] = log_f[:, :H_FOX]


def _proj(x, w_main, w_q, w_abs, q_g, kv_g, b_f, cos_t, sin_t, tm):
    t = x.shape[0]
    n_tab = cos_t.shape[0] // tm
    row = lambda w: pl.BlockSpec((tm, w), lambda i: (i, 0))
    tab = pl.BlockSpec((tm, 2 * LANES), lambda i: (i % n_tab, 0))
    hw = H_FOX * HD_FOX
    out_shape = (
        jax.ShapeDtypeStruct((H_MLA, t, 2 * LANES), BF16),
        jax.ShapeDtypeStruct((t, 2 * LANES), BF16),
        jax.ShapeDtypeStruct((t, KV_LORA), F32),
        jax.ShapeDtypeStruct((t, KV_LORA + LANES), BF16),
        jax.ShapeDtypeStruct((t, ROPE_DIM), F32),
        jax.ShapeDtypeStruct((t, hw), BF16),
        jax.ShapeDtypeStruct((t, hw), F32),
        jax.ShapeDtypeStruct((t, hw), F32),
        jax.ShapeDtypeStruct((t, hw), BF16),
        jax.ShapeDtypeStruct((t, 2 * hw), BF16),
        jax.ShapeDtypeStruct((t, H_MEM * HD_MEM), BF16),
        jax.ShapeDtypeStruct((t, H_FOX), F32),
        jax.ShapeDtypeStruct((t, LANES), F32),
    )
    out_specs = (
        pl.BlockSpec((H_MLA, tm, 2 * LANES), lambda i: (0, i, 0)),
        row(2 * LANES), row(KV_LORA), row(KV_LORA + LANES), row(ROPE_DIM), row(hw), row(hw), row(hw),
        row(hw), row(2 * hw), row(H_MEM * HD_MEM), row(H_FOX), row(LANES),
    )
    return pl.pallas_call(
        _proj_kernel,
        grid=(t // tm,),
        in_specs=[row(D_MODEL), _full(w_main.shape), _full(w_q.shape), _full(w_abs.shape),
                  _full(q_g.shape), _full(kv_g.shape), _full(b_f.shape), tab, tab],
        out_specs=out_specs,
        out_shape=out_shape,
        compiler_params=_params(1),
    )(x, w_main, w_q, w_abs, q_g, kv_g, b_f, cos_t, sin_t)


def _memkv_kernel(x_ref, wk_ref, wv_ref, k_ref, v_ref, kb_ref, vb_ref):
    xb = x_ref[...].astype(BF16)
    k = _dot(xb, wk_ref[...])
    v = _dot(xb, wv_ref[...])
    k_ref[...] = k
    v_ref[...] = v
    kb_ref[...] = k.astype(BF16)
    vb_ref[...] = v.astype(BF16)


def _memkv(x, wk, wv, tm):
    t = x.shape[0]
    w = H_MEM * HD_MEM
    row = lambda n: pl.BlockSpec((tm, n), lambda i: (i, 0))
    return pl.pallas_call(
        _memkv_kernel,
        grid=(t // tm,),
        in_specs=[row(D_MODEL), _full(wk.shape), _full(wv.shape)],
        out_specs=(row(w), row(w), row(w), row(w)),
        out_shape=(jax.ShapeDtypeStruct((t, w), F32), jax.ShapeDtypeStruct((t, w), F32),
                   jax.ShapeDtypeStruct((t, w), BF16), jax.ShapeDtypeStruct((t, w), BF16)),
        compiler_params=_params(1),
    )(x, wk, wv)


CUM_BLOCK = 256


def _cumsum_kernel(x_ref, c_ref):
    s = x_ref.shape[1]
    r = lax.broadcasted_iota(jnp.int32, (CUM_BLOCK, CUM_BLOCK), 0)
    c = lax.broadcasted_iota(jnp.int32, (CUM_BLOCK, CUM_BLOCK), 1)
    tri = jnp.where(r >= c, 1.0, 0.0).astype(BF16)

    def body(i, carry):
        r0 = pl.multiple_of(i * CUM_BLOCK, CUM_BLOCK)
        x = x_ref[0, pl.ds(r0, CUM_BLOCK), :]
        hi = x.astype(BF16)
        r1 = x - hi.astype(F32)
        mid = r1.astype(BF16)
        lo = (r1 - mid.astype(F32)).astype(BF16)
        cs = _dot(tri, hi) + _dot(tri, mid) + _dot(tri, lo) + carry
        c_ref[0, pl.ds(r0, CUM_BLOCK), :] = cs
        return cs[CUM_BLOCK - 1:CUM_BLOCK, :]

    lax.fori_loop(0, s // CUM_BLOCK, body, jnp.zeros((1, LANES), F32))


def _cumsum(x):
    b, s, _ = x.shape
    spec = pl.BlockSpec((1, s, LANES), lambda i: (i, 0, 0))
    return pl.pallas_call(
        _cumsum_kernel, grid=(b,), in_specs=[spec], out_specs=spec,
        out_shape=jax.ShapeDtypeStruct(x.shape, F32), compiler_params=_params(1),
    )(x)


def _online_softmax_step(s, v_ones, m_ref, acc_ref, shift=None):
    def lanes(x, width):
        return jnp.concatenate([x] * (width // LANES), axis=-1)

    m = m_ref[...]
    r = jnp.max(s, axis=-1, keepdims=True)
    m_new = jnp.maximum(m, r if shift is None else r + shift)
    alpha = jnp.exp(m - m_new)
    p = jnp.exp(s - lanes(m_new if shift is None else m_new - shift, s.shape[-1]))
    m_ref[...] = m_new
    acc_ref[...] = lanes(alpha, acc_ref.shape[-1]) * acc_ref[...] + _dot(p.astype(BF16), v_ones)


def _init_softmax_state(m_ref, acc_ref):
    m_ref[...] = jnp.full(m_ref.shape, NEG_INF, F32)
    acc_ref[...] = jnp.zeros(acc_ref.shape, F32)


def _normalised(acc):
    return acc[:, :LANES] / acc[:, LANES:]


def _kv_tile_counts(first_visible_end, last_visible_end, tk, kv_len, sk):
    n_full = jnp.minimum(first_visible_end // tk, kv_len // tk)
    n_any = jnp.minimum((jnp.minimum(last_visible_end, kv_len) + tk - 1) // tk, sk // tk)
    return n_full, n_any


def _kv_loop(n_full, n_any, consume):
    def body(kt, carry, masked):
        consume(kt, masked)
        return carry

    lax.fori_loop(0, n_full, functools.partial(body, masked=False), 0)
    lax.fori_loop(n_full, n_any, functools.partial(body, masked=True), 0)


def _mla_kernel(q_ref, k_ref, v_ref, o_ref, m_ref, acc_ref, *, tq, tk, kv_len, q_pos0):
    sk = k_ref.shape[1]
    rows = H_MLA * tq
    q_start = q_pos0 + pl.program_id(1) * tq
    n_full, n_any = _kv_tile_counts((q_start // CHUNK + 1) * CHUNK,
                                    ((q_start + tq - 1) // CHUNK + 1) * CHUNK, tk, kv_len, sk)
    q = q_ref[...].reshape(rows, q_ref.shape[-1])
    q_chunk = (q_start + lax.broadcasted_iota(jnp.int32, (rows, 1), 0) % tq) // CHUNK

    def consume(kt, masked):
        k0 = pl.multiple_of(kt * tk, tk)
        s = _dot_nt(q, k_ref[0, pl.ds(k0, tk), :])
        if masked:
            k_pos = k0 + lax.broadcasted_iota(jnp.int32, (1, tk), 1)
            mask = k_pos // CHUNK <= q_chunk
            if kv_len < sk:
                mask = jnp.logical_and(mask, k_pos < kv_len)
            s = jnp.where(mask, s, NEG_INF)
        _online_softmax_step(s, v_ref[0, pl.ds(k0, tk), :], m_ref, acc_ref)

    _init_softmax_state(m_ref, acc_ref)
    _kv_loop(n_full, n_any, consume)
    o = _normalised(acc_ref[...]).astype(BF16)
    for h in range(H_MLA):
        o_ref[:, h * KV_LORA:(h + 1) * KV_LORA] = o[h * tq:(h + 1) * tq, :]


def _mla_attention(qcat, kcat, latb, batch, sq, tq, tk, kv_len, q_pos0):
    sk = kcat.shape[1]
    nq = sq // tq
    kern = functools.partial(_mla_kernel, tq=tq, tk=tk, kv_len=kv_len, q_pos0=q_pos0)
    return pl.pallas_call(
        kern,
        grid=(batch, nq),
        in_specs=[pl.BlockSpec((H_MLA, tq, 2 * LANES), lambda b, i: (0, b * nq + i, 0)),
                  pl.BlockSpec((1, sk, 2 * LANES), lambda b, i: (b, 0, 0)),
                  pl.BlockSpec((1, sk, 2 * LANES), lambda b, i: (b, 0, 0))],
        out_specs=pl.BlockSpec((tq, H_MLA * KV_LORA), lambda b, i: (b * nq + i, 0)),
        out_shape=jax.ShapeDtypeStruct((batch * sq, H_MLA * KV_LORA), BF16),
        scratch_shapes=[pltpu.VMEM((H_MLA * tq, LANES), F32),
                        pltpu.VMEM((H_MLA * tq, 2 * LANES), F32)],
        compiler_params=_params(2),
    )(qcat, kcat, latb)


def _fox_kernel(q_ref, k_ref, v_ref, cq_ref, ck_ref, o_ref, m_ref, acc_ref, cqb_ref, *, tq, tk,
                kv_len, q_pos0):
    sk = k_ref.shape[1]
    q_start = q_pos0 + pl.program_id(1) * tq
    n_full, n_any = _kv_tile_counts(q_start + 1, q_start + tq, tk, kv_len, sk)
    lane = lax.broadcasted_iota(jnp.int32, (tq, LANES), 1)
    n_pairs = H_FOX * HD_FOX // LANES
    q_pos = q_start + lax.broadcasted_iota(jnp.int32, (2 * tq, 1), 0) % tq
    q2 = []
    for j in range(n_pairs):
        q_pair = q_ref[:, j * LANES:(j + 1) * LANES]
        zero = jnp.zeros_like(q_pair)
        q2.append(jnp.concatenate([jnp.where(lane < HD_FOX, q_pair, zero),
                                   jnp.where(lane < HD_FOX, zero, q_pair)], axis=0))
        for half in range(2):
            col = jnp.sum(jnp.where(lane == 2 * j + half, cq_ref[0], 0.0), axis=-1, keepdims=True)
            cqb_ref[j, half * tq:(half + 1) * tq, :] = jnp.broadcast_to(col, (tq, LANES))

    def consume(kt, masked):
        k0 = pl.multiple_of(kt * tk, tk)
        if masked:
            k_pos = k0 + lax.broadcasted_iota(jnp.int32, (1, tk), 1)
            mask = k_pos <= q_pos
            if kv_len < sk:
                mask = jnp.logical_and(mask, k_pos < kv_len)
        for j in range(n_pairs):
            s = _dot_nt(q2[j], k_ref[0, pl.ds(k0, tk), j * LANES:(j + 1) * LANES])
            c_k = jnp.concatenate(
                [jnp.broadcast_to(ck_ref[0, 2 * j:2 * j + 1, pl.ds(k0, tk)], (tq, tk)),
                 jnp.broadcast_to(ck_ref[0, 2 * j + 1:2 * j + 2, pl.ds(k0, tk)], (tq, tk))], axis=0)
            s = s - c_k
            if masked:
                s = jnp.where(mask, s, NEG_INF)
            v_ones = v_ref[0, pl.ds(k0, tk), 2 * j * LANES:2 * (j + 1) * LANES]
            _online_softmax_step(s, v_ones, m_ref.at[j], acc_ref.at[j], shift=cqb_ref[j])

    _init_softmax_state(m_ref, acc_ref)
    _kv_loop(n_full, n_any, consume)
    for j in range(n_pairs):
        o = _normalised(acc_ref[j])
        o_ref[:, j * LANES:(j + 1) * LANES] = jnp.where(lane < HD_FOX, o[:tq], o[tq:]).astype(BF16)


def _fox_attention(fq, fkb, fvb, c_q, c_k, batch, sq, tq, tk, kv_len, q_pos0):
    sk = fkb.shape[1]
    nq = sq // tq
    w = H_FOX * HD_FOX
    kern = functools.partial(_fox_kernel, tq=tq, tk=tk, kv_len=kv_len, q_pos0=q_pos0)
    return pl.pallas_call(
        kern,
        grid=(batch, nq),
        in_specs=[pl.BlockSpec((tq, w), lambda b, i: (b * nq + i, 0)),
                  pl.BlockSpec((1, sk, w), lambda b, i: (b, 0, 0)),
                  pl.BlockSpec((1, sk, 2 * w), lambda b, i: (b, 0, 0)),
                  pl.BlockSpec((1, tq, LANES), lambda b, i: (b, i, 0)),
                  pl.BlockSpec((1, H_FOX, sk), lambda b, i: (b, 0, 0))],
        out_specs=pl.BlockSpec((tq, w), lambda b, i: (b * nq + i, 0)),
        out_shape=jax.ShapeDtypeStruct((batch * sq, w), BF16),
        scratch_shapes=[pltpu.VMEM((w // LANES, 2 * tq, LANES), F32),
                        pltpu.VMEM((w // LANES, 2 * tq, 2 * LANES), F32),
                        pltpu.VMEM((w // LANES, 2 * tq, LANES), F32)],
        compiler_params=_params(2),
    )(fq, fkb, fvb, c_q, c_k)


def _mem_kernel(q_ref, k_ref, v_ref, o_ref):
    for h in range(H_MEM):
        cols = slice(h * HD_MEM, (h + 1) * HD_MEM)
        s = _dot_nt(q_ref[:, cols], k_ref[0, :, cols]) * MEM_SCALE
        m = jnp.max(s, axis=-1, keepdims=True)
        p = jnp.exp(s - m)
        l = jnp.sum(p, axis=-1, keepdims=True)
        o = _dot(p.astype(BF16), v_ref[0, :, cols]) / l
        o_ref[:, cols] = o.astype(BF16)


def _mem_attention(mq, mkb, mvb, batch, sq, tq):
    nq = sq // tq
    n_mem = mkb.shape[1]
    w = H_MEM * HD_MEM
    return pl.pallas_call(
        _mem_kernel,
        grid=(batch, nq),
        in_specs=[pl.BlockSpec((tq, w), lambda b, i: (b * nq + i, 0)),
                  pl.BlockSpec((1, n_mem, w), lambda b, i: (b, 0, 0)),
                  pl.BlockSpec((1, n_mem, w), lambda b, i: (b, 0, 0))],
        out_specs=pl.BlockSpec((tq, w), lambda b, i: (b * nq + i, 0)),
        out_shape=jax.ShapeDtypeStruct((batch * sq, w), BF16),
        compiler_params=_params(2),
    )(mq, mkb, mvb)


def _route(x1b, wr_t, bias):
    tm = x1b.shape[0]
    scores = _sigmoid(_dot_nt(wr_t, x1b))
    sel = scores + bias
    sub = lax.broadcasted_iota(jnp.int32, (GROUP_SIZE, tm), 0)
    gs = []
    for g in range(N_GROUPS):
        blk = sel[g * GROUP_SIZE:(g + 1) * GROUP_SIZE, :]
        m1 = jnp.max(blk, axis=0, keepdims=True)
        first = jnp.min(jnp.where(blk == m1, sub, GROUP_SIZE), axis=0, keepdims=True)
        m2 = jnp.max(jnp.where(sub == first, -jnp.inf, blk), axis=0, keepdims=True)
        gs.append(m1 + m2)
    e_idx = lax.broadcasted_iota(jnp.int32, (N_EXPERTS, tm), 0)
    allowed = jnp.zeros((N_EXPERTS, tm), jnp.bool_)
    for g in range(N_GROUPS):
        rank = jnp.zeros((1, tm), jnp.int32)
        for o in range(N_GROUPS):
            if o == g:
                continue
            beats = (gs[o] >= gs[g]) if o < g else (gs[o] > gs[g])
            rank = rank + beats.astype(jnp.int32)
        keep = rank < TOPK_GROUPS
        allowed = jnp.logical_or(allowed, jnp.logical_and(e_idx // GROUP_SIZE == g, keep))
    cand = jnp.where(allowed, sel, -jnp.inf)
    chosen = jnp.zeros((N_EXPERTS, tm), jnp.bool_)
    for _ in range(TOP_K):
        m = jnp.max(cand, axis=0, keepdims=True)
        first = jnp.min(jnp.where(cand == m, e_idx, N_EXPERTS), axis=0, keepdims=True)
        pick = e_idx == first
        chosen = jnp.logical_or(chosen, pick)
        cand = jnp.where(pick, -jnp.inf, cand)
    w = jnp.where(chosen, scores, 0.0)
    return w / jnp.sum(w, axis=0, keepdims=True) * ROUTED_SCALE, chosen


def _swiglu(xb, wg, wu):
    a = _dot(xb, wg)
    return (a * _sigmoid(a)) * _dot(xb, wu)


def _merge_kernel(x_ref, olat_ref, ofox_ref, omem_ref, wg_ref, wuv_ref, wbm_ref, wbf_ref, wbc_ref,
                  wo_ref, g1_ref, b1_ref, wsg_ref, wsu_ref, wsd_ref, x1b_ref, base_ref):
    x = x_ref[...]
    xb = x.astype(BF16)
    o_mla = jnp.concatenate(
        [_dot(olat_ref[:, j * 2 * KV_LORA:(j + 1) * 2 * KV_LORA], wuv_ref[j]).astype(BF16)
         for j in range(H_MLA // 2)], axis=-1)

    def gate(n):
        return _sigmoid(_dot(xb, wg_ref[:, n * D_MODEL:(n + 1) * D_MODEL]))

    y = gate(0) * _dot(o_mla, wbm_ref[...])
    y = y + gate(1) * _dot(ofox_ref[...], wbf_ref[...])
    y = y + gate(2) * _dot(omem_ref[...], wbc_ref[...])
    mix = _dot(y.astype(BF16), wo_ref[...])
    x1 = _layer_norm(ALPHA * x + mix, g1_ref[...], b1_ref[...])
    x1b = x1.astype(BF16)
    x1b_ref[...] = x1b
    shared = _dot(_swiglu(x1b, wsg_ref[...], wsu_ref[...]).astype(BF16), wsd_ref[...])
    base_ref[...] = ALPHA * x1 + shared


def _merge(x, olat, ofox, omem, wg, wuv, wbm, wbf, wbc, wo, g1, b1, wsg, wsu, wsd, tm):
    t = x.shape[0]
    row = lambda n: pl.BlockSpec((tm, n), lambda i: (i, 0))
    ws = (wg, wuv, wbm, wbf, wbc, wo, g1, b1, wsg, wsu, wsd)
    return pl.pallas_call(
        _merge_kernel,
        grid=(t // tm,),
        in_specs=[row(D_MODEL), row(H_MLA * KV_LORA), row(H_FOX * HD_FOX), row(H_MEM * HD_MEM)]
        + [_full(w.shape) for w in ws],
        out_specs=(row(D_MODEL), row(D_MODEL)),
        out_shape=(jax.ShapeDtypeStruct((t, D_MODEL), BF16), jax.ShapeDtypeStruct((t, D_MODEL), F32)),
        compiler_params=_params(1),
    )(x, olat, ofox, omem, *ws)


BF16_ROWS = 16
PLACE_ROWS = 512
EXPERT_ROWS = 2048
EXPERT_CHUNK = 512
EXPERTS_PER_STEP = 4


def _capacity(tm):
    return min(tm, _round_up(2 * tm * TOP_K // N_EXPERTS, BF16_ROWS))


def _dispatch_kernel(x1b_ref, wr_ref, rb_ref, xs_ref, gate_ref, slot_ref, govf_ref, flag_ref,
                     pm_ref, *, cap):
    xb = x1b_ref[...]
    tm = xb.shape[0]
    gate, chosen = _route(xb, wr_ref[...], rb_ref[...])
    r = lax.broadcasted_iota(jnp.int32, (tm, tm), 0)
    c = lax.broadcasted_iota(jnp.int32, (tm, tm), 1)
    earlier = jnp.where(r < c, 1.0, 0.0).astype(BF16)
    rank = _dot(jnp.where(chosen, 1.0, 0.0).astype(BF16), earlier)
    fits = jnp.logical_and(chosen, rank < cap)
    slot = jnp.where(fits, rank, -1.0)
    gate_ref[...] = jnp.where(fits, gate, 0.0)
    slot_ref[...] = slot
    g_ovf = jnp.where(fits, 0.0, gate)
    govf_ref[...] = jnp.concatenate([g_ovf, jnp.zeros((LANES - N_EXPERTS, tm), F32)], axis=0).T
    any_ovf = jnp.max(jnp.max(g_ovf, axis=0, keepdims=True), axis=1, keepdims=True)
    flag_ref[0] = jnp.broadcast_to(jnp.where(any_ovf > 0.0, 1, 0), flag_ref.shape[1:]).astype(jnp.int32)
    s_iota = lax.broadcasted_iota(jnp.int32, (cap, tm), 0).astype(F32)
    for e in range(N_EXPERTS):
        pm_ref[e * cap:(e + 1) * cap, :] = jnp.where(s_iota == slot[e:e + 1, :], 1.0, 0.0).astype(BF16)
    per = PLACE_ROWS // cap
    for g in range(N_EXPERTS // per):
        xs = _dot(pm_ref[g * PLACE_ROWS:(g + 1) * PLACE_ROWS, :], xb).astype(BF16)
        xs_ref[g * per:(g + 1) * per] = xs.reshape(per, cap, D_MODEL)


def _dispatch(x1b, wr_t, rb, tm, cap):
    t = x1b.shape[0]
    nt = t // tm
    col = pl.BlockSpec((N_EXPERTS, tm), lambda i: (0, i))
    return pl.pallas_call(
        functools.partial(_dispatch_kernel, cap=cap),
        grid=(nt,),
        in_specs=[pl.BlockSpec((tm, D_MODEL), lambda i: (i, 0)), _full(wr_t.shape), _full(rb.shape)],
        out_specs=(pl.BlockSpec((N_EXPERTS, cap, D_MODEL), lambda i: (0, i, 0)), col, col,
                   pl.BlockSpec((tm, LANES), lambda i: (i, 0)),
                   pl.BlockSpec((1, 8, LANES), lambda i: (i, 0, 0))),
        out_shape=(jax.ShapeDtypeStruct((N_EXPERTS, nt * cap, D_MODEL), BF16),
                   jax.ShapeDtypeStruct((N_EXPERTS, t), F32),
                   jax.ShapeDtypeStruct((N_EXPERTS, t), F32),
                   jax.ShapeDtypeStruct((t, LANES), F32),
                   jax.ShapeDtypeStruct((nt, 8, LANES), jnp.int32)),
        scratch_shapes=[pltpu.VMEM((N_EXPERTS * cap, tm), BF16)],
        compiler_params=_params(1),
    )(x1b, wr_t, rb)


def _expert_kernel(xs_ref, wg_ref, wu_ref, wd_ref, ys_ref, wgb_ref, wub_ref, wdb_ref, *, chunk):
    @pl.when(pl.program_id(1) == 0)
    def _():
        wgb_ref[...] = wg_ref[0].astype(BF16)
        wub_ref[...] = wu_ref[0].astype(BF16)
        wdb_ref[...] = wd_ref[0].astype(BF16)

    def body(c, carry):
        r0 = pl.multiple_of(c * chunk, chunk)
        hid = _swiglu(xs_ref[0, pl.ds(r0, chunk), :], wgb_ref[...], wub_ref[...])
        ys_ref[0, pl.ds(r0, chunk), :] = _dot(hid.astype(BF16), wdb_ref[...]).astype(BF16)
        return carry

    lax.fori_loop(0, xs_ref.shape[1] // chunk, body, 0)


def _experts(xs, weg, weu, wed):
    n_e, rows, _ = xs.shape
    rb = min(EXPERT_ROWS, rows)
    blk = pl.BlockSpec((1, rb, D_MODEL), lambda e, i: (e, i, 0))
    return pl.pallas_call(
        functools.partial(_expert_kernel, chunk=min(EXPERT_CHUNK, rb)),
        grid=(n_e, rows // rb),
        in_specs=[blk,
                  pl.BlockSpec((1, D_MODEL, D_EXPERT), lambda e, i: (e, 0, 0)),
                  pl.BlockSpec((1, D_MODEL, D_EXPERT), lambda e, i: (e, 0, 0)),
                  pl.BlockSpec((1, D_EXPERT, D_MODEL), lambda e, i: (e, 0, 0))],
        out_specs=blk,
        out_shape=jax.ShapeDtypeStruct(xs.shape, BF16),
        scratch_shapes=[pltpu.VMEM((D_MODEL, D_EXPERT), BF16), pltpu.VMEM((D_MODEL, D_EXPERT), BF16),
                        pltpu.VMEM((D_EXPERT, D_MODEL), BF16)],
        compiler_params=_params(2),
    )(xs, weg, weu, wed)


def _dense_routed_kernel(flag_ref, x1b_ref, gate_ref, weg_ref, weu_ref, wed_ref, out_ref):
    step = pl.program_id(1)

    @pl.when(step == 0)
    def _():
        out_ref[...] = jnp.zeros_like(out_ref)

    @pl.when(flag_ref[pl.program_id(0)] > 0)
    def _():
        xb = x1b_ref[...]
        gate = gate_ref[...]
        lane = lax.broadcasted_iota(jnp.int32, gate.shape, 1)
        for j in range(EXPERTS_PER_STEP):
            e = step * EXPERTS_PER_STEP + j
            g_col = jnp.sum(jnp.where(lane == e, gate, 0.0), axis=-1, keepdims=True)
            hid = _swiglu(xb, weg_ref[j], weu_ref[j])
            out_ref[...] += g_col * _dot(hid.astype(BF16), wed_ref[j])


def _dense_routed(flags, x1b, gate, weg, weu, wed, tm):
    t = x1b.shape[0]
    eb = EXPERTS_PER_STEP
    row = lambda n: pl.BlockSpec((tm, n), lambda i, s, f: (i, 0))
    wspec = lambda a, b: pl.BlockSpec((eb, a, b), lambda i, s, f: (jnp.where(f[i] > 0, s, 0), 0, 0))
    return pl.pallas_call(
        _dense_routed_kernel,
        grid_spec=pltpu.PrefetchScalarGridSpec(
            num_scalar_prefetch=1,
            grid=(t // tm, N_EXPERTS // eb),
            in_specs=[row(D_MODEL), row(LANES), wspec(D_MODEL, D_EXPERT), wspec(D_MODEL, D_EXPERT),
                      wspec(D_EXPERT, D_MODEL)],
            out_specs=row(D_MODEL)),
        out_shape=jax.ShapeDtypeStruct((t, D_MODEL), F32),
        compiler_params=_params(2),
    )(flags, x1b, gate, weg, weu, wed)


def _combine_kernel(gate_ref, slot_ref, ys_ref, base_ref, *rest, cap, has_overflow):
    if has_overflow:
        rovf_ref, g2_ref, b2_ref, y_ref, pw_ref = rest
    else:
        g2_ref, b2_ref, y_ref, pw_ref = rest
    tm = base_ref.shape[0]
    gate = gate_ref[...]
    slot = slot_ref[...]
    s_iota = lax.broadcasted_iota(jnp.int32, (cap, tm), 0).astype(F32)
    for e in range(N_EXPERTS):
        pw_ref[e * cap:(e + 1) * cap, :] = jnp.where(
            s_iota == slot[e:e + 1, :], gate[e:e + 1, :], 0.0).astype(BF16)
    routed = rovf_ref[...] if has_overflow else jnp.zeros((tm, D_MODEL), F32)
    per = PLACE_ROWS // cap
    for g in range(N_EXPERTS // per):
        ys = ys_ref[g * per:(g + 1) * per].reshape(PLACE_ROWS, D_MODEL)
        routed = routed + lax.dot_general(pw_ref[g * PLACE_ROWS:(g + 1) * PLACE_ROWS, :], ys,
                                          (((0,), (0,)), ((), ())), preferred_element_type=F32)
    y_ref[...] = _layer_norm(base_ref[...] + routed, g2_ref[...], b2_ref[...])


def _combine(gate_t, slot_t, ys, base, rovf, g2, b2, tm, cap):
    t = base.shape[0]
    col = pl.BlockSpec((N_EXPERTS, tm), lambda i: (0, i))
    row = pl.BlockSpec((tm, D_MODEL), lambda i: (i, 0))
    has_overflow = rovf is not None
    operands = (gate_t, slot_t, ys, base) + ((rovf,) if has_overflow else ()) + (g2, b2)
    return pl.pallas_call(
        functools.partial(_combine_kernel, cap=cap, has_overflow=has_overflow),
        grid=(t // tm,),
        in_specs=[col, col, pl.BlockSpec((N_EXPERTS, cap, D_MODEL), lambda i: (0, i, 0)), row]
        + ([row] if has_overflow else []) + [_full(g2.shape), _full(b2.shape)],
        out_specs=row,
        out_shape=jax.ShapeDtypeStruct((t, D_MODEL), F32),
        scratch_shapes=[pltpu.VMEM((N_EXPERTS * cap, tm), BF16)],
        compiler_params=_params(1),
    )(*operands)


def _moe_ln2(x1b, base, wr_t, rb, weg, weu, wed, g2, b2, tm):
    t = x1b.shape[0]
    cap = _capacity(tm)
    assert PLACE_ROWS % cap == 0 and (N_EXPERTS * cap) % PLACE_ROWS == 0
    xs, gate_t, slot_t, gate_ovf, flags = _dispatch(x1b, wr_t, rb, tm, cap)
    ys = _experts(xs, weg, weu, wed)
    flags = flags[:, 0, 0]

    def with_overflow():
        rovf = _dense_routed(flags, x1b, gate_ovf, weg.astype(BF16), weu.astype(BF16),
                             wed.astype(BF16), tm)
        return _combine(gate_t, slot_t, ys, base, rovf, g2, b2, tm, cap)

    return lax.cond(jnp.any(flags > 0), with_overflow,
                    lambda: _combine(gate_t, slot_t, ys, base, None, g2, b2, tm, cap))


def _rot_half_cols(w):
    half = w.shape[-1] // 2
    return jnp.concatenate([-w[..., half:], w[..., :half]], axis=-1)


def _block_diag2(a, b):
    za = jnp.zeros((a.shape[0], b.shape[1]), a.dtype)
    zb = jnp.zeros((b.shape[0], a.shape[1]), a.dtype)
    return jnp.concatenate([jnp.concatenate([a, za], 1), jnp.concatenate([zb, b], 1)], 0)


def _prep_weights(w_in, b_f, q_norm_g, w_q_up, kv_norm_g, w_kv_up):
    widths = (Q_LORA, KV_LORA, ROPE_DIM, H_FOX * HD_FOX, H_FOX * HD_FOX, H_FOX * HD_FOX, H_FOX,
              H_MEM * HD_MEM, N_BRANCH * D_MODEL)
    offs = np.cumsum((0,) + widths)
    w_ql, w_ckv, w_kr, w_fq, w_fk, w_fv, w_fl, w_mq, w_g = (
        w_in[:, offs[i]:offs[i + 1]] for i in range(len(widths)))
    rep = LANES // ROPE_DIM
    w_main = jnp.concatenate(
        [w_ql, w_ckv, jnp.tile(w_kr, (1, rep)), jnp.tile(_rot_half_cols(w_kr), (1, rep)),
         w_fq, w_fk, w_fv, w_mq, jnp.pad(w_fl, ((0, 0), (0, LANES - H_FOX)))], axis=1).astype(BF16)
    assert w_main.shape[1] == C_END
    wq3 = w_q_up.reshape(Q_LORA, H_MLA, NOPE_DIM + ROPE_DIM)
    wq_nope = wq3[:, :, :NOPE_DIM].reshape(Q_LORA, H_MLA * NOPE_DIM)
    wq_rope = wq3[:, :, NOPE_DIM:]
    w_q = jnp.concatenate(
        [wq_nope, wq_rope.reshape(Q_LORA, -1), _rot_half_cols(wq_rope).reshape(Q_LORA, -1)],
        axis=1).astype(BF16)
    wkv3 = w_kv_up.reshape(KV_LORA, H_MLA, NOPE_DIM + V_DIM)
    w_uk = wkv3[:, :, :NOPE_DIM]
    w_uv = wkv3[:, :, NOPE_DIM:]
    w_abs = jnp.stack([_block_diag2(w_uk[:, 2 * j].T, w_uk[:, 2 * j + 1].T)
                       for j in range(H_MLA // 2)]).astype(BF16)
    w_uvp = jnp.stack([_block_diag2(w_uv[:, 2 * j], w_uv[:, 2 * j + 1])
                       for j in range(H_MLA // 2)]).astype(BF16)
    b_fp = jnp.pad(b_f, (0, LANES - H_FOX)).reshape(1, LANES)
    return (w_main, w_q, w_abs, q_norm_g.reshape(1, -1), kv_norm_g.reshape(1, -1), b_fp,
            w_g.astype(BF16), w_uvp)


def _rope_tables(pos):
    half = ROPE_DIM // 2
    inv_freq = ROPE_THETA ** (-jnp.arange(half, dtype=F32) / half)
    ang = pos.astype(F32)[:, None] * inv_freq[None, :]
    rep = 2 * LANES // ROPE_DIM
    cos = jnp.tile(jnp.concatenate([jnp.cos(ang)] * 2, axis=-1), (1, rep))
    sin = jnp.tile(jnp.concatenate([jnp.sin(ang)] * 2, axis=-1), (1, rep))
    return cos, sin


def _round_up(n, m):
    return (n + m - 1) // m * m


def kernel(x_prompt, x_sample, mem_prompt, cache_mla_latent, cache_mla_krope, cache_fox_k,
           cache_fox_v, cache_fox_logf, cache_mem_k, cache_mem_v, ln1_g, ln1_b, w_in, b_f,
           q_norm_g, w_q_up, kv_norm_g, w_kv_up, w_mem_k, w_mem_v, w_br_mla, w_br_fox,
           w_br_mem, w_out, ln2_g, ln2_b, w_router, router_bias, w_e_gate, w_e_up, w_e_down,
           w_s_gate, w_s_up, w_s_down):
    assert w_in.shape[0] == DEPTH == 1
    batch, seq, _ = x_prompt.shape
    dec_batch, dec_seq, _ = x_sample.shape
    past = cache_mla_latent.shape[2]
    n_mem = mem_prompt.shape[1]
    l = 0

    (w_main, w_q, w_abs, q_g, kv_g, b_fp, w_gate, w_uvp) = _prep_weights(
        w_in[l], b_f[l], q_norm_g[l], w_q_up[l], kv_norm_g[l], w_kv_up[l])
    merge_w = (w_gate, w_uvp, w_br_mla[l].astype(BF16), w_br_fox[l].astype(BF16),
               w_br_mem[l].astype(BF16), w_out[l].astype(BF16), ln1_g[l].reshape(1, -1),
               ln1_b[l].reshape(1, -1), w_s_gate[l].astype(BF16), w_s_up[l].astype(BF16),
               w_s_down[l].astype(BF16))
    moe_w = (w_router[l].T.astype(BF16), router_bias[l].reshape(-1, 1),
             w_e_gate[l], w_e_up[l], w_e_down[l],
             ln2_g[l].reshape(1, -1), ln2_b[l].reshape(1, -1))

    def tail(x, olat, ofox, omem, tm, tm_moe):
        x1b, base = _merge(x, olat, ofox, omem, *merge_w, tm)
        return _moe_ln2(x1b, base, *moe_w, tm_moe)

    tp = batch * seq
    xp = x_prompt.reshape(tp, D_MODEL)
    tm_p = min(512, seq)
    tq_p = min(256, seq)
    tk_p = min(512, seq)
    cos_p, sin_p = _rope_tables(jnp.arange(seq, dtype=jnp.int32))
    (qcat, kcat, lat, latb, kr, fq, fk, fv, fkb, fvb, mq, logf, logfp) = _proj(
        xp, w_main, w_q, w_abs, q_g, kv_g, b_fp, cos_p, sin_p, tm_p)
    tmem = batch * n_mem
    m_k, m_v, m_kb, m_vb = _memkv(mem_prompt.reshape(tmem, D_MODEL), w_mem_k[l].astype(BF16),
                                  w_mem_v[l].astype(BF16), min(512, tmem))
    c_p = _cumsum(logfp.reshape(batch, seq, LANES))
    c_row = jnp.swapaxes(c_p[:, :, :H_FOX], 1, 2)
    olat = _mla_attention(qcat, kcat.reshape(batch, seq, -1), latb.reshape(batch, seq, -1),
                          batch, seq, tq_p, tk_p, seq, 0)
    ofox = _fox_attention(fq, fkb.reshape(batch, seq, -1), fvb.reshape(batch, seq, -1), c_p, c_row,
                          batch, seq, tq_p, tk_p, seq, 0)
    omem = _mem_attention(mq, m_kb.reshape(batch, n_mem, -1), m_vb.reshape(batch, n_mem, -1),
                          batch, seq, tq_p)
    y_p = tail(xp, olat, ofox, omem, tm_p, min(256, seq)).reshape(batch, seq, D_MODEL)

    ts = dec_batch * dec_seq
    xs = x_sample.reshape(ts, D_MODEL)
    cos_s, sin_s = _rope_tables(jnp.tile(past + jnp.arange(dec_seq, dtype=jnp.int32), dec_batch))
    (qcat_s, kcat_s, lat_s, latb_s, kr_s, fq_s, fk_s, fv_s, fkb_s, fvb_s, mq_s, logf_s,
     logfp_s) = _proj(xs, w_main, w_q, w_abs, q_g, kv_g, b_fp, cos_s, sin_s, ts)
    kv_len = past + dec_seq
    tk_s = 256
    sk = _round_up(kv_len, tk_s)

    def with_cache(cache, new):
        new = new.reshape(dec_batch, dec_seq, -1)
        pad = jnp.zeros((dec_batch, sk - kv_len, new.shape[-1]), new.dtype)
        return jnp.concatenate([cache.astype(new.dtype), new, pad], axis=1)

    rep = LANES // ROPE_DIM
    kcat_all = with_cache(
        jnp.concatenate([cache_mla_latent[l], jnp.tile(cache_mla_krope[l], (1, 1, rep))], axis=-1),
        kcat_s)
    lat_all = with_cache(
        jnp.concatenate([cache_mla_latent[l], jnp.ones((dec_batch, past, LANES), F32)], axis=-1), latb_s)
    fk_all = with_cache(cache_fox_k[l].reshape(dec_batch, past, -1), fkb_s)
    n_pairs = H_FOX * HD_FOX // LANES
    fv_pairs = cache_fox_v[l].reshape(dec_batch, past, n_pairs, LANES)
    fv_all = with_cache(
        jnp.concatenate([fv_pairs, jnp.ones_like(fv_pairs)], axis=-1).reshape(dec_batch, past, -1),
        fvb_s)
    logf_all = with_cache(jnp.pad(cache_fox_logf[l], ((0, 0), (0, 0), (0, LANES - H_FOX))), logfp_s)
    c_s = _cumsum(logf_all)
    c_row_s = jnp.swapaxes(c_s[:, :, :H_FOX], 1, 2)
    c_q_s = c_s[:, past:kv_len]
    olat_s = _mla_attention(qcat_s, kcat_all, lat_all, dec_batch, dec_seq, dec_seq, tk_s, kv_len, past)
    ofox_s = _fox_attention(fq_s, fk_all, fv_all, c_q_s, c_row_s, dec_batch, dec_seq, dec_seq, tk_s,
                            kv_len, past)
    omem_s = _mem_attention(mq_s, cache_mem_k[l].reshape(dec_batch, n_mem, -1).astype(BF16),
                            cache_mem_v[l].reshape(dec_batch, n_mem, -1).astype(BF16),
                            dec_batch, dec_seq, dec_seq)
    y_s = tail(xs, olat_s, ofox_s, omem_s, ts, ts).reshape(dec_batch, dec_seq, D_MODEL)

    def stack(a, b, s, *tail_shape):
        return a.reshape(1, b, s, *tail_shape)

    return (y_p, y_s,
            stack(lat, batch, seq, KV_LORA), stack(kr, batch, seq, ROPE_DIM),
            stack(fk, batch, seq, H_FOX, HD_FOX), stack(fv, batch, seq, H_FOX, HD_FOX),
            stack(logf, batch, seq, H_FOX),
            stack(m_k, batch, n_mem, H_MEM, HD_MEM), stack(m_v, batch, n_mem, H_MEM, HD_MEM),
            stack(lat_s, dec_batch, dec_seq, KV_LORA), stack(kr_s, dec_batch, dec_seq, ROPE_DIM),
            stack(fk_s, dec_batch, dec_seq, H_FOX, HD_FOX), stack(fv_s, dec_batch, dec_seq, H_FOX, HD_FOX),
            stack(logf_s, dec_batch, dec_seq, H_FOX))
```

```python
import functools

import jax
import jax.numpy as jnp
import numpy as np
from jax import lax
from jax.experimental import pallas as pl
from jax.experimental.pallas import tpu as pltpu

F32 = jnp.float32
BF16 = jnp.bfloat16

D_MODEL = 1024
CHUNK = 64
H_MLA = 8
Q_LORA = 256
KV_LORA = 128
NOPE_DIM = 64
ROPE_DIM = 32
V_DIM = 64
ROPE_THETA = 10000.0
MLA_SCALE = (NOPE_DIM + ROPE_DIM) ** -0.5
H_FOX = 8
HD_FOX = 64
FOX_SCALE = HD_FOX ** -0.5
H_MEM = 4
HD_MEM = 128
MEM_SCALE = HD_MEM ** -0.5
N_BRANCH = 3
N_EXPERTS = 64
TOP_K = 8
N_GROUPS = 8
GROUP_SIZE = N_EXPERTS // N_GROUPS
TOPK_GROUPS = 4
D_EXPERT = 256
ROUTED_SCALE = 2.5
DEPTH = 1
ALPHA = (2 * DEPTH) ** 0.25
NORM_EPS = 1e-5
NEG_INF = -1e30

LANES = 128
VMEM_LIMIT = 56 * 1024 * 1024

C_QLAT = 0
C_CKV = C_QLAT + Q_LORA
C_KR = C_CKV + KV_LORA
C_KRR = C_KR + LANES
C_FQ = C_KRR + LANES
C_FK = C_FQ + H_FOX * HD_FOX
C_FV = C_FK + H_FOX * HD_FOX
C_MQ = C_FV + H_FOX * HD_FOX
C_FL = C_MQ + H_MEM * HD_MEM
C_END = C_FL + LANES


def _params(n_grid_axes):
    return pltpu.CompilerParams(
        dimension_semantics=("arbitrary",) * n_grid_axes, vmem_limit_bytes=VMEM_LIMIT)


def _full(shape):
    return pl.BlockSpec(shape, lambda *_: (0,) * len(shape))


def _dot(a, b):
    return jnp.dot(a, b, preferred_element_type=F32)


def _dot_nt(a, b):
    return lax.dot_general(a, b, (((1,), (1,)), ((), ())), preferred_element_type=F32)


def _sigmoid(x):
    return 1.0 / (1.0 + jnp.exp(-x))


def _rms(x, g):
    ms = jnp.mean(x * x, axis=-1, keepdims=True)
    return x * lax.rsqrt(ms + NORM_EPS) * g


def _layer_norm(x, g, b):
    mu = jnp.mean(x, axis=-1, keepdims=True)
    xc = x - mu
    var = jnp.mean(xc * xc, axis=-1, keepdims=True)
    return xc * lax.rsqrt(var + NORM_EPS) * g + b


def _proj_kernel(x_ref, w_ref, wq_ref, wabs_ref, qg_ref, kvg_ref, bf_ref, cos_ref, sin_ref,
                 qcat_ref, kcat_ref, lat_ref, latb_ref, kr_ref, fq_ref, fk_ref, fv_ref,
                 fkb_ref, fvb_ref, mq_ref, logf_ref, logfp_ref):
    xb = x_ref[...].astype(BF16)
    tm = xb.shape[0]

    def proj(c0, width):
        return _dot(xb, w_ref[:, c0:c0 + width])

    cos = cos_ref[...]
    sin = sin_ref[...]
    qn = _rms(proj(C_QLAT, Q_LORA), qg_ref[...]).astype(BF16)
    q2 = _dot(qn, wq_ref[...])
    nq = H_MLA * NOPE_DIM
    nr = H_MLA * ROPE_DIM
    q_rope = (q2[:, nq:nq + nr] * cos + q2[:, nq + nr:nq + 2 * nr] * sin) * MLA_SCALE
    lane = lax.broadcasted_iota(jnp.int32, (tm, LANES), 1)
    heads_per_group = LANES // ROPE_DIM
    for j in range(H_MLA // 2):
        q_abs = _dot(q2[:, j * LANES:(j + 1) * LANES].astype(BF16), wabs_ref[j]) * MLA_SCALE
        for half in range(2):
            h = 2 * j + half
            grp = h // heads_per_group
            qr = q_rope[:, grp * LANES:(grp + 1) * LANES]
            qr = jnp.where(lane // ROPE_DIM == h % heads_per_group, qr, 0.0)
            qcat_ref[h, :, 0:LANES] = q_abs[:, half * LANES:(half + 1) * LANES].astype(BF16)
            qcat_ref[h, :, LANES:2 * LANES] = qr.astype(BF16)
    lat = _rms(proj(C_CKV, KV_LORA), kvg_ref[...])
    lat_ref[...] = lat
    ones = jnp.ones((tm, LANES), BF16)
    latb_ref[:, 0:KV_LORA] = lat.astype(BF16)
    latb_ref[:, KV_LORA:KV_LORA + LANES] = ones
    kr = proj(C_KR, LANES) * cos[:, :LANES] + proj(C_KRR, LANES) * sin[:, :LANES]
    kr_ref[...] = kr[:, :ROPE_DIM]
    kcat_ref[:, 0:KV_LORA] = lat.astype(BF16)
    kcat_ref[:, KV_LORA:KV_LORA + LANES] = kr.astype(BF16)
    fq_ref[...] = (proj(C_FQ, H_FOX * HD_FOX) * FOX_SCALE).astype(BF16)
    fk = proj(C_FK, H_FOX * HD_FOX)
    fk_ref[...] = fk
    fkb_ref[...] = fk.astype(BF16)
    fv = proj(C_FV, H_FOX * HD_FOX)
    fv_ref[...] = fv
    for j in range(H_FOX * HD_FOX // LANES):
        fvb_ref[:, 2 * j * LANES:(2 * j + 1) * LANES] = fv[:, j * LANES:(j + 1) * LANES].astype(BF16)
        fvb_ref[:, (2 * j + 1) * LANES:(2 * j + 2) * LANES] = ones
    mq_ref[...] = proj(C_MQ, H_MEM * HD_MEM).astype(BF16)
    z = proj(C_FL, LANES) + bf_ref[...]
    log_f = jnp.minimum(z, 0.0) - jnp.log1p(jnp.exp(-jnp.abs(z)))
    logfp_ref[...] = log_f
    logf_ref[...] = log_f[:, :H_FOX]


def _proj(x, w_main, w_q, w_abs, q_g, kv_g, b_f, cos_t, sin_t, tm):
    t = x.shape[0]
    n_tab = cos_t.shape[0] // tm
    row = lambda w: pl.BlockSpec((tm, w), lambda i: (i, 0))
    tab = pl.BlockSpec((tm, 2 * LANES), lambda i: (i % n_tab, 0))
    hw = H_FOX * HD_FOX
    out_shape = (
        jax.ShapeDtypeStruct((H_MLA, t, 2 * LANES), BF16),
        jax.ShapeDtypeStruct((t, 2 * LANES), BF16),
        jax.ShapeDtypeStruct((t, KV_LORA), F32),
        jax.ShapeDtypeStruct((t, KV_LORA + LANES), BF16),
        jax.ShapeDtypeStruct((t, ROPE_DIM), F32),
        jax.ShapeDtypeStruct((t, hw), BF16),
        jax.ShapeDtypeStruct((t, hw), F32),
        jax.ShapeDtypeStruct((t, hw), F32),
        jax.ShapeDtypeStruct((t, hw), BF16),
        jax.ShapeDtypeStruct((t, 2 * hw), BF16),
        jax.ShapeDtypeStruct((t, H_MEM * HD_MEM), BF16),
        jax.ShapeDtypeStruct((t, H_FOX), F32),
        jax.ShapeDtypeStruct((t, LANES), F32),
    )
    out_specs = (
        pl.BlockSpec((H_MLA, tm, 2 * LANES), lambda i: (0, i, 0)),
        row(2 * LANES), row(KV_LORA), row(KV_LORA + LANES), row(ROPE_DIM), row(hw), row(hw), row(hw),
        row(hw), row(2 * hw), row(H_MEM * HD_MEM), row(H_FOX), row(LANES),
    )
    return pl.pallas_call(
        _proj_kernel,
        grid=(t // tm,),
        in_specs=[row(D_MODEL), _full(w_main.shape), _full(w_q.shape), _full(w_abs.shape),
                  _full(q_g.shape), _full(kv_g.shape), _full(b_f.shape), tab, tab],
        out_specs=out_specs,
        out_shape=out_shape,
        compiler_params=_params(1),
    )(x, w_main, w_q, w_abs, q_g, kv_g, b_f, cos_t, sin_t)


def _memkv_kernel(x_ref, wk_ref, wv_ref, k_ref, v_ref, kb_ref, vb_ref):
    xb = x_ref[...].astype(BF16)
    k = _dot(xb, wk_ref[...])
    v = _dot(xb, wv_ref[...])
    k_ref[...] = k
    v_ref[...] = v
    kb_ref[...] = k.astype(BF16)
    vb_ref[...] = v.astype(BF16)


def _memkv(x, wk, wv, tm):
    t = x.shape[0]
    w = H_MEM * HD_MEM
    row = lambda n: pl.BlockSpec((tm, n), lambda i: (i, 0))
    return pl.pallas_call(
        _memkv_kernel,
        grid=(t // tm,),
        in_specs=[row(D_MODEL), _full(wk.shape), _full(wv.shape)],
        out_specs=(row(w), row(w), row(w), row(w)),
        out_shape=(jax.ShapeDtypeStruct((t, w), F32), jax.ShapeDtypeStruct((t, w), F32),
                   jax.ShapeDtypeStruct((t, w), BF16), jax.ShapeDtypeStruct((t, w), BF16)),
        compiler_params=_params(1),
    )(x, wk, wv)


CUM_BLOCK = 256


def _cumsum_kernel(x_ref, c_ref):
    s = x_ref.shape[1]
    r = lax.broadcasted_iota(jnp.int32, (CUM_BLOCK, CUM_BLOCK), 0)
    c = lax.broadcasted_iota(jnp.int32, (CUM_BLOCK, CUM_BLOCK), 1)
    tri = jnp.where(r >= c, 1.0, 0.0).astype(BF16)

    def body(i, carry):
        r0 = pl.multiple_of(i * CUM_BLOCK, CUM_BLOCK)
        x = x_ref[0, pl.ds(r0, CUM_BLOCK), :]
        hi = x.astype(BF16)
        r1 = x - hi.astype(F32)
        mid = r1.astype(BF16)
        lo = (r1 - mid.astype(F32)).astype(BF16)
        cs = _dot(tri, hi) + _dot(tri, mid) + _dot(tri, lo) + carry
        c_ref[0, pl.ds(r0, CUM_BLOCK), :] = cs
        return cs[CUM_BLOCK - 1:CUM_BLOCK, :]

    lax.fori_loop(0, s // CUM_BLOCK, body, jnp.zeros((1, LANES), F32))


def _cumsum(x):
    b, s, _ = x.shape
    spec = pl.BlockSpec((1, s, LANES), lambda i: (i, 0, 0))
    return pl.pallas_call(
        _cumsum_kernel, grid=(b,), in_specs=[spec], out_specs=spec,
        out_shape=jax.ShapeDtypeStruct(x.shape, F32), compiler_params=_params(1),
    )(x)


def _online_softmax_step(s, v_ones, m_ref, acc_ref, shift=None):
    def lanes(x, width):
        return jnp.concatenate([x] * (width // LANES), axis=-1)

    m = m_ref[...]
    r = jnp.max(s, axis=-1, keepdims=True)
    m_new = jnp.maximum(m, r if shift is None else r + shift)
    alpha = jnp.exp(m - m_new)
    p = jnp.exp(s - lanes(m_new if shift is None else m_new - shift, s.shape[-1]))
    m_ref[...] = m_new
    acc_ref[...] = lanes(alpha, acc_ref.shape[-1]) * acc_ref[...] + _dot(p.astype(BF16), v_ones)


def _init_softmax_state(m_ref, acc_ref):
    m_ref[...] = jnp.full(m_ref.shape, NEG_INF, F32)
    acc_ref[...] = jnp.zeros(acc_ref.shape, F32)


def _normalised(acc):
    return acc[:, :LANES] / acc[:, LANES:]


def _kv_tile_counts(first_visible_end, last_visible_end, tk, kv_len, sk):
    n_full = jnp.minimum(first_visible_end // tk, kv_len // tk)
    n_any = jnp.minimum((jnp.minimum(last_visible_end, kv_len) + tk - 1) // tk, sk // tk)
    return n_full, n_any


def _kv_loop(n_full, n_any, consume):
    def body(kt, carry, masked):
        consume(kt, masked)
        return carry

    lax.fori_loop(0, n_full, functools.partial(body, masked=False), 0)
    lax.fori_loop(n_full, n_any, functools.partial(body, masked=True), 0)


def _kv_loop_pipelined(n_any, produce, consume):
    produce(0, 0)
    n_pairs = (n_any - 1) // 2

    def body(i, carry):
        kt = 2 * i
        produce(1, kt + 1)
        consume(0, kt, False)
        produce(0, kt + 2)
        consume(1, kt + 1, False)
        return carry

    lax.fori_loop(0, n_pairs, body, 0)
    kt0 = 2 * n_pairs

    @pl.when(kt0 == n_any - 1)
    def _():
        consume(0, kt0, True)

    @pl.when(kt0 < n_any - 1)
    def _():
        produce(1, kt0 + 1)
        consume(0, kt0, True)
        consume(1, kt0 + 1, True)


def _mla_kernel(q_ref, k_ref, v_ref, o_ref, m_ref, acc_ref, s_ref, *, tq, tk, kv_len, q_pos0):
    sk = k_ref.shape[1]
    rows = H_MLA * tq
    q_start = q_pos0 + pl.program_id(1) * tq
    _, n_any = _kv_tile_counts((q_start // CHUNK + 1) * CHUNK,
                               ((q_start + tq - 1) // CHUNK + 1) * CHUNK, tk, kv_len, sk)
    q = q_ref[...].reshape(rows, q_ref.shape[-1])
    q_chunk = (q_start + lax.broadcasted_iota(jnp.int32, (rows, 1), 0) % tq) // CHUNK

    def produce(slot, kt):
        s_ref[slot] = _dot_nt(q, k_ref[0, pl.ds(pl.multiple_of(kt * tk, tk), tk), :])

    def consume(slot, kt, masked):
        k0 = pl.multiple_of(kt * tk, tk)
        s = s_ref[slot]
        if masked:
            k_pos = k0 + lax.broadcasted_iota(jnp.int32, (1, tk), 1)
            mask = k_pos // CHUNK <= q_chunk
            if kv_len < sk:
                mask = jnp.logical_and(mask, k_pos < kv_len)
            s = jnp.where(mask, s, NEG_INF)
        _online_softmax_step(s, v_ref[0, pl.ds(k0, tk), :], m_ref, acc_ref)

    _init_softmax_state(m_ref, acc_ref)
    _kv_loop_pipelined(n_any, produce, consume)
    o = _normalised(acc_ref[...]).astype(BF16)
    for h in range(H_MLA):
        o_ref[:, h * KV_LORA:(h + 1) * KV_LORA] = o[h * tq:(h + 1) * tq, :]


def _mla_attention(qcat, kcat, latb, batch, sq, tq, tk, kv_len, q_pos0):
    sk = kcat.shape[1]
    nq = sq // tq
    assert tk % tq == 0 and q_pos0 % tq == 0 and tq % CHUNK in (0, tq)
    kern = functools.partial(_mla_kernel, tq=tq, tk=tk, kv_len=kv_len, q_pos0=q_pos0)
    return pl.pallas_call(
        kern,
        grid=(batch, nq),
        in_specs=[pl.BlockSpec((H_MLA, tq, 2 * LANES), lambda b, i: (0, b * nq + i, 0)),
                  pl.BlockSpec((1, sk, 2 * LANES), lambda b, i: (b, 0, 0)),
                  pl.BlockSpec((1, sk, 2 * LANES), lambda b, i: (b, 0, 0))],
        out_specs=pl.BlockSpec((tq, H_MLA * KV_LORA), lambda b, i: (b * nq + i, 0)),
        out_shape=jax.ShapeDtypeStruct((batch * sq, H_MLA * KV_LORA), BF16),
        scratch_shapes=[pltpu.VMEM((H_MLA * tq, LANES), F32),
                        pltpu.VMEM((H_MLA * tq, 2 * LANES), F32),
                        pltpu.VMEM((2, H_MLA * tq, tk), F32)],
        compiler_params=_params(2),
    )(qcat, kcat, latb)


def _fox_kernel(q_ref, k_ref, v_ref, cq_ref, ck_ref, o_ref, m_ref, acc_ref, cqb_ref, *, tq, tk,
                kv_len, q_pos0):
    sk = k_ref.shape[1]
    q_start = q_pos0 + pl.program_id(1) * tq
    n_full, n_any = _kv_tile_counts(q_start + 1, q_start + tq, tk, kv_len, sk)
    lane = lax.broadcasted_iota(jnp.int32, (tq, LANES), 1)
    n_pairs = H_FOX * HD_FOX // LANES
    q_pos = q_start + lax.broadcasted_iota(jnp.int32, (2 * tq, 1), 0) % tq
    q2 = []
    for j in range(n_pairs):
        q_pair = q_ref[:, j * LANES:(j + 1) * LANES]
        zero = jnp.zeros_like(q_pair)
        q2.append(jnp.concatenate([jnp.where(lane < HD_FOX, q_pair, zero),
                                   jnp.where(lane < HD_FOX, zero, q_pair)], axis=0))
        for half in range(2):
            col = jnp.sum(jnp.where(lane == 2 * j + half, cq_ref[0], 0.0), axis=-1, keepdims=True)
            cqb_ref[j, half * tq:(half + 1) * tq, :] = jnp.broadcast_to(col, (tq, LANES))

    def consume(kt, masked):
        k0 = pl.multiple_of(kt * tk, tk)
        if masked:
            k_pos = k0 + lax.broadcasted_iota(jnp.int32, (1, tk), 1)
            mask = k_pos <= q_pos
            if kv_len < sk:
                mask = jnp.logical_and(mask, k_pos < kv_len)
        for j in range(n_pairs):
            s = _dot_nt(q2[j], k_ref[0, pl.ds(k0, tk), j * LANES:(j + 1) * LANES])
            c_k = jnp.concatenate(
                [jnp.broadcast_to(ck_ref[0, 2 * j:2 * j + 1, pl.ds(k0, tk)], (tq, tk)),
                 jnp.broadcast_to(ck_ref[0, 2 * j + 1:2 * j + 2, pl.ds(k0, tk)], (tq, tk))], axis=0)
            s = s - c_k
            if masked:
                s = jnp.where(mask, s, NEG_INF)
            v_ones = v_ref[0, pl.ds(k0, tk), 2 * j * LANES:2 * (j + 1) * LANES]
            _online_softmax_step(s, v_ones, m_ref.at[j], acc_ref.at[j], shift=cqb_ref[j])

    _init_softmax_state(m_ref, acc_ref)
    _kv_loop(n_full, n_any, consume)
    for j in range(n_pairs):
        o = _normalised(acc_ref[j])
        o_ref[:, j * LANES:(j + 1) * LANES] = jnp.where(lane < HD_FOX, o[:tq], o[tq:]).astype(BF16)


def _fox_attention(fq, fkb, fvb, c_q, c_k, batch, sq, tq, tk, kv_len, q_pos0):
    sk = fkb.shape[1]
    nq = sq // tq
    w = H_FOX * HD_FOX
    kern = functools.partial(_fox_kernel, tq=tq, tk=tk, kv_len=kv_len, q_pos0=q_pos0)
    return pl.pallas_call(
        kern,
        grid=(batch, nq),
        in_specs=[pl.BlockSpec((tq, w), lambda b, i: (b * nq + i, 0)),
                  pl.BlockSpec((1, sk, w), lambda b, i: (b, 0, 0)),
                  pl.BlockSpec((1, sk, 2 * w), lambda b, i: (b, 0, 0)),
                  pl.BlockSpec((1, tq, LANES), lambda b, i: (b, i, 0)),
                  pl.BlockSpec((1, H_FOX, sk), lambda b, i: (b, 0, 0))],
        out_specs=pl.BlockSpec((tq, w), lambda b, i: (b * nq + i, 0)),
        out_shape=jax.ShapeDtypeStruct((batch * sq, w), BF16),
        scratch_shapes=[pltpu.VMEM((w // LANES, 2 * tq, LANES), F32),
                        pltpu.VMEM((w // LANES, 2 * tq, 2 * LANES), F32),
                        pltpu.VMEM((w // LANES, 2 * tq, LANES), F32)],
        compiler_params=_params(2),
    )(fq, fkb, fvb, c_q, c_k)


def _mem_kernel(q_ref, k_ref, v_ref, o_ref):
    for h in range(H_MEM):
        cols = slice(h * HD_MEM, (h + 1) * HD_MEM)
        s = _dot_nt(q_ref[:, cols], k_ref[0, :, cols]) * MEM_SCALE
        m = jnp.max(s, axis=-1, keepdims=True)
        p = jnp.exp(s - m)
        l = jnp.sum(p, axis=-1, keepdims=True)
        o = _dot(p.astype(BF16), v_ref[0, :, cols]) / l
        o_ref[:, cols] = o.astype(BF16)


def _mem_attention(mq, mkb, mvb, batch, sq, tq):
    nq = sq // tq
    n_mem = mkb.shape[1]
    w = H_MEM * HD_MEM
    return pl.pallas_call(
        _mem_kernel,
        grid=(batch, nq),
        in_specs=[pl.BlockSpec((tq, w), lambda b, i: (b * nq + i, 0)),
                  pl.BlockSpec((1, n_mem, w), lambda b, i: (b, 0, 0)),
                  pl.BlockSpec((1, n_mem, w), lambda b, i: (b, 0, 0))],
        out_specs=pl.BlockSpec((tq, w), lambda b, i: (b * nq + i, 0)),
        out_shape=jax.ShapeDtypeStruct((batch * sq, w), BF16),
        compiler_params=_params(2),
    )(mq, mkb, mvb)


def _route(x1b, wr_t, bias):
    tm = x1b.shape[0]
    scores = _sigmoid(_dot_nt(wr_t, x1b))
    sel = scores + bias
    sub = lax.broadcasted_iota(jnp.int32, (GROUP_SIZE, tm), 0)
    gs = []
    for g in range(N_GROUPS):
        blk = sel[g * GROUP_SIZE:(g + 1) * GROUP_SIZE, :]
        m1 = jnp.max(blk, axis=0, keepdims=True)
        first = jnp.min(jnp.where(blk == m1, sub, GROUP_SIZE), axis=0, keepdims=True)
        m2 = jnp.max(jnp.where(sub == first, -jnp.inf, blk), axis=0, keepdims=True)
        gs.append(m1 + m2)
    e_idx = lax.broadcasted_iota(jnp.int32, (N_EXPERTS, tm), 0)
    allowed = jnp.zeros((N_EXPERTS, tm), jnp.bool_)
    for g in range(N_GROUPS):
        rank = jnp.zeros((1, tm), jnp.int32)
        for o in range(N_GROUPS):
            if o == g:
                continue
            beats = (gs[o] >= gs[g]) if o < g else (gs[o] > gs[g])
            rank = rank + beats.astype(jnp.int32)
        keep = rank < TOPK_GROUPS
        allowed = jnp.logical_or(allowed, jnp.logical_and(e_idx // GROUP_SIZE == g, keep))
    cand = jnp.where(allowed, sel, -jnp.inf)
    chosen = jnp.zeros((N_EXPERTS, tm), jnp.bool_)
    for _ in range(TOP_K):
        m = jnp.max(cand, axis=0, keepdims=True)
        first = jnp.min(jnp.where(cand == m, e_idx, N_EXPERTS), axis=0, keepdims=True)
        pick = e_idx == first
        chosen = jnp.logical_or(chosen, pick)
        cand = jnp.where(pick, -jnp.inf, cand)
    w = jnp.where(chosen, scores, 0.0)
    return w / jnp.sum(w, axis=0, keepdims=True) * ROUTED_SCALE, chosen


def _swiglu(xb, wg, wu):
    a = _dot(xb, wg)
    return (a * _sigmoid(a)) * _dot(xb, wu)


def _merge_kernel(x_ref, olat_ref, ofox_ref, omem_ref, wg_ref, wuv_ref, wbm_ref, wbf_ref, wbc_ref,
                  wo_ref, g1_ref, b1_ref, wsg_ref, wsu_ref, wsd_ref, x1b_ref, base_ref):
    x = x_ref[...]
    xb = x.astype(BF16)
    o_mla = jnp.concatenate(
        [_dot(olat_ref[:, j * 2 * KV_LORA:(j + 1) * 2 * KV_LORA], wuv_ref[j]).astype(BF16)
         for j in range(H_MLA // 2)], axis=-1)

    def gate(n):
        return _sigmoid(_dot(xb, wg_ref[:, n * D_MODEL:(n + 1) * D_MODEL]))

    y = gate(0) * _dot(o_mla, wbm_ref[...])
    y = y + gate(1) * _dot(ofox_ref[...], wbf_ref[...])
    y = y + gate(2) * _dot(omem_ref[...], wbc_ref[...])
    mix = _dot(y.astype(BF16), wo_ref[...])
    x1 = _layer_norm(ALPHA * x + mix, g1_ref[...], b1_ref[...])
    x1b = x1.astype(BF16)
    x1b_ref[...] = x1b
    shared = _dot(_swiglu(x1b, wsg_ref[...], wsu_ref[...]).astype(BF16), wsd_ref[...])
    base_ref[...] = ALPHA * x1 + shared


def _merge(x, olat, ofox, omem, wg, wuv, wbm, wbf, wbc, wo, g1, b1, wsg, wsu, wsd, tm):
    t = x.shape[0]
    row = lambda n: pl.BlockSpec((tm, n), lambda i: (i, 0))
    ws = (wg, wuv, wbm, wbf, wbc, wo, g1, b1, wsg, wsu, wsd)
    return pl.pallas_call(
        _merge_kernel,
        grid=(t // tm,),
        in_specs=[row(D_MODEL), row(H_MLA * KV_LORA), row(H_FOX * HD_FOX), row(H_MEM * HD_MEM)]
        + [_full(w.shape) for w in ws],
        out_specs=(row(D_MODEL), row(D_MODEL)),
        out_shape=(jax.ShapeDtypeStruct((t, D_MODEL), BF16), jax.ShapeDtypeStruct((t, D_MODEL), F32)),
        compiler_params=_params(1),
    )(x, olat, ofox, omem, *ws)


BF16_ROWS = 16
SLOT_BLOCK = BF16_ROWS
PLACE_ROWS = 512
EXPERT_ROWS = 2048
EXPERT_CHUNK = 512
EXPERTS_PER_STEP = 4


def _capacity(tm):
    return min(tm, _round_up(2 * tm * TOP_K // N_EXPERTS, BF16_ROWS))


def _dispatch_kernel(x1b_ref, wr_ref, rb_ref, xs_ref, gate_ref, slot_ref, govf_ref, flag_ref,
                     pm_ref, *, cap):
    xb = x1b_ref[...]
    tm = xb.shape[0]
    gate, chosen = _route(xb, wr_ref[...], rb_ref[...])
    r = lax.broadcasted_iota(jnp.int32, (tm, tm), 0)
    c = lax.broadcasted_iota(jnp.int32, (tm, tm), 1)
    earlier = jnp.where(r < c, 1.0, 0.0).astype(BF16)
    rank = _dot(jnp.where(chosen, 1.0, 0.0).astype(BF16), earlier)
    fits = jnp.logical_and(chosen, rank < cap)
    slot = jnp.where(fits, rank, -1.0)
    gate_ref[...] = jnp.where(fits, gate, 0.0)
    slot_ref[...] = slot
    g_ovf = jnp.where(fits, 0.0, gate)
    govf_ref[...] = jnp.concatenate([g_ovf, jnp.zeros((LANES - N_EXPERTS, tm), F32)], axis=0).T
    any_ovf = jnp.max(jnp.max(g_ovf, axis=0, keepdims=True), axis=1, keepdims=True)
    flag_ref[0] = jnp.broadcast_to(jnp.where(any_ovf > 0.0, 1, 0), flag_ref.shape[1:]).astype(jnp.int32)
    _fill_placement(pm_ref, slot, None, cap)
    per = PLACE_ROWS // SLOT_BLOCK
    for g in range(N_EXPERTS * cap // PLACE_ROWS):
        b, eh = divmod(g, N_EXPERTS // per)
        xs = _dot(pm_ref[g * PLACE_ROWS:(g + 1) * PLACE_ROWS, :], xb).astype(BF16)
        xs_ref[eh * per:(eh + 1) * per, b] = xs.reshape(per, SLOT_BLOCK, D_MODEL)


def _fill_placement(p_ref, slot, gate, cap):
    tm = slot.shape[1]
    s_iota = lax.broadcasted_iota(jnp.int32, (SLOT_BLOCK, tm), 0).astype(F32)
    for e in range(N_EXPERTS):
        value = 1.0 if gate is None else gate[e:e + 1, :]
        for b in range(cap // SLOT_BLOCK):
            r0 = (b * N_EXPERTS + e) * SLOT_BLOCK
            hit = s_iota + float(b * SLOT_BLOCK) == slot[e:e + 1, :]
            p_ref[r0:r0 + SLOT_BLOCK, :] = jnp.where(hit, value, 0.0).astype(BF16)


def _dispatch(x1b, wr_t, rb, tm, cap):
    t = x1b.shape[0]
    nt = t // tm
    nb = cap // SLOT_BLOCK
    col = pl.BlockSpec((N_EXPERTS, tm), lambda i: (0, i))
    return pl.pallas_call(
        functools.partial(_dispatch_kernel, cap=cap),
        grid=(nt,),
        in_specs=[pl.BlockSpec((tm, D_MODEL), lambda i: (i, 0)), _full(wr_t.shape), _full(rb.shape)],
        out_specs=(pl.BlockSpec((N_EXPERTS, nb, SLOT_BLOCK, D_MODEL), lambda i: (0, 0, i, 0)), col, col,
                   pl.BlockSpec((tm, LANES), lambda i: (i, 0)),
                   pl.BlockSpec((1, 8, LANES), lambda i: (i, 0, 0))),
        out_shape=(jax.ShapeDtypeStruct((N_EXPERTS, nb, nt * SLOT_BLOCK, D_MODEL), BF16),
                   jax.ShapeDtypeStruct((N_EXPERTS, t), F32),
                   jax.ShapeDtypeStruct((N_EXPERTS, t), F32),
                   jax.ShapeDtypeStruct((t, LANES), F32),
                   jax.ShapeDtypeStruct((nt, 8, LANES), jnp.int32)),
        scratch_shapes=[pltpu.VMEM((N_EXPERTS * cap, tm), BF16)],
        compiler_params=_params(1),
    )(x1b, wr_t, rb)


def _expert_kernel(need_ref, xs_ref, wg_ref, wu_ref, wd_ref, ys_ref, wgb_ref, wub_ref, wdb_ref, *,
                   chunk):
    e, b = pl.program_id(0), pl.program_id(1)

    @pl.when(jnp.logical_and(b == 0, pl.program_id(2) == 0))
    def _():
        wgb_ref[...] = wg_ref[0].astype(BF16)
        wub_ref[...] = wu_ref[0].astype(BF16)
        wdb_ref[...] = wd_ref[0].astype(BF16)

    @pl.when(b < need_ref[e])
    def _():
        def body(c, carry):
            r0 = pl.multiple_of(c * chunk, chunk)
            hid = _swiglu(xs_ref[0, 0, pl.ds(r0, chunk), :], wgb_ref[...], wub_ref[...])
            ys_ref[0, 0, pl.ds(r0, chunk), :] = _dot(hid.astype(BF16), wdb_ref[...]).astype(BF16)
            return carry

        lax.fori_loop(0, xs_ref.shape[2] // chunk, body, 0)

    @pl.when(b >= need_ref[e])
    def _():
        ys_ref[...] = jnp.zeros_like(ys_ref)


def _experts(need, xs, weg, weu, wed):
    n_e, nb, rows, _ = xs.shape
    rb = min(EXPERT_ROWS, rows)
    n_i = rows // rb

    def x_map(e, b, i, need):
        live = b < need[e]
        return (e, jnp.where(live, b, jnp.maximum(need[e] - 1, 0)), jnp.where(live, i, n_i - 1), 0)

    w_spec = lambda r, c: pl.BlockSpec((1, r, c), lambda e, b, i, need: (e, 0, 0))
    return pl.pallas_call(
        functools.partial(_expert_kernel, chunk=min(EXPERT_CHUNK, rb)),
        grid_spec=pltpu.PrefetchScalarGridSpec(
            num_scalar_prefetch=1,
            grid=(n_e, nb, n_i),
            in_specs=[pl.BlockSpec((1, 1, rb, D_MODEL), x_map),
                      w_spec(D_MODEL, D_EXPERT), w_spec(D_MODEL, D_EXPERT), w_spec(D_EXPERT, D_MODEL)],
            out_specs=pl.BlockSpec((1, 1, rb, D_MODEL), lambda e, b, i, need: (e, b, i, 0)),
            scratch_shapes=[pltpu.VMEM((D_MODEL, D_EXPERT), BF16),
                            pltpu.VMEM((D_MODEL, D_EXPERT), BF16),
                            pltpu.VMEM((D_EXPERT, D_MODEL), BF16)]),
        out_shape=jax.ShapeDtypeStruct(xs.shape, BF16),
        compiler_params=_params(3),
    )(need, xs, weg, weu, wed)


def _dense_routed_kernel(flag_ref, x1b_ref, gate_ref, weg_ref, weu_ref, wed_ref, out_ref):
    step = pl.program_id(1)

    @pl.when(step == 0)
    def _():
        out_ref[...] = jnp.zeros_like(out_ref)

    @pl.when(flag_ref[pl.program_id(0)] > 0)
    def _():
        xb = x1b_ref[...]
        gate = gate_ref[...]
        lane = lax.broadcasted_iota(jnp.int32, gate.shape, 1)
        for j in range(EXPERTS_PER_STEP):
            e = step * EXPERTS_PER_STEP + j
            g_col = jnp.sum(jnp.where(lane == e, gate, 0.0), axis=-1, keepdims=True)
            hid = _swiglu(xb, weg_ref[j], weu_ref[j])
            out_ref[...] += g_col * _dot(hid.astype(BF16), wed_ref[j])


def _dense_routed(flags, x1b, gate, weg, weu, wed, tm):
    t = x1b.shape[0]
    eb = EXPERTS_PER_STEP
    row = lambda n: pl.BlockSpec((tm, n), lambda i, s, f: (i, 0))
    wspec = lambda a, b: pl.BlockSpec((eb, a, b), lambda i, s, f: (jnp.where(f[i] > 0, s, 0), 0, 0))
    return pl.pallas_call(
        _dense_routed_kernel,
        grid_spec=pltpu.PrefetchScalarGridSpec(
            num_scalar_prefetch=1,
            grid=(t // tm, N_EXPERTS // eb),
            in_specs=[row(D_MODEL), row(LANES), wspec(D_MODEL, D_EXPERT), wspec(D_MODEL, D_EXPERT),
                      wspec(D_EXPERT, D_MODEL)],
            out_specs=row(D_MODEL)),
        out_shape=jax.ShapeDtypeStruct((t, D_MODEL), F32),
        compiler_params=_params(2),
    )(flags, x1b, gate, weg, weu, wed)


def _combine_kernel(need_ref, gate_ref, slot_ref, *rest, cap, has_overflow):
    nb = cap // SLOT_BLOCK
    ys_refs, rest = rest[:nb], rest[nb:]
    if has_overflow:
        base_ref, rovf_ref, g2_ref, b2_ref, y_ref, pw_ref, acc_ref = rest
    else:
        base_ref, g2_ref, b2_ref, y_ref, pw_ref, acc_ref = rest
    need = need_ref[pl.program_id(0)]
    _fill_placement(pw_ref, slot_ref[...], gate_ref[...], cap)
    acc_ref[...] = rovf_ref[...] if has_overflow else jnp.zeros(acc_ref.shape, F32)
    per = PLACE_ROWS // SLOT_BLOCK
    for b in range(nb):
        @pl.when(b < need)
        def _():
            for eh in range(N_EXPERTS // per):
                r0 = (b * N_EXPERTS + eh * per) * SLOT_BLOCK
                ys = ys_refs[b][eh * per:(eh + 1) * per, 0].reshape(PLACE_ROWS, D_MODEL)
                acc_ref[...] += lax.dot_general(pw_ref[r0:r0 + PLACE_ROWS, :], ys,
                                                (((0,), (0,)), ((), ())), preferred_element_type=F32)
    y_ref[...] = _layer_norm(base_ref[...] + acc_ref[...], g2_ref[...], b2_ref[...])


def _combine(need, gate_t, slot_t, ys, base, rovf, g2, b2, tm, cap):
    t = base.shape[0]
    nb = cap // SLOT_BLOCK
    col = pl.BlockSpec((N_EXPERTS, tm), lambda i, need: (0, i))
    row = pl.BlockSpec((tm, D_MODEL), lambda i, need: (i, 0))
    ys_specs = [pl.BlockSpec((N_EXPERTS, 1, SLOT_BLOCK, D_MODEL),
                             lambda i, need, b=b: (0, b, jnp.where(b < need[i], i, 0), 0))
                for b in range(nb)]
    has_overflow = rovf is not None
    operands = (gate_t, slot_t) + (ys,) * nb + (base,) + ((rovf,) if has_overflow else ()) + (g2, b2)
    full = lambda a: pl.BlockSpec(a.shape, lambda i, need: (0,) * a.ndim)
    return pl.pallas_call(
        functools.partial(_combine_kernel, cap=cap, has_overflow=has_overflow),
        grid_spec=pltpu.PrefetchScalarGridSpec(
            num_scalar_prefetch=1,
            grid=(t // tm,),
            in_specs=[col, col] + ys_specs + [row] + ([row] if has_overflow else [])
            + [full(g2), full(b2)],
            out_specs=row,
            scratch_shapes=[pltpu.VMEM((N_EXPERTS * cap, tm), BF16), pltpu.VMEM((tm, D_MODEL), F32)]),
        out_shape=jax.ShapeDtypeStruct((t, D_MODEL), F32),
        compiler_params=_params(1),
    )(need, *operands)


def _moe_ln2(x1b, base, wr_t, rb, weg, weu, wed, g2, b2, tm):
    t = x1b.shape[0]
    cap = _capacity(tm)
    assert cap % SLOT_BLOCK == 0 and (N_EXPERTS * SLOT_BLOCK) % PLACE_ROWS == 0
    xs, gate_t, slot_t, gate_ovf, flags = _dispatch(x1b, wr_t, rb, tm, cap)
    blocks = (slot_t.astype(jnp.int32) // SLOT_BLOCK + 1).reshape(N_EXPERTS, t // tm, tm)
    ys = _experts(jnp.max(blocks, axis=(1, 2)), xs, weg, weu, wed)
    tile_need = jnp.max(blocks, axis=(0, 2))
    flags = flags[:, 0, 0]

    def with_overflow():
        rovf = _dense_routed(flags, x1b, gate_ovf, weg.astype(BF16), weu.astype(BF16),
                             wed.astype(BF16), tm)
        return _combine(tile_need, gate_t, slot_t, ys, base, rovf, g2, b2, tm, cap)

    return lax.cond(jnp.any(flags > 0), with_overflow,
                    lambda: _combine(tile_need, gate_t, slot_t, ys, base, None, g2, b2, tm, cap))


def _rot_half_cols(w):
    half = w.shape[-1] // 2
    return jnp.concatenate([-w[..., half:], w[..., :half]], axis=-1)


def _block_diag2(a, b):
    za = jnp.zeros((a.shape[0], b.shape[1]), a.dtype)
    zb = jnp.zeros((b.shape[0], a.shape[1]), a.dtype)
    return jnp.concatenate([jnp.concatenate([a, za], 1), jnp.concatenate([zb, b], 1)], 0)


def _prep_weights(w_in, b_f, q_norm_g, w_q_up, kv_norm_g, w_kv_up):
    widths = (Q_LORA, KV_LORA, ROPE_DIM, H_FOX * HD_FOX, H_FOX * HD_FOX, H_FOX * HD_FOX, H_FOX,
              H_MEM * HD_MEM, N_BRANCH * D_MODEL)
    offs = np.cumsum((0,) + widths)
    w_ql, w_ckv, w_kr, w_fq, w_fk, w_fv, w_fl, w_mq, w_g = (
        w_in[:, offs[i]:offs[i + 1]] for i in range(len(widths)))
    rep = LANES // ROPE_DIM
    w_main = jnp.concatenate(
        [w_ql, w_ckv, jnp.tile(w_kr, (1, rep)), jnp.tile(_rot_half_cols(w_kr), (1, rep)),
         w_fq, w_fk, w_fv, w_mq, jnp.pad(w_fl, ((0, 0), (0, LANES - H_FOX)))], axis=1).astype(BF16)
    assert w_main.shape[1] == C_END
    wq3 = w_q_up.reshape(Q_LORA, H_MLA, NOPE_DIM + ROPE_DIM)
    wq_nope = wq3[:, :, :NOPE_DIM].reshape(Q_LORA, H_MLA * NOPE_DIM)
    wq_rope = wq3[:, :, NOPE_DIM:]
    w_q = jnp.concatenate(
        [wq_nope, wq_rope.reshape(Q_LORA, -1), _rot_half_cols(wq_rope).reshape(Q_LORA, -1)],
        axis=1).astype(BF16)
    wkv3 = w_kv_up.reshape(KV_LORA, H_MLA, NOPE_DIM + V_DIM)
    w_uk = wkv3[:, :, :NOPE_DIM]
    w_uv = wkv3[:, :, NOPE_DIM:]
    w_abs = jnp.stack([_block_diag2(w_uk[:, 2 * j].T, w_uk[:, 2 * j + 1].T)
                       for j in range(H_MLA // 2)]).astype(BF16)
    w_uvp = jnp.stack([_block_diag2(w_uv[:, 2 * j], w_uv[:, 2 * j + 1])
                       for j in range(H_MLA // 2)]).astype(BF16)
    b_fp = jnp.pad(b_f, (0, LANES - H_FOX)).reshape(1, LANES)
    return (w_main, w_q, w_abs, q_norm_g.reshape(1, -1), kv_norm_g.reshape(1, -1), b_fp,
            w_g.astype(BF16), w_uvp)


def _rope_tables(pos):
    half = ROPE_DIM // 2
    inv_freq = ROPE_THETA ** (-jnp.arange(half, dtype=F32) / half)
    ang = pos.astype(F32)[:, None] * inv_freq[None, :]
    rep = 2 * LANES // ROPE_DIM
    cos = jnp.tile(jnp.concatenate([jnp.cos(ang)] * 2, axis=-1), (1, rep))
    sin = jnp.tile(jnp.concatenate([jnp.sin(ang)] * 2, axis=-1), (1, rep))
    return cos, sin


def _round_up(n, m):
    return (n + m - 1) // m * m


def kernel(x_prompt, x_sample, mem_prompt, cache_mla_latent, cache_mla_krope, cache_fox_k,
           cache_fox_v, cache_fox_logf, cache_mem_k, cache_mem_v, ln1_g, ln1_b, w_in, b_f,
           q_norm_g, w_q_up, kv_norm_g, w_kv_up, w_mem_k, w_mem_v, w_br_mla, w_br_fox,
           w_br_mem, w_out, ln2_g, ln2_b, w_router, router_bias, w_e_gate, w_e_up, w_e_down,
           w_s_gate, w_s_up, w_s_down):
    assert w_in.shape[0] == DEPTH == 1
    batch, seq, _ = x_prompt.shape
    dec_batch, dec_seq, _ = x_sample.shape
    past = cache_mla_latent.shape[2]
    n_mem = mem_prompt.shape[1]
    l = 0

    (w_main, w_q, w_abs, q_g, kv_g, b_fp, w_gate, w_uvp) = _prep_weights(
        w_in[l], b_f[l], q_norm_g[l], w_q_up[l], kv_norm_g[l], w_kv_up[l])
    merge_w = (w_gate, w_uvp, w_br_mla[l].astype(BF16), w_br_fox[l].astype(BF16),
               w_br_mem[l].astype(BF16), w_out[l].astype(BF16), ln1_g[l].reshape(1, -1),
               ln1_b[l].reshape(1, -1), w_s_gate[l].astype(BF16), w_s_up[l].astype(BF16),
               w_s_down[l].astype(BF16))
    moe_w = (w_router[l].T.astype(BF16), router_bias[l].reshape(-1, 1),
             w_e_gate[l], w_e_up[l], w_e_down[l],
             ln2_g[l].reshape(1, -1), ln2_b[l].reshape(1, -1))

    def tail(x, olat, ofox, omem, tm, tm_moe):
        x1b, base = _merge(x, olat, ofox, omem, *merge_w, tm)
        return _moe_ln2(x1b, base, *moe_w, tm_moe)

    tp = batch * seq
    xp = x_prompt.reshape(tp, D_MODEL)
    tm_p = min(512, seq)
    tq_p = min(256, seq)
    tk_p = min(512, seq)
    cos_p, sin_p = _rope_tables(jnp.arange(seq, dtype=jnp.int32))
    (qcat, kcat, lat, latb, kr, fq, fk, fv, fkb, fvb, mq, logf, logfp) = _proj(
        xp, w_main, w_q, w_abs, q_g, kv_g, b_fp, cos_p, sin_p, tm_p)
    tmem = batch * n_mem
    m_k, m_v, m_kb, m_vb = _memkv(mem_prompt.reshape(tmem, D_MODEL), w_mem_k[l].astype(BF16),
                                  w_mem_v[l].astype(BF16), min(512, tmem))
    c_p = _cumsum(logfp.reshape(batch, seq, LANES))
    c_row = jnp.swapaxes(c_p[:, :, :H_FOX], 1, 2)
    olat = _mla_attention(qcat, kcat.reshape(batch, seq, -1), latb.reshape(batch, seq, -1),
                          batch, seq, tq_p, tk_p, seq, 0)
    ofox = _fox_attention(fq, fkb.reshape(batch, seq, -1), fvb.reshape(batch, seq, -1), c_p, c_row,
                          batch, seq, tq_p, tk_p, seq, 0)
    omem = _mem_attention(mq, m_kb.reshape(batch, n_mem, -1), m_vb.reshape(batch, n_mem, -1),
                          batch, seq, tq_p)
    y_p = tail(xp, olat, ofox, omem, tm_p, min(256, seq)).reshape(batch, seq, D_MODEL)

    ts = dec_batch * dec_seq
    xs = x_sample.reshape(ts, D_MODEL)
    cos_s, sin_s = _rope_tables(jnp.tile(past + jnp.arange(dec_seq, dtype=jnp.int32), dec_batch))
    (qcat_s, kcat_s, lat_s, latb_s, kr_s, fq_s, fk_s, fv_s, fkb_s, fvb_s, mq_s, logf_s,
     logfp_s) = _proj(xs, w_main, w_q, w_abs, q_g, kv_g, b_fp, cos_s, sin_s, ts)
    kv_len = past + dec_seq
    tk_s = 256
    sk = _round_up(kv_len, tk_s)

    def with_cache(cache, new):
        new = new.reshape(dec_batch, dec_seq, -1)
        pad = jnp.zeros((dec_batch, sk - kv_len, new.shape[-1]), new.dtype)
        return jnp.concatenate([cache.astype(new.dtype), new, pad], axis=1)

    rep = LANES // ROPE_DIM
    kcat_all = with_cache(
        jnp.concatenate([cache_mla_latent[l], jnp.tile(cache_mla_krope[l], (1, 1, rep))], axis=-1),
        kcat_s)
    lat_all = with_cache(
        jnp.concatenate([cache_mla_latent[l], jnp.ones((dec_batch, past, LANES), F32)], axis=-1), latb_s)
    fk_all = with_cache(cache_fox_k[l].reshape(dec_batch, past, -1), fkb_s)
    n_pairs = H_FOX * HD_FOX // LANES
    fv_pairs = cache_fox_v[l].reshape(dec_batch, past, n_pairs, LANES)
    fv_all = with_cache(
        jnp.concatenate([fv_pairs, jnp.ones_like(fv_pairs)], axis=-1).reshape(dec_batch, past, -1),
        fvb_s)
    logf_all = with_cache(jnp.pad(cache_fox_logf[l], ((0, 0), (0, 0), (0, LANES - H_FOX))), logfp_s)
    c_s = _cumsum(logf_all)
    c_row_s = jnp.swapaxes(c_s[:, :, :H_FOX], 1, 2)
    c_q_s = c_s[:, past:kv_len]
    olat_s = _mla_attention(qcat_s, kcat_all, lat_all, dec_batch, dec_seq, dec_seq, tk_s, kv_len, past)
    ofox_s = _fox_attention(fq_s, fk_all, fv_all, c_q_s, c_row_s, dec_batch, dec_seq, dec_seq, tk_s,
                            kv_len, past)
    omem_s = _mem_attention(mq_s, cache_mem_k[l].reshape(dec_batch, n_mem, -1).astype(BF16),
                            cache_mem_v[l].reshape(dec_batch, n_mem, -1).astype(BF16),
                            dec_batch, dec_seq, dec_seq)
    y_s = tail(xs, olat_s, ofox_s, omem_s, ts, ts).reshape(dec_batch, dec_seq, D_MODEL)

    def stack(a, b, s, *tail_shape):
        return a.reshape(1, b, s, *tail_shape)

    return (y_p, y_s,
            stack(lat, batch, seq, KV_LORA), stack(kr, batch, seq, ROPE_DIM),
            stack(fk, batch, seq, H_FOX, HD_FOX), stack(fv, batch, seq, H_FOX, HD_FOX),
            stack(logf, batch, seq, H_FOX),
            stack(m_k, batch, n_mem, H_MEM, HD_MEM), stack(m_v, batch, n_mem, H_MEM, HD_MEM),
            stack(lat_s, dec_batch, dec_seq, KV_LORA), stack(kr_s, dec_batch, dec_seq, ROPE_DIM),
            stack(fk_s, dec_batch, dec_seq, H_FOX, HD_FOX), stack(fv_s, dec_batch, dec_seq, H_FOX, HD_FOX),
            stack(logf_s, dec_batch, dec_seq, H_FOX))
```

```python
import functools

import jax
import jax.numpy as jnp
import numpy as np
from jax import lax
from jax.experimental import pallas as pl
from jax.experimental.pallas import tpu as pltpu

F32 = jnp.float32
BF16 = jnp.bfloat16

D_MODEL = 1024
CHUNK = 64
H_MLA = 8
Q_LORA = 256
KV_LORA = 128
NOPE_DIM = 64
ROPE_DIM = 32
V_DIM = 64
ROPE_THETA = 10000.0
MLA_SCALE = (NOPE_DIM + ROPE_DIM) ** -0.5
H_FOX = 8
HD_FOX = 64
FOX_SCALE = HD_FOX ** -0.5
H_MEM = 4
HD_MEM = 128
MEM_SCALE = HD_MEM ** -0.5
N_BRANCH = 3
N_EXPERTS = 64
TOP_K = 8
N_GROUPS = 8
GROUP_SIZE = N_EXPERTS // N_GROUPS
TOPK_GROUPS = 4
D_EXPERT = 256
ROUTED_SCALE = 2.5
DEPTH = 1
ALPHA = (2 * DEPTH) ** 0.25
NORM_EPS = 1e-5
NEG_INF = -1e30

LOG2E = 1.4426950408889634
N_PIECES = 3
LANES = 128
VMEM_LIMIT = 56 * 1024 * 1024

C_QLAT = 0
C_CKV = C_QLAT + Q_LORA
C_KR = C_CKV + KV_LORA
C_KRR = C_KR + LANES
C_FQ = C_KRR + LANES
C_FK = C_FQ + H_FOX * HD_FOX
C_FV = C_FK + H_FOX * HD_FOX
C_MQ = C_FV + H_FOX * HD_FOX
C_FL = C_MQ + H_MEM * HD_MEM
C_END = C_FL + LANES


def _params(n_grid_axes):
    return pltpu.CompilerParams(
        dimension_semantics=("arbitrary",) * n_grid_axes, vmem_limit_bytes=VMEM_LIMIT)


def _full(shape):
    return pl.BlockSpec(shape, lambda *_: (0,) * len(shape))


def _dot(a, b):
    return jnp.dot(a, b, preferred_element_type=F32)


def _dot_nt(a, b):
    return lax.dot_general(a, b, (((1,), (1,)), ((), ())), preferred_element_type=F32)


def _sigmoid(x):
    return 1.0 / (1.0 + jnp.exp(-x))


def _rms(x, g):
    ms = jnp.mean(x * x, axis=-1, keepdims=True)
    return x * lax.rsqrt(ms + NORM_EPS) * g


def _layer_norm(x, g, b):
    mu = jnp.mean(x, axis=-1, keepdims=True)
    xc = x - mu
    var = jnp.mean(xc * xc, axis=-1, keepdims=True)
    return xc * lax.rsqrt(var + NORM_EPS) * g + b


def _proj_kernel(x_ref, w_ref, wq_ref, wabs_ref, qg_ref, kvg_ref, bf_ref, cos_ref, sin_ref,
                 qcat_ref, kcat_ref, lat_ref, latb_ref, kr_ref, fq_ref, fk_ref, fv_ref,
                 fkb_ref, fvb_ref, mq_ref, logf_ref, logfp_ref):
    xb = x_ref[...].astype(BF16)
    tm = xb.shape[0]

    def proj(c0, width):
        return _dot(xb, w_ref[:, c0:c0 + width])

    cos = cos_ref[...]
    sin = sin_ref[...]
    qn = _rms(proj(C_QLAT, Q_LORA), qg_ref[...]).astype(BF16)
    q2 = _dot(qn, wq_ref[...])
    nq = H_MLA * NOPE_DIM
    nr = H_MLA * ROPE_DIM
    q_rope = (q2[:, nq:nq + nr] * cos + q2[:, nq + nr:nq + 2 * nr] * sin) * (MLA_SCALE * LOG2E)
    lane = lax.broadcasted_iota(jnp.int32, (tm, LANES), 1)
    heads_per_group = LANES // ROPE_DIM
    for j in range(H_MLA // 2):
        q_abs = _dot(q2[:, j * LANES:(j + 1) * LANES].astype(BF16), wabs_ref[j]) * (MLA_SCALE * LOG2E)
        for half in range(2):
            h = 2 * j + half
            grp = h // heads_per_group
            qr = q_rope[:, grp * LANES:(grp + 1) * LANES]
            qr = jnp.where(lane // ROPE_DIM == h % heads_per_group, qr, 0.0)
            qcat_ref[h, :, 0:LANES] = q_abs[:, half * LANES:(half + 1) * LANES].astype(BF16)
            qcat_ref[h, :, LANES:2 * LANES] = qr.astype(BF16)
    lat = _rms(proj(C_CKV, KV_LORA), kvg_ref[...])
    lat_ref[...] = lat
    ones = jnp.ones((tm, LANES), BF16)
    latb_ref[:, 0:KV_LORA] = lat.astype(BF16)
    latb_ref[:, KV_LORA:KV_LORA + LANES] = ones
    kr = proj(C_KR, LANES) * cos[:, :LANES] + proj(C_KRR, LANES) * sin[:, :LANES]
    kr_ref[...] = kr[:, :ROPE_DIM]
    kcat_ref[:, 0:KV_LORA] = lat.astype(BF16)
    kcat_ref[:, KV_LORA:KV_LORA + LANES] = kr.astype(BF16)
    fq_ref[...] = (proj(C_FQ, H_FOX * HD_FOX) * (FOX_SCALE * LOG2E)).astype(BF16)
    fk = proj(C_FK, H_FOX * HD_FOX)
    fk_ref[...] = fk
    fkb_ref[...] = fk.astype(BF16)
    fv = proj(C_FV, H_FOX * HD_FOX)
    fv_ref[...] = fv
    for j in range(H_FOX * HD_FOX // LANES):
        fvb_ref[:, 2 * j * LANES:(2 * j + 1) * LANES] = fv[:, j * LANES:(j + 1) * LANES].astype(BF16)
        fvb_ref[:, (2 * j + 1) * LANES:(2 * j + 2) * LANES] = ones
    mq_ref[...] = proj(C_MQ, H_MEM * HD_MEM).astype(BF16)
    z = proj(C_FL, LANES) + bf_ref[...]
    log_f = jnp.minimum(z, 0.0) - jnp.log1p(jnp.exp(-jnp.abs(z)))
    logfp_ref[...] = log_f
    logf_ref[...] = log_f[:, :H_FOX]


def _proj(x, w_main, w_q, w_abs, q_g, kv_g, b_f, cos_t, sin_t, tm):
    t = x.shape[0]
    n_tab = cos_t.shape[0] // tm
    row = lambda w: pl.BlockSpec((tm, w), lambda i: (i, 0))
    tab = pl.BlockSpec((tm, 2 * LANES), lambda i: (i % n_tab, 0))
    hw = H_FOX * HD_FOX
    out_shape = (
        jax.ShapeDtypeStruct((H_MLA, t, 2 * LANES), BF16),
        jax.ShapeDtypeStruct((t, 2 * LANES), BF16),
        jax.ShapeDtypeStruct((t, KV_LORA), F32),
        jax.ShapeDtypeStruct((t, KV_LORA + LANES), BF16),
        jax.ShapeDtypeStruct((t, ROPE_DIM), F32),
        jax.ShapeDtypeStruct((t, hw), BF16),
        jax.ShapeDtypeStruct((t, hw), F32),
        jax.ShapeDtypeStruct((t, hw), F32),
        jax.ShapeDtypeStruct((t, hw), BF16),
        jax.ShapeDtypeStruct((t, 2 * hw), BF16),
        jax.ShapeDtypeStruct((t, H_MEM * HD_MEM), BF16),
        jax.ShapeDtypeStruct((t, H_FOX), F32),
        jax.ShapeDtypeStruct((t, LANES), F32),
    )
    out_specs = (
        pl.BlockSpec((H_MLA, tm, 2 * LANES), lambda i: (0, i, 0)),
        row(2 * LANES), row(KV_LORA), row(KV_LORA + LANES), row(ROPE_DIM), row(hw), row(hw), row(hw),
        row(hw), row(2 * hw), row(H_MEM * HD_MEM), row(H_FOX), row(LANES),
    )
    return pl.pallas_call(
        _proj_kernel,
        grid=(t // tm,),
        in_specs=[row(D_MODEL), _full(w_main.shape), _full(w_q.shape), _full(w_abs.shape),
                  _full(q_g.shape), _full(kv_g.shape), _full(b_f.shape), tab, tab],
        out_specs=out_specs,
        out_shape=out_shape,
        compiler_params=_params(1),
    )(x, w_main, w_q, w_abs, q_g, kv_g, b_f, cos_t, sin_t)


def _memkv_kernel(x_ref, wk_ref, wv_ref, k_ref, v_ref, kb_ref, vb_ref):
    xb = x_ref[...].astype(BF16)
    k = _dot(xb, wk_ref[...])
    v = _dot(xb, wv_ref[...])
    k_ref[...] = k
    v_ref[...] = v
    kb_ref[...] = k.astype(BF16)
    vb_ref[...] = v.astype(BF16)


def _memkv(x, wk, wv, tm):
    t = x.shape[0]
    w = H_MEM * HD_MEM
    row = lambda n: pl.BlockSpec((tm, n), lambda i: (i, 0))
    return pl.pallas_call(
        _memkv_kernel,
        grid=(t // tm,),
        in_specs=[row(D_MODEL), _full(wk.shape), _full(wv.shape)],
        out_specs=(row(w), row(w), row(w), row(w)),
        out_shape=(jax.ShapeDtypeStruct((t, w), F32), jax.ShapeDtypeStruct((t, w), F32),
                   jax.ShapeDtypeStruct((t, w), BF16), jax.ShapeDtypeStruct((t, w), BF16)),
        compiler_params=_params(1),
    )(x, wk, wv)


CUM_BLOCK = 256


def _bf16_pieces(x):
    hi = x.astype(BF16)
    r1 = x - hi.astype(F32)
    mid = r1.astype(BF16)
    lo = (r1 - mid.astype(F32)).astype(BF16)
    return hi, mid, lo


def _cumsum_kernel(x_ref, c_ref, aug_ref):
    s = x_ref.shape[1]
    r = lax.broadcasted_iota(jnp.int32, (CUM_BLOCK, CUM_BLOCK), 0)
    c = lax.broadcasted_iota(jnp.int32, (CUM_BLOCK, CUM_BLOCK), 1)
    tri = jnp.where(r >= c, 1.0, 0.0).astype(BF16)
    head = lax.broadcasted_iota(jnp.int32, (LANES, aug_ref.shape[-1]), 0)
    col = lax.broadcasted_iota(jnp.int32, (LANES, aug_ref.shape[-1]), 1)
    off = col % LANES
    src = 2 * (col // LANES) + off // N_PIECES
    place = [jnp.where(jnp.logical_and(jnp.logical_and(head == src, off < 2 * N_PIECES),
                                       off % N_PIECES == p), 1.0, 0.0).astype(BF16)
             for p in range(N_PIECES)]

    def body(i, carry):
        r0 = pl.multiple_of(i * CUM_BLOCK, CUM_BLOCK)
        x = x_ref[0, pl.ds(r0, CUM_BLOCK), :]
        cs = sum(_dot(tri, piece) for piece in _bf16_pieces(x)) + carry
        c2 = cs * LOG2E
        c_ref[0, pl.ds(r0, CUM_BLOCK), :] = c2
        aug = sum(_dot(piece, place[p]) for p, piece in enumerate(_bf16_pieces(-c2)))
        aug_ref[0, pl.ds(r0, CUM_BLOCK), :] = aug.astype(BF16)
        return cs[CUM_BLOCK - 1:CUM_BLOCK, :]

    lax.fori_loop(0, s // CUM_BLOCK, body, jnp.zeros((1, LANES), F32))


def _cumsum(x):
    b, s, _ = x.shape
    spec = pl.BlockSpec((1, s, LANES), lambda i: (i, 0, 0))
    w = H_FOX * HD_FOX
    return pl.pallas_call(
        _cumsum_kernel, grid=(b,), in_specs=[spec],
        out_specs=(spec, pl.BlockSpec((1, s, w), lambda i: (i, 0, 0))),
        out_shape=(jax.ShapeDtypeStruct(x.shape, F32), jax.ShapeDtypeStruct((b, s, w), BF16)),
        compiler_params=_params(1),
    )(x)


def _online_softmax_step(s, v_ones, m_ref, acc_ref, shift=None):
    def lanes(x, width):
        return jnp.concatenate([x] * (width // LANES), axis=-1)

    m = m_ref[...]
    r = jnp.max(s, axis=-1, keepdims=True)
    m_new = jnp.maximum(m, r if shift is None else r + shift)
    alpha = jnp.exp2(m - m_new)
    p = jnp.exp2(s - lanes(m_new if shift is None else m_new - shift, s.shape[-1]))
    m_ref[...] = m_new
    acc_ref[...] = lanes(alpha, acc_ref.shape[-1]) * acc_ref[...] + _dot(p.astype(BF16), v_ones)


def _init_softmax_state(m_ref, acc_ref):
    m_ref[...] = jnp.full(m_ref.shape, NEG_INF, F32)
    acc_ref[...] = jnp.zeros(acc_ref.shape, F32)


def _normalised(acc):
    return acc[:, :LANES] / acc[:, LANES:]


def _kv_tile_counts(first_visible_end, last_visible_end, tk, kv_len, sk):
    n_full = jnp.minimum(first_visible_end // tk, kv_len // tk)
    n_any = jnp.minimum((jnp.minimum(last_visible_end, kv_len) + tk - 1) // tk, sk // tk)
    return n_full, n_any


def _kv_loop(n_full, n_any, consume):
    def body(kt, carry, masked):
        consume(kt, masked)
        return carry

    lax.fori_loop(0, n_full, functools.partial(body, masked=False), 0)
    lax.fori_loop(n_full, n_any, functools.partial(body, masked=True), 0)


def _kv_loop_pipelined(n_any, produce, consume):
    produce(0, 0)
    n_pairs = (n_any - 1) // 2

    def body(i, carry):
        kt = 2 * i
        produce(1, kt + 1)
        consume(0, kt, False)
        produce(0, kt + 2)
        consume(1, kt + 1, False)
        return carry

    lax.fori_loop(0, n_pairs, body, 0)
    kt0 = 2 * n_pairs

    @pl.when(kt0 == n_any - 1)
    def _():
        consume(0, kt0, True)

    @pl.when(kt0 < n_any - 1)
    def _():
        produce(1, kt0 + 1)
        consume(0, kt0, True)
        consume(1, kt0 + 1, True)


def _mla_kernel(q_ref, k_ref, v_ref, o_ref, m_ref, acc_ref, s_ref, *, tq, tk, kv_len, q_pos0):
    sk = k_ref.shape[1]
    rows = H_MLA * tq
    q_start = q_pos0 + pl.program_id(1) * tq
    _, n_any = _kv_tile_counts((q_start // CHUNK + 1) * CHUNK,
                               ((q_start + tq - 1) // CHUNK + 1) * CHUNK, tk, kv_len, sk)
    q = q_ref[...].reshape(rows, q_ref.shape[-1])
    q_chunk = (q_start + lax.broadcasted_iota(jnp.int32, (rows, 1), 0) % tq) // CHUNK

    def produce(slot, kt):
        s_ref[slot] = _dot_nt(q, k_ref[0, pl.ds(pl.multiple_of(kt * tk, tk), tk), :])

    def consume(slot, kt, masked):
        k0 = pl.multiple_of(kt * tk, tk)
        s = s_ref[slot]
        if masked:
            k_pos = k0 + lax.broadcasted_iota(jnp.int32, (1, tk), 1)
            mask = k_pos // CHUNK <= q_chunk
            if kv_len < sk:
                mask = jnp.logical_and(mask, k_pos < kv_len)
            s = jnp.where(mask, s, NEG_INF)
        _online_softmax_step(s, v_ref[0, pl.ds(k0, tk), :], m_ref, acc_ref)

    _init_softmax_state(m_ref, acc_ref)
    _kv_loop_pipelined(n_any, produce, consume)
    o = _normalised(acc_ref[...]).astype(BF16)
    for h in range(H_MLA):
        o_ref[:, h * KV_LORA:(h + 1) * KV_LORA] = o[h * tq:(h + 1) * tq, :]


def _mla_attention(qcat, kcat, latb, batch, sq, tq, tk, kv_len, q_pos0):
    sk = kcat.shape[1]
    nq = sq // tq
    assert tk % tq == 0 and q_pos0 % tq == 0 and tq % CHUNK in (0, tq)
    kern = functools.partial(_mla_kernel, tq=tq, tk=tk, kv_len=kv_len, q_pos0=q_pos0)
    return pl.pallas_call(
        kern,
        grid=(batch, nq),
        in_specs=[pl.BlockSpec((H_MLA, tq, 2 * LANES), lambda b, i: (0, b * nq + i, 0)),
                  pl.BlockSpec((1, sk, 2 * LANES), lambda b, i: (b, 0, 0)),
                  pl.BlockSpec((1, sk, 2 * LANES), lambda b, i: (b, 0, 0))],
        out_specs=pl.BlockSpec((tq, H_MLA * KV_LORA), lambda b, i: (b * nq + i, 0)),
        out_shape=jax.ShapeDtypeStruct((batch * sq, H_MLA * KV_LORA), BF16),
        scratch_shapes=[pltpu.VMEM((H_MLA * tq, LANES), F32),
                        pltpu.VMEM((H_MLA * tq, 2 * LANES), F32),
                        pltpu.VMEM((2, H_MLA * tq, tk), F32)],
        compiler_params=_params(2),
    )(qcat, kcat, latb)


def _fox_kernel(q_ref, k_ref, v_ref, cq_ref, ck_ref, o_ref, m_ref, acc_ref, cqb_ref, *, tq, tk,
                kv_len, q_pos0):
    sk = k_ref.shape[1]
    q_start = q_pos0 + pl.program_id(1) * tq
    n_full, n_any = _kv_tile_counts(q_start + 1, q_start + tq, tk, kv_len, sk)
    lane = lax.broadcasted_iota(jnp.int32, (tq, LANES), 1)
    n_pairs = H_FOX * HD_FOX // LANES
    q_pos = q_start + lax.broadcasted_iota(jnp.int32, (2 * tq, 1), 0) % tq
    pick = jnp.concatenate([jnp.where(lane < N_PIECES, 1.0, 0.0),
                            jnp.where(lane // N_PIECES == 1, 1.0, 0.0)], axis=0).astype(BF16)
    q2 = []
    for j in range(n_pairs):
        q_pair = q_ref[:, j * LANES:(j + 1) * LANES]
        zero = jnp.zeros_like(q_pair)
        q2.append(jnp.concatenate(
            [jnp.concatenate([jnp.where(lane < HD_FOX, q_pair, zero),
                              jnp.where(lane < HD_FOX, zero, q_pair)], axis=0), pick], axis=-1))
        for half in range(2):
            col = jnp.sum(jnp.where(lane == 2 * j + half, cq_ref[0], 0.0), axis=-1, keepdims=True)
            cqb_ref[j, half * tq:(half + 1) * tq, :] = jnp.broadcast_to(col, (tq, LANES))

    def consume(kt, masked):
        k0 = pl.multiple_of(kt * tk, tk)
        if masked:
            k_pos = k0 + lax.broadcasted_iota(jnp.int32, (1, tk), 1)
            mask = k_pos <= q_pos
            if kv_len < sk:
                mask = jnp.logical_and(mask, k_pos < kv_len)
        for j in range(n_pairs):
            cols = slice(j * LANES, (j + 1) * LANES)
            k_dec = jnp.concatenate([k_ref[0, pl.ds(k0, tk), cols], ck_ref[0, pl.ds(k0, tk), cols]],
                                    axis=-1)
            s = _dot_nt(q2[j], k_dec)
            if masked:
                s = jnp.where(mask, s, NEG_INF)
            v_ones = v_ref[0, pl.ds(k0, tk), 2 * j * LANES:2 * (j + 1) * LANES]
            _online_softmax_step(s, v_ones, m_ref.at[j], acc_ref.at[j], shift=cqb_ref[j])

    _init_softmax_state(m_ref, acc_ref)
    _kv_loop(n_full, n_any, consume)
    for j in range(n_pairs):
        o = _normalised(acc_ref[j])
        o_ref[:, j * LANES:(j + 1) * LANES] = jnp.where(lane < HD_FOX, o[:tq], o[tq:]).astype(BF16)


def _fox_attention(fq, fkb, fvb, c_q, c_k, batch, sq, tq, tk, kv_len, q_pos0):
    sk = fkb.shape[1]
    nq = sq // tq
    w = H_FOX * HD_FOX
    kern = functools.partial(_fox_kernel, tq=tq, tk=tk, kv_len=kv_len, q_pos0=q_pos0)
    return pl.pallas_call(
        kern,
        grid=(batch, nq),
        in_specs=[pl.BlockSpec((tq, w), lambda b, i: (b * nq + i, 0)),
                  pl.BlockSpec((1, sk, w), lambda b, i: (b, 0, 0)),
                  pl.BlockSpec((1, sk, 2 * w), lambda b, i: (b, 0, 0)),
                  pl.BlockSpec((1, tq, LANES), lambda b, i: (b, i, 0)),
                  pl.BlockSpec((1, sk, w), lambda b, i: (b, 0, 0))],
        out_specs=pl.BlockSpec((tq, w), lambda b, i: (b * nq + i, 0)),
        out_shape=jax.ShapeDtypeStruct((batch * sq, w), BF16),
        scratch_shapes=[pltpu.VMEM((w // LANES, 2 * tq, LANES), F32),
                        pltpu.VMEM((w // LANES, 2 * tq, 2 * LANES), F32),
                        pltpu.VMEM((w // LANES, 2 * tq, LANES), F32)],
        compiler_params=_params(2),
    )(fq, fkb, fvb, c_q, c_k)


def _mem_kernel(q_ref, k_ref, v_ref, o_ref):
    for h in range(H_MEM):
        cols = slice(h * HD_MEM, (h + 1) * HD_MEM)
        s = _dot_nt(q_ref[:, cols], k_ref[0, :, cols]) * (MEM_SCALE * LOG2E)
        m = jnp.max(s, axis=-1, keepdims=True)
        p = jnp.exp2(s - m)
        l = jnp.sum(p, axis=-1, keepdims=True)
        o = _dot(p.astype(BF16), v_ref[0, :, cols]) / l
        o_ref[:, cols] = o.astype(BF16)


def _mem_attention(mq, mkb, mvb, batch, sq, tq):
    nq = sq // tq
    n_mem = mkb.shape[1]
    w = H_MEM * HD_MEM
    return pl.pallas_call(
        _mem_kernel,
        grid=(batch, nq),
        in_specs=[pl.BlockSpec((tq, w), lambda b, i: (b * nq + i, 0)),
                  pl.BlockSpec((1, n_mem, w), lambda b, i: (b, 0, 0)),
                  pl.BlockSpec((1, n_mem, w), lambda b, i: (b, 0, 0))],
        out_specs=pl.BlockSpec((tq, w), lambda b, i: (b * nq + i, 0)),
        out_shape=jax.ShapeDtypeStruct((batch * sq, w), BF16),
        compiler_params=_params(2),
    )(mq, mkb, mvb)


def _route(x1b, wr_t, bias):
    tm = x1b.shape[0]
    scores = _sigmoid(_dot_nt(wr_t, x1b))
    sel = scores + bias
    sub = lax.broadcasted_iota(jnp.int32, (GROUP_SIZE, tm), 0)
    gs = []
    for g in range(N_GROUPS):
        blk = sel[g * GROUP_SIZE:(g + 1) * GROUP_SIZE, :]
        m1 = jnp.max(blk, axis=0, keepdims=True)
        first = jnp.min(jnp.where(blk == m1, sub, GROUP_SIZE), axis=0, keepdims=True)
        m2 = jnp.max(jnp.where(sub == first, -jnp.inf, blk), axis=0, keepdims=True)
        gs.append(m1 + m2)
    e_idx = lax.broadcasted_iota(jnp.int32, (N_EXPERTS, tm), 0)
    allowed = jnp.zeros((N_EXPERTS, tm), jnp.bool_)
    for g in range(N_GROUPS):
        rank = jnp.zeros((1, tm), jnp.int32)
        for o in range(N_GROUPS):
            if o == g:
                continue
            beats = (gs[o] >= gs[g]) if o < g else (gs[o] > gs[g])
            rank = rank + beats.astype(jnp.int32)
        keep = rank < TOPK_GROUPS
        allowed = jnp.logical_or(allowed, jnp.logical_and(e_idx // GROUP_SIZE == g, keep))
    cand = jnp.where(allowed, sel, -jnp.inf)
    chosen = jnp.zeros((N_EXPERTS, tm), jnp.bool_)
    for _ in range(TOP_K):
        m = jnp.max(cand, axis=0, keepdims=True)
        first = jnp.min(jnp.where(cand == m, e_idx, N_EXPERTS), axis=0, keepdims=True)
        pick = e_idx == first
        chosen = jnp.logical_or(chosen, pick)
        cand = jnp.where(pick, -jnp.inf, cand)
    w = jnp.where(chosen, scores, 0.0)
    return w / jnp.sum(w, axis=0, keepdims=True) * ROUTED_SCALE, chosen


def _swiglu(xb, wg, wu):
    a = _dot(xb, wg)
    return (a * _sigmoid(a)) * _dot(xb, wu)


def _merge_kernel(x_ref, olat_ref, ofox_ref, omem_ref, wg_ref, wuv_ref, wbm_ref, wbf_ref, wbc_ref,
                  wo_ref, g1_ref, b1_ref, wsg_ref, wsu_ref, wsd_ref, x1b_ref, base_ref):
    x = x_ref[...]
    xb = x.astype(BF16)
    o_mla = jnp.concatenate(
        [_dot(olat_ref[:, j * 2 * KV_LORA:(j + 1) * 2 * KV_LORA], wuv_ref[j]).astype(BF16)
         for j in range(H_MLA // 2)], axis=-1)

    def gate(n):
        return _sigmoid(_dot(xb, wg_ref[:, n * D_MODEL:(n + 1) * D_MODEL]))

    y = gate(0) * _dot(o_mla, wbm_ref[...])
    y = y + gate(1) * _dot(ofox_ref[...], wbf_ref[...])
    y = y + gate(2) * _dot(omem_ref[...], wbc_ref[...])
    mix = _dot(y.astype(BF16), wo_ref[...])
    x1 = _layer_norm(ALPHA * x + mix, g1_ref[...], b1_ref[...])
    x1b = x1.astype(BF16)
    x1b_ref[...] = x1b
    shared = _dot(_swiglu(x1b, wsg_ref[...], wsu_ref[...]).astype(BF16), wsd_ref[...])
    base_ref[...] = ALPHA * x1 + shared


def _merge(x, olat, ofox, omem, wg, wuv, wbm, wbf, wbc, wo, g1, b1, wsg, wsu, wsd, tm):
    t = x.shape[0]
    row = lambda n: pl.BlockSpec((tm, n), lambda i: (i, 0))
    ws = (wg, wuv, wbm, wbf, wbc, wo, g1, b1, wsg, wsu, wsd)
    return pl.pallas_call(
        _merge_kernel,
        grid=(t // tm,),
        in_specs=[row(D_MODEL), row(H_MLA * KV_LORA), row(H_FOX * HD_FOX), row(H_MEM * HD_MEM)]
        + [_full(w.shape) for w in ws],
        out_specs=(row(D_MODEL), row(D_MODEL)),
        out_shape=(jax.ShapeDtypeStruct((t, D_MODEL), BF16), jax.ShapeDtypeStruct((t, D_MODEL), F32)),
        compiler_params=_params(1),
    )(x, olat, ofox, omem, *ws)


BF16_ROWS = 16
PLACE_ROWS = 512
EXPERT_ROWS = 2048
EXPERT_CHUNK = 512
EXPERTS_PER_STEP = 4


def _capacity(tm):
    return min(tm, _round_up(2 * tm * TOP_K // N_EXPERTS, BF16_ROWS))


def _dispatch_kernel(x1b_ref, wr_ref, rb_ref, xs_ref, gate_ref, slot_ref, govf_ref, flag_ref,
                     pm_ref, *, cap):
    xb = x1b_ref[...]
    tm = xb.shape[0]
    gate, chosen = _route(xb, wr_ref[...], rb_ref[...])
    r = lax.broadcasted_iota(jnp.int32, (tm, tm), 0)
    c = lax.broadcasted_iota(jnp.int32, (tm, tm), 1)
    earlier = jnp.where(r < c, 1.0, 0.0).astype(BF16)
    rank = _dot(jnp.where(chosen, 1.0, 0.0).astype(BF16), earlier)
    fits = jnp.logical_and(chosen, rank < cap)
    slot = jnp.where(fits, rank, -1.0)
    gate_ref[...] = jnp.where(fits, gate, 0.0)
    slot_ref[...] = slot
    g_ovf = jnp.where(fits, 0.0, gate)
    govf_ref[...] = jnp.concatenate([g_ovf, jnp.zeros((LANES - N_EXPERTS, tm), F32)], axis=0).T
    any_ovf = jnp.max(jnp.max(g_ovf, axis=0, keepdims=True), axis=1, keepdims=True)
    flag_ref[0] = jnp.broadcast_to(jnp.where(any_ovf > 0.0, 1, 0), flag_ref.shape[1:]).astype(jnp.int32)
    s_iota = lax.broadcasted_iota(jnp.int32, (cap, tm), 0).astype(F32)
    for e in range(N_EXPERTS):
        pm_ref[e * cap:(e + 1) * cap, :] = jnp.where(s_iota == slot[e:e + 1, :], 1.0, 0.0).astype(BF16)
    per = PLACE_ROWS // cap
    for g in range(N_EXPERTS // per):
        xs = _dot(pm_ref[g * PLACE_ROWS:(g + 1) * PLACE_ROWS, :], xb).astype(BF16)
        xs_ref[g * per:(g + 1) * per] = xs.reshape(per, cap, D_MODEL)


def _dispatch(x1b, wr_t, rb, tm, cap):
    t = x1b.shape[0]
    nt = t // tm
    col = pl.BlockSpec((N_EXPERTS, tm), lambda i: (0, i))
    return pl.pallas_call(
        functools.partial(_dispatch_kernel, cap=cap),
        grid=(nt,),
        in_specs=[pl.BlockSpec((tm, D_MODEL), lambda i: (i, 0)), _full(wr_t.shape), _full(rb.shape)],
        out_specs=(pl.BlockSpec((N_EXPERTS, cap, D_MODEL), lambda i: (0, i, 0)), col, col,
                   pl.BlockSpec((tm, LANES), lambda i: (i, 0)),
                   pl.BlockSpec((1, 8, LANES), lambda i: (i, 0, 0))),
        out_shape=(jax.ShapeDtypeStruct((N_EXPERTS, nt * cap, D_MODEL), BF16),
                   jax.ShapeDtypeStruct((N_EXPERTS, t), F32),
                   jax.ShapeDtypeStruct((N_EXPERTS, t), F32),
                   jax.ShapeDtypeStruct((t, LANES), F32),
                   jax.ShapeDtypeStruct((nt, 8, LANES), jnp.int32)),
        scratch_shapes=[pltpu.VMEM((N_EXPERTS * cap, tm), BF16)],
        compiler_params=_params(1),
    )(x1b, wr_t, rb)


def _expert_kernel(xs_ref, wg_ref, wu_ref, wd_ref, ys_ref, wgb_ref, wub_ref, wdb_ref, *, chunk):
    @pl.when(pl.program_id(1) == 0)
    def _():
        wgb_ref[...] = wg_ref[0].astype(BF16)
        wub_ref[...] = wu_ref[0].astype(BF16)
        wdb_ref[...] = wd_ref[0].astype(BF16)

    def body(c, carry):
        r0 = pl.multiple_of(c * chunk, chunk)
        hid = _swiglu(xs_ref[0, pl.ds(r0, chunk), :], wgb_ref[...], wub_ref[...])
        ys_ref[0, pl.ds(r0, chunk), :] = _dot(hid.astype(BF16), wdb_ref[...]).astype(BF16)
        return carry

    lax.fori_loop(0, xs_ref.shape[1] // chunk, body, 0)


def _experts(xs, weg, weu, wed):
    n_e, rows, _ = xs.shape
    rb = min(EXPERT_ROWS, rows)
    blk = pl.BlockSpec((1, rb, D_MODEL), lambda e, i: (e, i, 0))
    return pl.pallas_call(
        functools.partial(_expert_kernel, chunk=min(EXPERT_CHUNK, rb)),
        grid=(n_e, rows // rb),
        in_specs=[blk,
                  pl.BlockSpec((1, D_MODEL, D_EXPERT), lambda e, i: (e, 0, 0)),
                  pl.BlockSpec((1, D_MODEL, D_EXPERT), lambda e, i: (e, 0, 0)),
                  pl.BlockSpec((1, D_EXPERT, D_MODEL), lambda e, i: (e, 0, 0))],
        out_specs=blk,
        out_shape=jax.ShapeDtypeStruct(xs.shape, BF16),
        scratch_shapes=[pltpu.VMEM((D_MODEL, D_EXPERT), BF16), pltpu.VMEM((D_MODEL, D_EXPERT), BF16),
                        pltpu.VMEM((D_EXPERT, D_MODEL), BF16)],
        compiler_params=_params(2),
    )(xs, weg, weu, wed)


def _dense_routed_kernel(flag_ref, x1b_ref, gate_ref, weg_ref, weu_ref, wed_ref, out_ref):
    step = pl.program_id(1)

    @pl.when(step == 0)
    def _():
        out_ref[...] = jnp.zeros_like(out_ref)

    @pl.when(flag_ref[pl.program_id(0)] > 0)
    def _():
        xb = x1b_ref[...]
        gate = gate_ref[...]
        lane = lax.broadcasted_iota(jnp.int32, gate.shape, 1)
        for j in range(EXPERTS_PER_STEP):
            e = step * EXPERTS_PER_STEP + j
            g_col = jnp.sum(jnp.where(lane == e, gate, 0.0), axis=-1, keepdims=True)
            hid = _swiglu(xb, weg_ref[j], weu_ref[j])
            out_ref[...] += g_col * _dot(hid.astype(BF16), wed_ref[j])


def _dense_routed(flags, x1b, gate, weg, weu, wed, tm):
    t = x1b.shape[0]
    eb = EXPERTS_PER_STEP
    row = lambda n: pl.BlockSpec((tm, n), lambda i, s, f: (i, 0))
    wspec = lambda a, b: pl.BlockSpec((eb, a, b), lambda i, s, f: (jnp.where(f[i] > 0, s, 0), 0, 0))
    return pl.pallas_call(
        _dense_routed_kernel,
        grid_spec=pltpu.PrefetchScalarGridSpec(
            num_scalar_prefetch=1,
            grid=(t // tm, N_EXPERTS // eb),
            in_specs=[row(D_MODEL), row(LANES), wspec(D_MODEL, D_EXPERT), wspec(D_MODEL, D_EXPERT),
                      wspec(D_EXPERT, D_MODEL)],
            out_specs=row(D_MODEL)),
        out_shape=jax.ShapeDtypeStruct((t, D_MODEL), F32),
        compiler_params=_params(2),
    )(flags, x1b, gate, weg, weu, wed)


def _combine_kernel(gate_ref, slot_ref, ys_ref, base_ref, *rest, cap, has_overflow):
    if has_overflow:
        rovf_ref, g2_ref, b2_ref, y_ref, pw_ref = rest
    else:
        g2_ref, b2_ref, y_ref, pw_ref = rest
    tm = base_ref.shape[0]
    gate = gate_ref[...]
    slot = slot_ref[...]
    s_iota = lax.broadcasted_iota(jnp.int32, (cap, tm), 0).astype(F32)
    for e in range(N_EXPERTS):
        pw_ref[e * cap:(e + 1) * cap, :] = jnp.where(
            s_iota == slot[e:e + 1, :], gate[e:e + 1, :], 0.0).astype(BF16)
    routed = rovf_ref[...] if has_overflow else jnp.zeros((tm, D_MODEL), F32)
    per = PLACE_ROWS // cap
    for g in range(N_EXPERTS // per):
        ys = ys_ref[g * per:(g + 1) * per].reshape(PLACE_ROWS, D_MODEL)
        routed = routed + lax.dot_general(pw_ref[g * PLACE_ROWS:(g + 1) * PLACE_ROWS, :], ys,
                                          (((0,), (0,)), ((), ())), preferred_element_type=F32)
    y_ref[...] = _layer_norm(base_ref[...] + routed, g2_ref[...], b2_ref[...])


def _combine(gate_t, slot_t, ys, base, rovf, g2, b2, tm, cap):
    t = base.shape[0]
    col = pl.BlockSpec((N_EXPERTS, tm), lambda i: (0, i))
    row = pl.BlockSpec((tm, D_MODEL), lambda i: (i, 0))
    has_overflow = rovf is not None
    operands = (gate_t, slot_t, ys, base) + ((rovf,) if has_overflow else ()) + (g2, b2)
    return pl.pallas_call(
        functools.partial(_combine_kernel, cap=cap, has_overflow=has_overflow),
        grid=(t // tm,),
        in_specs=[col, col, pl.BlockSpec((N_EXPERTS, cap, D_MODEL), lambda i: (0, i, 0)), row]
        + ([row] if has_overflow else []) + [_full(g2.shape), _full(b2.shape)],
        out_specs=row,
        out_shape=jax.ShapeDtypeStruct((t, D_MODEL), F32),
        scratch_shapes=[pltpu.VMEM((N_EXPERTS * cap, tm), BF16)],
        compiler_params=_params(1),
    )(*operands)


def _moe_ln2(x1b, base, wr_t, rb, weg, weu, wed, g2, b2, tm):
    t = x1b.shape[0]
    cap = _capacity(tm)
    assert PLACE_ROWS % cap == 0 and (N_EXPERTS * cap) % PLACE_ROWS == 0
    xs, gate_t, slot_t, gate_ovf, flags = _dispatch(x1b, wr_t, rb, tm, cap)
    ys = _experts(xs, weg, weu, wed)
    flags = flags[:, 0, 0]

    def with_overflow():
        rovf = _dense_routed(flags, x1b, gate_ovf, weg.astype(BF16), weu.astype(BF16),
                             wed.astype(BF16), tm)
        return _combine(gate_t, slot_t, ys, base, rovf, g2, b2, tm, cap)

    return lax.cond(jnp.any(flags > 0), with_overflow,
                    lambda: _combine(gate_t, slot_t, ys, base, None, g2, b2, tm, cap))


def _rot_half_cols(w):
    half = w.shape[-1] // 2
    return jnp.concatenate([-w[..., half:], w[..., :half]], axis=-1)


def _block_diag2(a, b):
    za = jnp.zeros((a.shape[0], b.shape[1]), a.dtype)
    zb = jnp.zeros((b.shape[0], a.shape[1]), a.dtype)
    return jnp.concatenate([jnp.concatenate([a, za], 1), jnp.concatenate([zb, b], 1)], 0)


def _prep_weights(w_in, b_f, q_norm_g, w_q_up, kv_norm_g, w_kv_up):
    widths = (Q_LORA, KV_LORA, ROPE_DIM, H_FOX * HD_FOX, H_FOX * HD_FOX, H_FOX * HD_FOX, H_FOX,
              H_MEM * HD_MEM, N_BRANCH * D_MODEL)
    offs = np.cumsum((0,) + widths)
    w_ql, w_ckv, w_kr, w_fq, w_fk, w_fv, w_fl, w_mq, w_g = (
        w_in[:, offs[i]:offs[i + 1]] for i in range(len(widths)))
    rep = LANES // ROPE_DIM
    w_main = jnp.concatenate(
        [w_ql, w_ckv, jnp.tile(w_kr, (1, rep)), jnp.tile(_rot_half_cols(w_kr), (1, rep)),
         w_fq, w_fk, w_fv, w_mq, jnp.pad(w_fl, ((0, 0), (0, LANES - H_FOX)))], axis=1).astype(BF16)
    assert w_main.shape[1] == C_END
    wq3 = w_q_up.reshape(Q_LORA, H_MLA, NOPE_DIM + ROPE_DIM)
    wq_nope = wq3[:, :, :NOPE_DIM].reshape(Q_LORA, H_MLA * NOPE_DIM)
    wq_rope = wq3[:, :, NOPE_DIM:]
    w_q = jnp.concatenate(
        [wq_nope, wq_rope.reshape(Q_LORA, -1), _rot_half_cols(wq_rope).reshape(Q_LORA, -1)],
        axis=1).astype(BF16)
    wkv3 = w_kv_up.reshape(KV_LORA, H_MLA, NOPE_DIM + V_DIM)
    w_uk = wkv3[:, :, :NOPE_DIM]
    w_uv = wkv3[:, :, NOPE_DIM:]
    w_abs = jnp.stack([_block_diag2(w_uk[:, 2 * j].T, w_uk[:, 2 * j + 1].T)
                       for j in range(H_MLA // 2)]).astype(BF16)
    w_uvp = jnp.stack([_block_diag2(w_uv[:, 2 * j], w_uv[:, 2 * j + 1])
                       for j in range(H_MLA // 2)]).astype(BF16)
    b_fp = jnp.pad(b_f, (0, LANES - H_FOX)).reshape(1, LANES)
    return (w_main, w_q, w_abs, q_norm_g.reshape(1, -1), kv_norm_g.reshape(1, -1), b_fp,
            w_g.astype(BF16), w_uvp)


def _rope_tables(pos):
    half = ROPE_DIM // 2
    inv_freq = ROPE_THETA ** (-jnp.arange(half, dtype=F32) / half)
    ang = pos.astype(F32)[:, None] * inv_freq[None, :]
    rep = 2 * LANES // ROPE_DIM
    cos = jnp.tile(jnp.concatenate([jnp.cos(ang)] * 2, axis=-1), (1, rep))
    sin = jnp.tile(jnp.concatenate([jnp.sin(ang)] * 2, axis=-1), (1, rep))
    return cos, sin


def _round_up(n, m):
    return (n + m - 1) // m * m


def kernel(x_prompt, x_sample, mem_prompt, cache_mla_latent, cache_mla_krope, cache_fox_k,
           cache_fox_v, cache_fox_logf, cache_mem_k, cache_mem_v, ln1_g, ln1_b, w_in, b_f,
           q_norm_g, w_q_up, kv_norm_g, w_kv_up, w_mem_k, w_mem_v, w_br_mla, w_br_fox,
           w_br_mem, w_out, ln2_g, ln2_b, w_router, router_bias, w_e_gate, w_e_up, w_e_down,
           w_s_gate, w_s_up, w_s_down):
    assert w_in.shape[0] == DEPTH == 1
    batch, seq, _ = x_prompt.shape
    dec_batch, dec_seq, _ = x_sample.shape
    past = cache_mla_latent.shape[2]
    n_mem = mem_prompt.shape[1]
    l = 0

    (w_main, w_q, w_abs, q_g, kv_g, b_fp, w_gate, w_uvp) = _prep_weights(
        w_in[l], b_f[l], q_norm_g[l], w_q_up[l], kv_norm_g[l], w_kv_up[l])
    merge_w = (w_gate, w_uvp, w_br_mla[l].astype(BF16), w_br_fox[l].astype(BF16),
               w_br_mem[l].astype(BF16), w_out[l].astype(BF16), ln1_g[l].reshape(1, -1),
               ln1_b[l].reshape(1, -1), w_s_gate[l].astype(BF16), w_s_up[l].astype(BF16),
               w_s_down[l].astype(BF16))
    moe_w = (w_router[l].T.astype(BF16), router_bias[l].reshape(-1, 1),
             w_e_gate[l], w_e_up[l], w_e_down[l],
             ln2_g[l].reshape(1, -1), ln2_b[l].reshape(1, -1))

    def tail(x, olat, ofox, omem, tm, tm_moe):
        x1b, base = _merge(x, olat, ofox, omem, *merge_w, tm)
        return _moe_ln2(x1b, base, *moe_w, tm_moe)

    tp = batch * seq
    xp = x_prompt.reshape(tp, D_MODEL)
    tm_p = min(512, seq)
    tq_p = min(256, seq)
    tk_p = min(512, seq)
    cos_p, sin_p = _rope_tables(jnp.arange(seq, dtype=jnp.int32))
    (qcat, kcat, lat, latb, kr, fq, fk, fv, fkb, fvb, mq, logf, logfp) = _proj(
        xp, w_main, w_q, w_abs, q_g, kv_g, b_fp, cos_p, sin_p, tm_p)
    tmem = batch * n_mem
    m_k, m_v, m_kb, m_vb = _memkv(mem_prompt.reshape(tmem, D_MODEL), w_mem_k[l].astype(BF16),
                                  w_mem_v[l].astype(BF16), min(512, tmem))
    c_p, c_keys = _cumsum(logfp.reshape(batch, seq, LANES))
    olat = _mla_attention(qcat, kcat.reshape(batch, seq, -1), latb.reshape(batch, seq, -1),
                          batch, seq, tq_p, tk_p, seq, 0)
    ofox = _fox_attention(fq, fkb.reshape(batch, seq, -1), fvb.reshape(batch, seq, -1), c_p, c_keys,
                          batch, seq, tq_p, tk_p, seq, 0)
    omem = _mem_attention(mq, m_kb.reshape(batch, n_mem, -1), m_vb.reshape(batch, n_mem, -1),
                          batch, seq, tq_p)
    y_p = tail(xp, olat, ofox, omem, tm_p, min(256, seq)).reshape(batch, seq, D_MODEL)

    ts = dec_batch * dec_seq
    xs = x_sample.reshape(ts, D_MODEL)
    cos_s, sin_s = _rope_tables(jnp.tile(past + jnp.arange(dec_seq, dtype=jnp.int32), dec_batch))
    (qcat_s, kcat_s, lat_s, latb_s, kr_s, fq_s, fk_s, fv_s, fkb_s, fvb_s, mq_s, logf_s,
     logfp_s) = _proj(xs, w_main, w_q, w_abs, q_g, kv_g, b_fp, cos_s, sin_s, ts)
    kv_len = past + dec_seq
    tk_s = 256
    sk = _round_up(kv_len, tk_s)

    def with_cache(cache, new):
        new = new.reshape(dec_batch, dec_seq, -1)
        pad = jnp.zeros((dec_batch, sk - kv_len, new.shape[-1]), new.dtype)
        return jnp.concatenate([cache.astype(new.dtype), new, pad], axis=1)

    rep = LANES // ROPE_DIM
    kcat_all = with_cache(
        jnp.concatenate([cache_mla_latent[l], jnp.tile(cache_mla_krope[l], (1, 1, rep))], axis=-1),
        kcat_s)
    lat_all = with_cache(
        jnp.concatenate([cache_mla_latent[l], jnp.ones((dec_batch, past, LANES), F32)], axis=-1), latb_s)
    fk_all = with_cache(cache_fox_k[l].reshape(dec_batch, past, -1), fkb_s)
    n_pairs = H_FOX * HD_FOX // LANES
    fv_pairs = cache_fox_v[l].reshape(dec_batch, past, n_pairs, LANES)
    fv_all = with_cache(
        jnp.concatenate([fv_pairs, jnp.ones_like(fv_pairs)], axis=-1).reshape(dec_batch, past, -1),
        fvb_s)
    logf_all = with_cache(jnp.pad(cache_fox_logf[l], ((0, 0), (0, 0), (0, LANES - H_FOX))), logfp_s)
    c_s, c_keys_s = _cumsum(logf_all)
    c_q_s = c_s[:, past:kv_len]
    olat_s = _mla_attention(qcat_s, kcat_all, lat_all, dec_batch, dec_seq, dec_seq, tk_s, kv_len, past)
    ofox_s = _fox_attention(fq_s, fk_all, fv_all, c_q_s, c_keys_s, dec_batch, dec_seq, dec_seq, tk_s,
                            kv_len, past)
    omem_s = _mem_attention(mq_s, cache_mem_k[l].reshape(dec_batch, n_mem, -1).astype(BF16),
                            cache_mem_v[l].reshape(dec_batch, n_mem, -1).astype(BF16),
                            dec_batch, dec_seq, dec_seq)
    y_s = tail(xs, olat_s, ofox_s, omem_s, ts, ts).reshape(dec_batch, dec_seq, D_MODEL)

    def stack(a, b, s, *tail_shape):
        return a.reshape(1, b, s, *tail_shape)

    return (y_p, y_s,
            stack(lat, batch, seq, KV_LORA), stack(kr, batch, seq, ROPE_DIM),
            stack(fk, batch, seq, H_FOX, HD_FOX), stack(fv, batch, seq, H_FOX, HD_FOX),
            stack(logf, batch, seq, H_FOX),
            stack(m_k, batch, n_mem, H_MEM, HD_MEM), stack(m_v, batch, n_mem, H_MEM, HD_MEM),
            stack(lat_s, dec_batch, dec_seq, KV_LORA), stack(kr_s, dec_batch, dec_seq, ROPE_DIM),
            stack(fk_s, dec_batch, dec_seq, H_FOX, HD_FOX), stack(fv_s, dec_batch, dec_seq, H_FOX, HD_FOX),
            stack(logf_s, dec_batch, dec_seq, H_FOX))
```

```python
import functools

import jax
import jax.numpy as jnp
import numpy as np
from jax import lax
from jax.experimental import pallas as pl
from jax.experimental.pallas import tpu as pltpu

F32 = jnp.float32
BF16 = jnp.bfloat16

D_MODEL = 1024
CHUNK = 64
H_MLA = 8
Q_LORA = 256
KV_LORA = 128
NOPE_DIM = 64
ROPE_DIM = 32
V_DIM = 64
ROPE_THETA = 10000.0
MLA_SCALE = (NOPE_DIM + ROPE_DIM) ** -0.5
H_FOX = 8
HD_FOX = 64
FOX_SCALE = HD_FOX ** -0.5
H_MEM = 4
HD_MEM = 128
MEM_SCALE = HD_MEM ** -0.5
N_BRANCH = 3
N_EXPERTS = 64
TOP_K = 8
N_GROUPS = 8
GROUP_SIZE = N_EXPERTS // N_GROUPS
TOPK_GROUPS = 4
D_EXPERT = 256
ROUTED_SCALE = 2.5
DEPTH = 1
ALPHA = (2 * DEPTH) ** 0.25
NORM_EPS = 1e-5
NEG_INF = -1e30

LOG2E = 1.4426950408889634
LANES = 128
VMEM_LIMIT = 56 * 1024 * 1024

C_QLAT = 0
C_CKV = C_QLAT + Q_LORA
C_KR = C_CKV + KV_LORA
C_KRR = C_KR + LANES
C_FQ = C_KRR + LANES
C_FK = C_FQ + H_FOX * HD_FOX
C_FV = C_FK + H_FOX * HD_FOX
C_MQ = C_FV + H_FOX * HD_FOX
C_FL = C_MQ + H_MEM * HD_MEM
C_END = C_FL + LANES


def _params(n_grid_axes):
    return pltpu.CompilerParams(
        dimension_semantics=("arbitrary",) * n_grid_axes, vmem_limit_bytes=VMEM_LIMIT)


def _full(shape):
    return pl.BlockSpec(shape, lambda *_: (0,) * len(shape))


def _dot(a, b):
    return jnp.dot(a, b, preferred_element_type=F32)


def _dot_nt(a, b):
    return lax.dot_general(a, b, (((1,), (1,)), ((), ())), preferred_element_type=F32)


def _sigmoid(x):
    return 1.0 / (1.0 + jnp.exp(-x))


def _rms(x, g):
    ms = jnp.mean(x * x, axis=-1, keepdims=True)
    return x * lax.rsqrt(ms + NORM_EPS) * g


def _layer_norm(x, g, b):
    mu = jnp.mean(x, axis=-1, keepdims=True)
    xc = x - mu
    var = jnp.mean(xc * xc, axis=-1, keepdims=True)
    return xc * lax.rsqrt(var + NORM_EPS) * g + b


def _proj_kernel(x_ref, w_ref, wq_ref, wabs_ref, qg_ref, kvg_ref, bf_ref, cos_ref, sin_ref,
                 qcat_ref, kcat_ref, lat_ref, latb_ref, kr_ref, fq_ref, fk_ref, fv_ref,
                 fkb_ref, fvb_ref, mq_ref, logf_ref, logfp_ref):
    xb = x_ref[...].astype(BF16)
    tm = xb.shape[0]

    def proj(c0, width):
        return _dot(xb, w_ref[:, c0:c0 + width])

    cos = cos_ref[...]
    sin = sin_ref[...]
    qn = _rms(proj(C_QLAT, Q_LORA), qg_ref[...]).astype(BF16)
    q2 = _dot(qn, wq_ref[...])
    nq = H_MLA * NOPE_DIM
    nr = H_MLA * ROPE_DIM
    q_rope = (q2[:, nq:nq + nr] * cos + q2[:, nq + nr:nq + 2 * nr] * sin) * (MLA_SCALE * LOG2E)
    lane = lax.broadcasted_iota(jnp.int32, (tm, LANES), 1)
    heads_per_group = LANES // ROPE_DIM
    for j in range(H_MLA // 2):
        q_abs = _dot(q2[:, j * LANES:(j + 1) * LANES].astype(BF16), wabs_ref[j]) * (MLA_SCALE * LOG2E)
        for half in range(2):
            h = 2 * j + half
            grp = h // heads_per_group
            qr = q_rope[:, grp * LANES:(grp + 1) * LANES]
            qr = jnp.where(lane // ROPE_DIM == h % heads_per_group, qr, 0.0)
            qcat_ref[h, :, 0:LANES] = q_abs[:, half * LANES:(half + 1) * LANES].astype(BF16)
            qcat_ref[h, :, LANES:2 * LANES] = qr.astype(BF16)
    lat = _rms(proj(C_CKV, KV_LORA), kvg_ref[...])
    lat_ref[...] = lat
    ones = jnp.ones((tm, LANES), BF16)
    latb_ref[:, 0:KV_LORA] = lat.astype(BF16)
    latb_ref[:, KV_LORA:KV_LORA + LANES] = ones
    kr = proj(C_KR, LANES) * cos[:, :LANES] + proj(C_KRR, LANES) * sin[:, :LANES]
    kr_ref[...] = kr[:, :ROPE_DIM]
    kcat_ref[:, 0:KV_LORA] = lat.astype(BF16)
    kcat_ref[:, KV_LORA:KV_LORA + LANES] = kr.astype(BF16)
    fq_ref[...] = (proj(C_FQ, H_FOX * HD_FOX) * (FOX_SCALE * LOG2E)).astype(BF16)
    fk = proj(C_FK, H_FOX * HD_FOX)
    fk_ref[...] = fk
    fkb_ref[...] = fk.astype(BF16)
    fv = proj(C_FV, H_FOX * HD_FOX)
    fv_ref[...] = fv
    for j in range(H_FOX * HD_FOX // LANES):
        fvb_ref[:, 2 * j * LANES:(2 * j + 1) * LANES] = fv[:, j * LANES:(j + 1) * LANES].astype(BF16)
        fvb_ref[:, (2 * j + 1) * LANES:(2 * j + 2) * LANES] = ones
    mq_ref[...] = proj(C_MQ, H_MEM * HD_MEM).astype(BF16)
    z = proj(C_FL, LANES) + bf_ref[...]
    log_f = jnp.minimum(z, 0.0) - jnp.log1p(jnp.exp(-jnp.abs(z)))
    logfp_ref[...] = log_f
    logf_ref[...] = log_f[:, :H_FOX]


def _proj(x, w_main, w_q, w_abs, q_g, kv_g, b_f, cos_t, sin_t, tm):
    t = x.shape[0]
    n_tab = cos_t.shape[0] // tm
    row = lambda w: pl.BlockSpec((tm, w), lambda i: (i, 0))
    tab = pl.BlockSpec((tm, 2 * LANES), lambda i: (i % n_tab, 0))
    hw = H_FOX * HD_FOX
    out_shape = (
        jax.ShapeDtypeStruct((H_MLA, t, 2 * LANES), BF16),
        jax.ShapeDtypeStruct((t, 2 * LANES), BF16),
        jax.ShapeDtypeStruct((t, KV_LORA), F32),
        jax.ShapeDtypeStruct((t, KV_LORA + LANES), BF16),
        jax.ShapeDtypeStruct((t, ROPE_DIM), F32),
        jax.ShapeDtypeStruct((t, hw), BF16),
        jax.ShapeDtypeStruct((t, hw), F32),
        jax.ShapeDtypeStruct((t, hw), F32),
        jax.ShapeDtypeStruct((t, hw), BF16),
        jax.ShapeDtypeStruct((t, 2 * hw), BF16),
        jax.ShapeDtypeStruct((t, H_MEM * HD_MEM), BF16),
        jax.ShapeDtypeStruct((t, H_FOX), F32),
        jax.ShapeDtypeStruct((t, LANES), F32),
    )
    out_specs = (
        pl.BlockSpec((H_MLA, tm, 2 * LANES), lambda i: (0, i, 0)),
        row(2 * LANES), row(KV_LORA), row(KV_LORA + LANES), row(ROPE_DIM), row(hw), row(hw), row(hw),
        row(hw), row(2 * hw), row(H_MEM * HD_MEM), row(H_FOX), row(LANES),
    )
    return pl.pallas_call(
        _proj_kernel,
        grid=(t // tm,),
        in_specs=[row(D_MODEL), _full(w_main.shape), _full(w_q.shape), _full(w_abs.shape),
                  _full(q_g.shape), _full(kv_g.shape), _full(b_f.shape), tab, tab],
        out_specs=out_specs,
        out_shape=out_shape,
        compiler_params=_params(1),
    )(x, w_main, w_q, w_abs, q_g, kv_g, b_f, cos_t, sin_t)


def _memkv_kernel(x_ref, wk_ref, wv_ref, k_ref, v_ref, kb_ref, vb_ref):
    xb = x_ref[...].astype(BF16)
    k = _dot(xb, wk_ref[...])
    v = _dot(xb, wv_ref[...])
    k_ref[...] = k
    v_ref[...] = v
    kb_ref[...] = k.astype(BF16)
    vb_ref[...] = v.astype(BF16)


def _memkv(x, wk, wv, tm):
    t = x.shape[0]
    w = H_MEM * HD_MEM
    row = lambda n: pl.BlockSpec((tm, n), lambda i: (i, 0))
    return pl.pallas_call(
        _memkv_kernel,
        grid=(t // tm,),
        in_specs=[row(D_MODEL), _full(wk.shape), _full(wv.shape)],
        out_specs=(row(w), row(w), row(w), row(w)),
        out_shape=(jax.ShapeDtypeStruct((t, w), F32), jax.ShapeDtypeStruct((t, w), F32),
                   jax.ShapeDtypeStruct((t, w), BF16), jax.ShapeDtypeStruct((t, w), BF16)),
        compiler_params=_params(1),
    )(x, wk, wv)


CUM_BLOCK = 256


def _bf16_pieces(x):
    hi = x.astype(BF16)
    r1 = x - hi.astype(F32)
    mid = r1.astype(BF16)
    lo = (r1 - mid.astype(F32)).astype(BF16)
    return hi, mid, lo


def _cumsum_kernel(x_ref, c_ref):
    s = x_ref.shape[1]
    r = lax.broadcasted_iota(jnp.int32, (CUM_BLOCK, CUM_BLOCK), 0)
    c = lax.broadcasted_iota(jnp.int32, (CUM_BLOCK, CUM_BLOCK), 1)
    tri = jnp.where(r >= c, 1.0, 0.0).astype(BF16)

    def body(i, carry):
        r0 = pl.multiple_of(i * CUM_BLOCK, CUM_BLOCK)
        x = x_ref[0, pl.ds(r0, CUM_BLOCK), :]
        cs = sum(_dot(tri, piece) for piece in _bf16_pieces(x)) + carry
        c_ref[0, pl.ds(r0, CUM_BLOCK), :] = cs * LOG2E
        return cs[CUM_BLOCK - 1:CUM_BLOCK, :]

    lax.fori_loop(0, s // CUM_BLOCK, body, jnp.zeros((1, LANES), F32))


def _cumsum(x):
    b, s, _ = x.shape
    spec = pl.BlockSpec((1, s, LANES), lambda i: (i, 0, 0))
    return pl.pallas_call(
        _cumsum_kernel, grid=(b,), in_specs=[spec], out_specs=spec,
        out_shape=jax.ShapeDtypeStruct(x.shape, F32), compiler_params=_params(1),
    )(x)


def _online_softmax_step(s, v_ones, m_ref, acc_ref, shift=None):
    def lanes(x, width):
        return jnp.concatenate([x] * (width // LANES), axis=-1)

    m = m_ref[...]
    r = jnp.max(s, axis=-1, keepdims=True)
    m_new = jnp.maximum(m, r if shift is None else r + shift)
    alpha = jnp.exp2(m - m_new)
    p = jnp.exp2(s - lanes(m_new if shift is None else m_new - shift, s.shape[-1]))
    m_ref[...] = m_new
    acc_ref[...] = lanes(alpha, acc_ref.shape[-1]) * acc_ref[...] + _dot(p.astype(BF16), v_ones)


def _init_softmax_state(m_ref, acc_ref):
    m_ref[...] = jnp.full(m_ref.shape, NEG_INF, F32)
    acc_ref[...] = jnp.zeros(acc_ref.shape, F32)


def _normalised(acc):
    return acc[:, :LANES] / acc[:, LANES:]


def _kv_tile_counts(first_visible_end, last_visible_end, tk, kv_len, sk):
    n_full = jnp.minimum(first_visible_end // tk, kv_len // tk)
    n_any = jnp.minimum((jnp.minimum(last_visible_end, kv_len) + tk - 1) // tk, sk // tk)
    return n_full, n_any


def _kv_loop(n_full, n_any, consume):
    def body(kt, carry, masked):
        consume(kt, masked)
        return carry

    lax.fori_loop(0, n_full, functools.partial(body, masked=False), 0)
    lax.fori_loop(n_full, n_any, functools.partial(body, masked=True), 0)


def _kv_loop_pipelined(n_any, produce, consume):
    produce(0, 0)
    n_pairs = (n_any - 1) // 2

    def body(i, carry):
        kt = 2 * i
        produce(1, kt + 1)
        consume(0, kt, False)
        produce(0, kt + 2)
        consume(1, kt + 1, False)
        return carry

    lax.fori_loop(0, n_pairs, body, 0)
    kt0 = 2 * n_pairs

    @pl.when(kt0 == n_any - 1)
    def _():
        consume(0, kt0, True)

    @pl.when(kt0 < n_any - 1)
    def _():
        produce(1, kt0 + 1)
        consume(0, kt0, True)
        consume(1, kt0 + 1, True)


def _mla_kernel(q_ref, k_ref, v_ref, o_ref, m_ref, acc_ref, s_ref, *, tq, tk, kv_len, q_pos0):
    sk = k_ref.shape[1]
    rows = H_MLA * tq
    q_start = q_pos0 + pl.program_id(1) * tq
    _, n_any = _kv_tile_counts((q_start // CHUNK + 1) * CHUNK,
                               ((q_start + tq - 1) // CHUNK + 1) * CHUNK, tk, kv_len, sk)
    q = q_ref[...].reshape(rows, q_ref.shape[-1])
    q_chunk = (q_start + lax.broadcasted_iota(jnp.int32, (rows, 1), 0) % tq) // CHUNK

    def produce(slot, kt):
        s_ref[slot] = _dot_nt(q, k_ref[0, pl.ds(pl.multiple_of(kt * tk, tk), tk), :])

    def consume(slot, kt, masked):
        k0 = pl.multiple_of(kt * tk, tk)
        s = s_ref[slot]
        if masked:
            k_pos = k0 + lax.broadcasted_iota(jnp.int32, (1, tk), 1)
            mask = k_pos // CHUNK <= q_chunk
            if kv_len < sk:
                mask = jnp.logical_and(mask, k_pos < kv_len)
            s = jnp.where(mask, s, NEG_INF)
        _online_softmax_step(s, v_ref[0, pl.ds(k0, tk), :], m_ref, acc_ref)

    _init_softmax_state(m_ref, acc_ref)
    _kv_loop_pipelined(n_any, produce, consume)
    o = _normalised(acc_ref[...]).astype(BF16)
    for h in range(H_MLA):
        o_ref[:, h * KV_LORA:(h + 1) * KV_LORA] = o[h * tq:(h + 1) * tq, :]


def _mla_attention(qcat, kcat, latb, batch, sq, tq, tk, kv_len, q_pos0):
    sk = kcat.shape[1]
    nq = sq // tq
    assert tk % tq == 0 and q_pos0 % tq == 0 and tq % CHUNK in (0, tq)
    kern = functools.partial(_mla_kernel, tq=tq, tk=tk, kv_len=kv_len, q_pos0=q_pos0)
    return pl.pallas_call(
        kern,
        grid=(batch, nq),
        in_specs=[pl.BlockSpec((H_MLA, tq, 2 * LANES), lambda b, i: (0, b * nq + i, 0)),
                  pl.BlockSpec((1, sk, 2 * LANES), lambda b, i: (b, 0, 0)),
                  pl.BlockSpec((1, sk, 2 * LANES), lambda b, i: (b, 0, 0))],
        out_specs=pl.BlockSpec((tq, H_MLA * KV_LORA), lambda b, i: (b * nq + i, 0)),
        out_shape=jax.ShapeDtypeStruct((batch * sq, H_MLA * KV_LORA), BF16),
        scratch_shapes=[pltpu.VMEM((H_MLA * tq, LANES), F32),
                        pltpu.VMEM((H_MLA * tq, 2 * LANES), F32),
                        pltpu.VMEM((2, H_MLA * tq, tk), F32)],
        compiler_params=_params(2),
    )(qcat, kcat, latb)


def _fox_kernel(q_ref, k_ref, v_ref, cq_ref, ck_ref, o_ref, m_ref, acc_ref, cqb_ref, *, tq, tk,
                kv_len, q_pos0):
    sk = k_ref.shape[1]
    q_start = q_pos0 + pl.program_id(1) * tq
    n_full, n_any = _kv_tile_counts(q_start + 1, q_start + tq, tk, kv_len, sk)
    lane = lax.broadcasted_iota(jnp.int32, (tq, LANES), 1)
    n_pairs = H_FOX * HD_FOX // LANES
    q_pos = q_start + lax.broadcasted_iota(jnp.int32, (2 * tq, 1), 0) % tq
    q2 = []
    for j in range(n_pairs):
        q_pair = q_ref[:, j * LANES:(j + 1) * LANES]
        zero = jnp.zeros_like(q_pair)
        q2.append(jnp.concatenate([jnp.where(lane < HD_FOX, q_pair, zero),
                                   jnp.where(lane < HD_FOX, zero, q_pair)], axis=0))
        for half in range(2):
            col = jnp.sum(jnp.where(lane == 2 * j + half, cq_ref[0], 0.0), axis=-1, keepdims=True)
            cqb_ref[j, half * tq:(half + 1) * tq, :] = jnp.broadcast_to(col, (tq, LANES))

    def consume(kt, masked):
        k0 = pl.multiple_of(kt * tk, tk)
        if masked:
            k_pos = k0 + lax.broadcasted_iota(jnp.int32, (1, tk), 1)
            mask = k_pos <= q_pos
            if kv_len < sk:
                mask = jnp.logical_and(mask, k_pos < kv_len)
        for j in range(n_pairs):
            s = _dot_nt(q2[j], k_ref[0, pl.ds(k0, tk), j * LANES:(j + 1) * LANES])
            c_k = jnp.concatenate(
                [jnp.broadcast_to(ck_ref[0, 2 * j:2 * j + 1, pl.ds(k0, tk)], (tq, tk)),
                 jnp.broadcast_to(ck_ref[0, 2 * j + 1:2 * j + 2, pl.ds(k0, tk)], (tq, tk))], axis=0)
            s = s - c_k
            if masked:
                s = jnp.where(mask, s, NEG_INF)
            v_ones = v_ref[0, pl.ds(k0, tk), 2 * j * LANES:2 * (j + 1) * LANES]
            _online_softmax_step(s, v_ones, m_ref.at[j], acc_ref.at[j], shift=cqb_ref[j])

    _init_softmax_state(m_ref, acc_ref)
    _kv_loop(n_full, n_any, consume)
    for j in range(n_pairs):
        o = _normalised(acc_ref[j])
        o_ref[:, j * LANES:(j + 1) * LANES] = jnp.where(lane < HD_FOX, o[:tq], o[tq:]).astype(BF16)


def _fox_attention(fq, fkb, fvb, c_q, c_k, batch, sq, tq, tk, kv_len, q_pos0):
    sk = fkb.shape[1]
    nq = sq // tq
    w = H_FOX * HD_FOX
    kern = functools.partial(_fox_kernel, tq=tq, tk=tk, kv_len=kv_len, q_pos0=q_pos0)
    return pl.pallas_call(
        kern,
        grid=(batch, nq),
        in_specs=[pl.BlockSpec((tq, w), lambda b, i: (b * nq + i, 0)),
                  pl.BlockSpec((1, sk, w), lambda b, i: (b, 0, 0)),
                  pl.BlockSpec((1, sk, 2 * w), lambda b, i: (b, 0, 0)),
                  pl.BlockSpec((1, tq, LANES), lambda b, i: (b, i, 0)),
                  pl.BlockSpec((1, H_FOX, sk), lambda b, i: (b, 0, 0))],
        out_specs=pl.BlockSpec((tq, w), lambda b, i: (b * nq + i, 0)),
        out_shape=jax.ShapeDtypeStruct((batch * sq, w), BF16),
        scratch_shapes=[pltpu.VMEM((w // LANES, 2 * tq, LANES), F32),
                        pltpu.VMEM((w // LANES, 2 * tq, 2 * LANES), F32),
                        pltpu.VMEM((w // LANES, 2 * tq, LANES), F32)],
        compiler_params=_params(2),
    )(fq, fkb, fvb, c_q, c_k)


def _mem_kernel(q_ref, k_ref, v_ref, o_ref):
    for h in range(H_MEM):
        cols = slice(h * HD_MEM, (h + 1) * HD_MEM)
        s = _dot_nt(q_ref[:, cols], k_ref[0, :, cols]) * (MEM_SCALE * LOG2E)
        m = jnp.max(s, axis=-1, keepdims=True)
        p = jnp.exp2(s - m)
        l = jnp.sum(p, axis=-1, keepdims=True)
        o = _dot(p.astype(BF16), v_ref[0, :, cols]) / l
        o_ref[:, cols] = o.astype(BF16)


def _mem_attention(mq, mkb, mvb, batch, sq, tq):
    nq = sq // tq
    n_mem = mkb.shape[1]
    w = H_MEM * HD_MEM
    return pl.pallas_call(
        _mem_kernel,
        grid=(batch, nq),
        in_specs=[pl.BlockSpec((tq, w), lambda b, i: (b * nq + i, 0)),
                  pl.BlockSpec((1, n_mem, w), lambda b, i: (b, 0, 0)),
                  pl.BlockSpec((1, n_mem, w), lambda b, i: (b, 0, 0))],
        out_specs=pl.BlockSpec((tq, w), lambda b, i: (b * nq + i, 0)),
        out_shape=jax.ShapeDtypeStruct((batch * sq, w), BF16),
        compiler_params=_params(2),
    )(mq, mkb, mvb)


def _route(x1b, wr_t, bias):
    tm = x1b.shape[0]
    scores = _sigmoid(_dot_nt(wr_t, x1b))
    sel = scores + bias
    sub = lax.broadcasted_iota(jnp.int32, (GROUP_SIZE, tm), 0)
    gs = []
    for g in range(N_GROUPS):
        blk = sel[g * GROUP_SIZE:(g + 1) * GROUP_SIZE, :]
        m1 = jnp.max(blk, axis=0, keepdims=True)
        first = jnp.min(jnp.where(blk == m1, sub, GROUP_SIZE), axis=0, keepdims=True)
        m2 = jnp.max(jnp.where(sub == first, -jnp.inf, blk), axis=0, keepdims=True)
        gs.append(m1 + m2)
    e_idx = lax.broadcasted_iota(jnp.int32, (N_EXPERTS, tm), 0)
    allowed = jnp.zeros((N_EXPERTS, tm), jnp.bool_)
    for g in range(N_GROUPS):
        rank = jnp.zeros((1, tm), jnp.int32)
        for o in range(N_GROUPS):
            if o == g:
                continue
            beats = (gs[o] >= gs[g]) if o < g else (gs[o] > gs[g])
            rank = rank + beats.astype(jnp.int32)
        keep = rank < TOPK_GROUPS
        allowed = jnp.logical_or(allowed, jnp.logical_and(e_idx // GROUP_SIZE == g, keep))
    cand = jnp.where(allowed, sel, -jnp.inf)
    chosen = jnp.zeros((N_EXPERTS, tm), jnp.bool_)
    for _ in range(TOP_K):
        m = jnp.max(cand, axis=0, keepdims=True)
        first = jnp.min(jnp.where(cand == m, e_idx, N_EXPERTS), axis=0, keepdims=True)
        pick = e_idx == first
        chosen = jnp.logical_or(chosen, pick)
        cand = jnp.where(pick, -jnp.inf, cand)
    w = jnp.where(chosen, scores, 0.0)
    return w / jnp.sum(w, axis=0, keepdims=True) * ROUTED_SCALE, chosen


def _swiglu(xb, wg, wu):
    a = _dot(xb, wg)
    return (a * _sigmoid(a)) * _dot(xb, wu)


def _merge_kernel(x_ref, olat_ref, ofox_ref, omem_ref, wg_ref, wuv_ref, wbm_ref, wbf_ref, wbc_ref,
                  wo_ref, g1_ref, b1_ref, wsg_ref, wsu_ref, wsd_ref, x1b_ref, base_ref):
    x = x_ref[...]
    xb = x.astype(BF16)
    o_mla = jnp.concatenate(
        [_dot(olat_ref[:, j * 2 * KV_LORA:(j + 1) * 2 * KV_LORA], wuv_ref[j]).astype(BF16)
         for j in range(H_MLA // 2)], axis=-1)

    def gate(n):
        return _sigmoid(_dot(xb, wg_ref[:, n * D_MODEL:(n + 1) * D_MODEL]))

    y = gate(0) * _dot(o_mla, wbm_ref[...])
    y = y + gate(1) * _dot(ofox_ref[...], wbf_ref[...])
    y = y + gate(2) * _dot(omem_ref[...], wbc_ref[...])
    mix = _dot(y.astype(BF16), wo_ref[...])
    x1 = _layer_norm(ALPHA * x + mix, g1_ref[...], b1_ref[...])
    x1b = x1.astype(BF16)
    x1b_ref[...] = x1b
    shared = _dot(_swiglu(x1b, wsg_ref[...], wsu_ref[...]).astype(BF16), wsd_ref[...])
    base_ref[...] = ALPHA * x1 + shared


def _merge(x, olat, ofox, omem, wg, wuv, wbm, wbf, wbc, wo, g1, b1, wsg, wsu, wsd, tm):
    t = x.shape[0]
    row = lambda n: pl.BlockSpec((tm, n), lambda i: (i, 0))
    ws = (wg, wuv, wbm, wbf, wbc, wo, g1, b1, wsg, wsu, wsd)
    return pl.pallas_call(
        _merge_kernel,
        grid=(t // tm,),
        in_specs=[row(D_MODEL), row(H_MLA * KV_LORA), row(H_FOX * HD_FOX), row(H_MEM * HD_MEM)]
        + [_full(w.shape) for w in ws],
        out_specs=(row(D_MODEL), row(D_MODEL)),
        out_shape=(jax.ShapeDtypeStruct((t, D_MODEL), BF16), jax.ShapeDtypeStruct((t, D_MODEL), F32)),
        compiler_params=_params(1),
    )(x, olat, ofox, omem, *ws)


BF16_ROWS = 16
PLACE_ROWS = 512
EXPERT_ROWS = 4096
EXPERT_CHUNK = 512
EXPERTS_PER_STEP = 4


def _capacity(tm):
    return min(tm, _round_up(2 * tm * TOP_K // N_EXPERTS, BF16_ROWS))


def _dispatch_kernel(x1b_ref, wr_ref, rb_ref, xs_ref, gate_ref, slot_ref, govf_ref, flag_ref,
                     pm_ref, *, cap):
    xb = x1b_ref[...]
    tm = xb.shape[0]
    gate, chosen = _route(xb, wr_ref[...], rb_ref[...])
    r = lax.broadcasted_iota(jnp.int32, (tm, tm), 0)
    c = lax.broadcasted_iota(jnp.int32, (tm, tm), 1)
    earlier = jnp.where(r < c, 1.0, 0.0).astype(BF16)
    rank = _dot(jnp.where(chosen, 1.0, 0.0).astype(BF16), earlier)
    fits = jnp.logical_and(chosen, rank < cap)
    slot = jnp.where(fits, rank, -1.0)
    gate_ref[...] = jnp.where(fits, gate, 0.0)
    slot_ref[...] = slot
    g_ovf = jnp.where(fits, 0.0, gate)
    govf_ref[...] = jnp.concatenate([g_ovf, jnp.zeros((LANES - N_EXPERTS, tm), F32)], axis=0).T
    any_ovf = jnp.max(jnp.max(g_ovf, axis=0, keepdims=True), axis=1, keepdims=True)
    flag_ref[0] = jnp.broadcast_to(jnp.where(any_ovf > 0.0, 1, 0), flag_ref.shape[1:]).astype(jnp.int32)
    s_iota = lax.broadcasted_iota(jnp.int32, (cap, tm), 0).astype(F32)
    for e in range(N_EXPERTS):
        pm_ref[e * cap:(e + 1) * cap, :] = jnp.where(s_iota == slot[e:e + 1, :], 1.0, 0.0).astype(BF16)
    per = PLACE_ROWS // cap
    for g in range(N_EXPERTS // per):
        xs = _dot(pm_ref[g * PLACE_ROWS:(g + 1) * PLACE_ROWS, :], xb).astype(BF16)
        xs_ref[g * per:(g + 1) * per] = xs.reshape(per, cap, D_MODEL)


def _dispatch(x1b, wr_t, rb, tm, cap):
    t = x1b.shape[0]
    nt = t // tm
    col = pl.BlockSpec((N_EXPERTS, tm), lambda i: (0, i))
    return pl.pallas_call(
        functools.partial(_dispatch_kernel, cap=cap),
        grid=(nt,),
        in_specs=[pl.BlockSpec((tm, D_MODEL), lambda i: (i, 0)), _full(wr_t.shape), _full(rb.shape)],
        out_specs=(pl.BlockSpec((N_EXPERTS, cap, D_MODEL), lambda i: (0, i, 0)), col, col,
                   pl.BlockSpec((tm, LANES), lambda i: (i, 0)),
                   pl.BlockSpec((1, 8, LANES), lambda i: (i, 0, 0))),
        out_shape=(jax.ShapeDtypeStruct((N_EXPERTS, nt * cap, D_MODEL), BF16),
                   jax.ShapeDtypeStruct((N_EXPERTS, t), F32),
                   jax.ShapeDtypeStruct((N_EXPERTS, t), F32),
                   jax.ShapeDtypeStruct((t, LANES), F32),
                   jax.ShapeDtypeStruct((nt, 8, LANES), jnp.int32)),
        scratch_shapes=[pltpu.VMEM((N_EXPERTS * cap, tm), BF16)],
        compiler_params=_params(1),
    )(x1b, wr_t, rb)


def _expert_kernel(xs_ref, wg_ref, wu_ref, wd_ref, ys_ref, wgb_ref, wub_ref, wdb_ref, *, chunk):
    @pl.when(pl.program_id(1) == 0)
    def _():
        wgb_ref[...] = wg_ref[0].astype(BF16)
        wub_ref[...] = wu_ref[0].astype(BF16)
        wdb_ref[...] = wd_ref[0].astype(BF16)

    def body(c, carry):
        r0 = pl.multiple_of(c * chunk, chunk)
        hid = _swiglu(xs_ref[0, pl.ds(r0, chunk), :], wgb_ref[...], wub_ref[...])
        ys_ref[0, pl.ds(r0, chunk), :] = _dot(hid.astype(BF16), wdb_ref[...]).astype(BF16)
        return carry

    lax.fori_loop(0, xs_ref.shape[1] // chunk, body, 0)


def _experts(xs, weg, weu, wed):
    n_e, rows, _ = xs.shape
    rb = min(EXPERT_ROWS, rows)
    blk = pl.BlockSpec((1, rb, D_MODEL), lambda e, i: (e, i, 0))
    return pl.pallas_call(
        functools.partial(_expert_kernel, chunk=min(EXPERT_CHUNK, rb)),
        grid=(n_e, rows // rb),
        in_specs=[blk,
                  pl.BlockSpec((1, D_MODEL, D_EXPERT), lambda e, i: (e, 0, 0)),
                  pl.BlockSpec((1, D_MODEL, D_EXPERT), lambda e, i: (e, 0, 0)),
                  pl.BlockSpec((1, D_EXPERT, D_MODEL), lambda e, i: (e, 0, 0))],
        out_specs=blk,
        out_shape=jax.ShapeDtypeStruct(xs.shape, BF16),
        scratch_shapes=[pltpu.VMEM((D_MODEL, D_EXPERT), BF16), pltpu.VMEM((D_MODEL, D_EXPERT), BF16),
                        pltpu.VMEM((D_EXPERT, D_MODEL), BF16)],
        compiler_params=_params(2),
    )(xs, weg, weu, wed)


def _dense_routed_kernel(flag_ref, x1b_ref, gate_ref, weg_ref, weu_ref, wed_ref, out_ref):
    step = pl.program_id(1)

    @pl.when(step == 0)
    def _():
        out_ref[...] = jnp.zeros_like(out_ref)

    @pl.when(flag_ref[pl.program_id(0)] > 0)
    def _():
        xb = x1b_ref[...]
        gate = gate_ref[...]
        lane = lax.broadcasted_iota(jnp.int32, gate.shape, 1)
        for j in range(EXPERTS_PER_STEP):
            e = step * EXPERTS_PER_STEP + j
            g_col = jnp.sum(jnp.where(lane == e, gate, 0.0), axis=-1, keepdims=True)
            hid = _swiglu(xb, weg_ref[j], weu_ref[j])
            out_ref[...] += g_col * _dot(hid.astype(BF16), wed_ref[j])


def _dense_routed(flags, x1b, gate, weg, weu, wed, tm):
    t = x1b.shape[0]
    eb = EXPERTS_PER_STEP
    row = lambda n: pl.BlockSpec((tm, n), lambda i, s, f: (i, 0))
    wspec = lambda a, b: pl.BlockSpec((eb, a, b), lambda i, s, f: (jnp.where(f[i] > 0, s, 0), 0, 0))
    return pl.pallas_call(
        _dense_routed_kernel,
        grid_spec=pltpu.PrefetchScalarGridSpec(
            num_scalar_prefetch=1,
            grid=(t // tm, N_EXPERTS // eb),
            in_specs=[row(D_MODEL), row(LANES), wspec(D_MODEL, D_EXPERT), wspec(D_MODEL, D_EXPERT),
                      wspec(D_EXPERT, D_MODEL)],
            out_specs=row(D_MODEL)),
        out_shape=jax.ShapeDtypeStruct((t, D_MODEL), F32),
        compiler_params=_params(2),
    )(flags, x1b, gate, weg, weu, wed)


def _combine_kernel(gate_ref, slot_ref, ys_ref, base_ref, *rest, cap, has_overflow):
    if has_overflow:
        rovf_ref, g2_ref, b2_ref, y_ref, pw_ref = rest
    else:
        g2_ref, b2_ref, y_ref, pw_ref = rest
    tm = base_ref.shape[0]
    gate = gate_ref[...]
    slot = slot_ref[...]
    s_iota = lax.broadcasted_iota(jnp.int32, (cap, tm), 0).astype(F32)
    for e in range(N_EXPERTS):
        pw_ref[e * cap:(e + 1) * cap, :] = jnp.where(
            s_iota == slot[e:e + 1, :], gate[e:e + 1, :], 0.0).astype(BF16)
    routed = rovf_ref[...] if has_overflow else jnp.zeros((tm, D_MODEL), F32)
    per = PLACE_ROWS // cap
    for g in range(N_EXPERTS // per):
        ys = ys_ref[g * per:(g + 1) * per].reshape(PLACE_ROWS, D_MODEL)
        routed = routed + lax.dot_general(pw_ref[g * PLACE_ROWS:(g + 1) * PLACE_ROWS, :], ys,
                                          (((0,), (0,)), ((), ())), preferred_element_type=F32)
    y_ref[...] = _layer_norm(base_ref[...] + routed, g2_ref[...], b2_ref[...])


def _combine(gate_t, slot_t, ys, base, rovf, g2, b2, tm, cap):
    t = base.shape[0]
    col = pl.BlockSpec((N_EXPERTS, tm), lambda i: (0, i))
    row = pl.BlockSpec((tm, D_MODEL), lambda i: (i, 0))
    has_overflow = rovf is not None
    operands = (gate_t, slot_t, ys, base) + ((rovf,) if has_overflow else ()) + (g2, b2)
    return pl.pallas_call(
        functools.partial(_combine_kernel, cap=cap, has_overflow=has_overflow),
        grid=(t // tm,),
        in_specs=[col, col, pl.BlockSpec((N_EXPERTS, cap, D_MODEL), lambda i: (0, i, 0)), row]
        + ([row] if has_overflow else []) + [_full(g2.shape), _full(b2.shape)],
        out_specs=row,
        out_shape=jax.ShapeDtypeStruct((t, D_MODEL), F32),
        scratch_shapes=[pltpu.VMEM((N_EXPERTS * cap, tm), BF16)],
        compiler_params=_params(1),
    )(*operands)


def _moe_ln2(x1b, base, wr_t, rb, weg, weu, wed, g2, b2, tm):
    t = x1b.shape[0]
    cap = _capacity(tm)
    assert PLACE_ROWS % cap == 0 and (N_EXPERTS * cap) % PLACE_ROWS == 0
    xs, gate_t, slot_t, gate_ovf, flags = _dispatch(x1b, wr_t, rb, tm, cap)
    ys = _experts(xs, weg, weu, wed)
    flags = flags[:, 0, 0]

    def with_overflow():
        rovf = _dense_routed(flags, x1b, gate_ovf, weg.astype(BF16), weu.astype(BF16),
                             wed.astype(BF16), tm)
        return _combine(gate_t, slot_t, ys, base, rovf, g2, b2, tm, cap)

    return lax.cond(jnp.any(flags > 0), with_overflow,
                    lambda: _combine(gate_t, slot_t, ys, base, None, g2, b2, tm, cap))


def _rot_half_cols(w):
    half = w.shape[-1] // 2
    return jnp.concatenate([-w[..., half:], w[..., :half]], axis=-1)


def _block_diag2(a, b):
    za = jnp.zeros((a.shape[0], b.shape[1]), a.dtype)
    zb = jnp.zeros((b.shape[0], a.shape[1]), a.dtype)
    return jnp.concatenate([jnp.concatenate([a, za], 1), jnp.concatenate([zb, b], 1)], 0)


def _prep_weights(w_in, b_f, q_norm_g, w_q_up, kv_norm_g, w_kv_up):
    widths = (Q_LORA, KV_LORA, ROPE_DIM, H_FOX * HD_FOX, H_FOX * HD_FOX, H_FOX * HD_FOX, H_FOX,
              H_MEM * HD_MEM, N_BRANCH * D_MODEL)
    offs = np.cumsum((0,) + widths)
    w_ql, w_ckv, w_kr, w_fq, w_fk, w_fv, w_fl, w_mq, w_g = (
        w_in[:, offs[i]:offs[i + 1]] for i in range(len(widths)))
    rep = LANES // ROPE_DIM
    w_main = jnp.concatenate(
        [w_ql, w_ckv, jnp.tile(w_kr, (1, rep)), jnp.tile(_rot_half_cols(w_kr), (1, rep)),
         w_fq, w_fk, w_fv, w_mq, jnp.pad(w_fl, ((0, 0), (0, LANES - H_FOX)))], axis=1).astype(BF16)
    assert w_main.shape[1] == C_END
    wq3 = w_q_up.reshape(Q_LORA, H_MLA, NOPE_DIM + ROPE_DIM)
    wq_nope = wq3[:, :, :NOPE_DIM].reshape(Q_LORA, H_MLA * NOPE_DIM)
    wq_rope = wq3[:, :, NOPE_DIM:]
    w_q = jnp.concatenate(
        [wq_nope, wq_rope.reshape(Q_LORA, -1), _rot_half_cols(wq_rope).reshape(Q_LORA, -1)],
        axis=1).astype(BF16)
    wkv3 = w_kv_up.reshape(KV_LORA, H_MLA, NOPE_DIM + V_DIM)
    w_uk = wkv3[:, :, :NOPE_DIM]
    w_uv = wkv3[:, :, NOPE_DIM:]
    w_abs = jnp.stack([_block_diag2(w_uk[:, 2 * j].T, w_uk[:, 2 * j + 1].T)
                       for j in range(H_MLA // 2)]).astype(BF16)
    w_uvp = jnp.stack([_block_diag2(w_uv[:, 2 * j], w_uv[:, 2 * j + 1])
                       for j in range(H_MLA // 2)]).astype(BF16)
    b_fp = jnp.pad(b_f, (0, LANES - H_FOX)).reshape(1, LANES)
    return (w_main, w_q, w_abs, q_norm_g.reshape(1, -1), kv_norm_g.reshape(1, -1), b_fp,
            w_g.astype(BF16), w_uvp)


def _rope_tables(pos):
    half = ROPE_DIM // 2
    inv_freq = ROPE_THETA ** (-jnp.arange(half, dtype=F32) / half)
    ang = pos.astype(F32)[:, None] * inv_freq[None, :]
    rep = 2 * LANES // ROPE_DIM
    cos = jnp.tile(jnp.concatenate([jnp.cos(ang)] * 2, axis=-1), (1, rep))
    sin = jnp.tile(jnp.concatenate([jnp.sin(ang)] * 2, axis=-1), (1, rep))
    return cos, sin


def _round_up(n, m):
    return (n + m - 1) // m * m


def kernel(x_prompt, x_sample, mem_prompt, cache_mla_latent, cache_mla_krope, cache_fox_k,
           cache_fox_v, cache_fox_logf, cache_mem_k, cache_mem_v, ln1_g, ln1_b, w_in, b_f,
           q_norm_g, w_q_up, kv_norm_g, w_kv_up, w_mem_k, w_mem_v, w_br_mla, w_br_fox,
           w_br_mem, w_out, ln2_g, ln2_b, w_router, router_bias, w_e_gate, w_e_up, w_e_down,
           w_s_gate, w_s_up, w_s_down):
    assert w_in.shape[0] == DEPTH == 1
    batch, seq, _ = x_prompt.shape
    dec_batch, dec_seq, _ = x_sample.shape
    past = cache_mla_latent.shape[2]
    n_mem = mem_prompt.shape[1]
    l = 0

    (w_main, w_q, w_abs, q_g, kv_g, b_fp, w_gate, w_uvp) = _prep_weights(
        w_in[l], b_f[l], q_norm_g[l], w_q_up[l], kv_norm_g[l], w_kv_up[l])
    merge_w = (w_gate, w_uvp, w_br_mla[l].astype(BF16), w_br_fox[l].astype(BF16),
               w_br_mem[l].astype(BF16), w_out[l].astype(BF16), ln1_g[l].reshape(1, -1),
               ln1_b[l].reshape(1, -1), w_s_gate[l].astype(BF16), w_s_up[l].astype(BF16),
               w_s_down[l].astype(BF16))
    moe_w = (w_router[l].T.astype(BF16), router_bias[l].reshape(-1, 1),
             w_e_gate[l], w_e_up[l], w_e_down[l],
             ln2_g[l].reshape(1, -1), ln2_b[l].reshape(1, -1))

    def tail(x, olat, ofox, omem, tm, tm_moe):
        x1b, base = _merge(x, olat, ofox, omem, *merge_w, tm)
        return _moe_ln2(x1b, base, *moe_w, tm_moe)

    tp = batch * seq
    xp = x_prompt.reshape(tp, D_MODEL)
    tm_p = min(512, seq)
    tq_p = min(256, seq)
    tk_p = min(512, seq)
    cos_p, sin_p = _rope_tables(jnp.arange(seq, dtype=jnp.int32))
    (qcat, kcat, lat, latb, kr, fq, fk, fv, fkb, fvb, mq, logf, logfp) = _proj(
        xp, w_main, w_q, w_abs, q_g, kv_g, b_fp, cos_p, sin_p, tm_p)
    tmem = batch * n_mem
    m_k, m_v, m_kb, m_vb = _memkv(mem_prompt.reshape(tmem, D_MODEL), w_mem_k[l].astype(BF16),
                                  w_mem_v[l].astype(BF16), min(512, tmem))
    c_p = _cumsum(logfp.reshape(batch, seq, LANES))
    c_keys = jnp.swapaxes(c_p[:, :, :H_FOX], 1, 2)
    olat = _mla_attention(qcat, kcat.reshape(batch, seq, -1), latb.reshape(batch, seq, -1),
                          batch, seq, tq_p, tk_p, seq, 0)
    ofox = _fox_attention(fq, fkb.reshape(batch, seq, -1), fvb.reshape(batch, seq, -1), c_p, c_keys,
                          batch, seq, tk_p, tk_p, seq, 0)
    omem = _mem_attention(mq, m_kb.reshape(batch, n_mem, -1), m_vb.reshape(batch, n_mem, -1),
                          batch, seq, tq_p)
    y_p = tail(xp, olat, ofox, omem, tm_p, min(256, seq)).reshape(batch, seq, D_MODEL)

    ts = dec_batch * dec_seq
    xs = x_sample.reshape(ts, D_MODEL)
    cos_s, sin_s = _rope_tables(jnp.tile(past + jnp.arange(dec_seq, dtype=jnp.int32), dec_batch))
    (qcat_s, kcat_s, lat_s, latb_s, kr_s, fq_s, fk_s, fv_s, fkb_s, fvb_s, mq_s, logf_s,
     logfp_s) = _proj(xs, w_main, w_q, w_abs, q_g, kv_g, b_fp, cos_s, sin_s, ts)
    kv_len = past + dec_seq
    tk_s = 1024
    sk = _round_up(kv_len, tk_s)

    def with_cache(cache, new):
        new = new.reshape(dec_batch, dec_seq, -1)
        pad = jnp.zeros((dec_batch, sk - kv_len, new.shape[-1]), new.dtype)
        return jnp.concatenate([cache.astype(new.dtype), new, pad], axis=1)

    rep = LANES // ROPE_DIM
    kcat_all = with_cache(
        jnp.concatenate([cache_mla_latent[l], jnp.tile(cache_mla_krope[l], (1, 1, rep))], axis=-1),
        kcat_s)
    lat_all = with_cache(
        jnp.concatenate([cache_mla_latent[l], jnp.ones((dec_batch, past, LANES), F32)], axis=-1), latb_s)
    fk_all = with_cache(cache_fox_k[l].reshape(dec_batch, past, -1), fkb_s)
    n_pairs = H_FOX * HD_FOX // LANES
    fv_pairs = cache_fox_v[l].reshape(dec_batch, past, n_pairs, LANES)
    fv_all = with_cache(
        jnp.concatenate([fv_pairs, jnp.ones_like(fv_pairs)], axis=-1).reshape(dec_batch, past, -1),
        fvb_s)
    logf_all = with_cache(jnp.pad(cache_fox_logf[l], ((0, 0), (0, 0), (0, LANES - H_FOX))), logfp_s)
    c_s = _cumsum(logf_all)
    c_keys_s = jnp.swapaxes(c_s[:, :, :H_FOX], 1, 2)
    c_q_s = c_s[:, past:kv_len]
    olat_s = _mla_attention(qcat_s, kcat_all, lat_all, dec_batch, dec_seq, dec_seq, tk_s, kv_len, past)
    ofox_s = _fox_attention(fq_s, fk_all, fv_all, c_q_s, c_keys_s, dec_batch, dec_seq, dec_seq, tk_s,
                            kv_len, past)
    omem_s = _mem_attention(mq_s, cache_mem_k[l].reshape(dec_batch, n_mem, -1).astype(BF16),
                            cache_mem_v[l].reshape(dec_batch, n_mem, -1).astype(BF16),
                            dec_batch, dec_seq, dec_seq)
    y_s = tail(xs, olat_s, ofox_s, omem_s, ts, ts).reshape(dec_batch, dec_seq, D_MODEL)

    def stack(a, b, s, *tail_shape):
        return a.reshape(1, b, s, *tail_shape)

    return (y_p, y_s,
            stack(lat, batch, seq, KV_LORA), stack(kr, batch, seq, ROPE_DIM),
            stack(fk, batch, seq, H_FOX, HD_FOX), stack(fv, batch, seq, H_FOX, HD_FOX),
            stack(logf, batch, seq, H_FOX),
            stack(m_k, batch, n_mem, H_MEM, HD_MEM), stack(m_v, batch, n_mem, H_MEM, HD_MEM),
            stack(lat_s, dec_batch, dec_seq, KV_LORA), stack(kr_s, dec_batch, dec_seq, ROPE_DIM),
            stack(fk_s, dec_batch, dec_seq, H_FOX, HD_FOX), stack(fv_s, dec_batch, dec_seq, H_FOX, HD_FOX),
            stack(logf_s, dec_batch, dec_seq, H_FOX))
```

```python
import functools

import jax
import jax.numpy as jnp
import numpy as np
from jax import lax
from jax.experimental import pallas as pl
from jax.experimental.pallas import tpu as pltpu

F32 = jnp.float32
BF16 = jnp.bfloat16

D_MODEL = 1024
CHUNK = 64
H_MLA = 8
Q_LORA = 256
KV_LORA = 128
NOPE_DIM = 64
ROPE_DIM = 32
V_DIM = 64
ROPE_THETA = 10000.0
MLA_SCALE = (NOPE_DIM + ROPE_DIM) ** -0.5
H_FOX = 8
HD_FOX = 64
FOX_SCALE = HD_FOX ** -0.5
H_MEM = 4
HD_MEM = 128
MEM_SCALE = HD_MEM ** -0.5
N_BRANCH = 3
N_EXPERTS = 64
TOP_K = 8
N_GROUPS = 8
GROUP_SIZE = N_EXPERTS // N_GROUPS
TOPK_GROUPS = 4
D_EXPERT = 256
ROUTED_SCALE = 2.5
DEPTH = 1
ALPHA = (2 * DEPTH) ** 0.25
NORM_EPS = 1e-5
NEG_INF = -1e30

LOG2E = 1.4426950408889634
LANES = 128
VMEM_LIMIT = 56 * 1024 * 1024

C_QLAT = 0
C_CKV = C_QLAT + Q_LORA
C_KR = C_CKV + KV_LORA
C_KRR = C_KR + LANES
C_FQ = C_KRR + LANES
C_FK = C_FQ + H_FOX * HD_FOX
C_FV = C_FK + H_FOX * HD_FOX
C_MQ = C_FV + H_FOX * HD_FOX
C_FL = C_MQ + H_MEM * HD_MEM
C_END = C_FL + LANES


def _params(n_grid_axes):
    return pltpu.CompilerParams(
        dimension_semantics=("arbitrary",) * n_grid_axes, vmem_limit_bytes=VMEM_LIMIT)


def _full(shape):
    return pl.BlockSpec(shape, lambda *_: (0,) * len(shape))


def _dot(a, b):
    return jnp.dot(a, b, preferred_element_type=F32)


def _dot_nt(a, b):
    return lax.dot_general(a, b, (((1,), (1,)), ((), ())), preferred_element_type=F32)


def _sigmoid(x):
    return 1.0 / (1.0 + jnp.exp(-x))


def _rms(x, g):
    ms = jnp.mean(x * x, axis=-1, keepdims=True)
    return x * lax.rsqrt(ms + NORM_EPS) * g


def _layer_norm(x, g, b):
    mu = jnp.mean(x, axis=-1, keepdims=True)
    xc = x - mu
    var = jnp.mean(xc * xc, axis=-1, keepdims=True)
    return xc * lax.rsqrt(var + NORM_EPS) * g + b


def _proj_kernel(x_ref, w_ref, wq_ref, wabs_ref, qg_ref, kvg_ref, bf_ref, cos_ref, sin_ref,
                 qcat_ref, kcat_ref, lat_ref, latb_ref, kr_ref, fq_ref, fk_ref, fv_ref,
                 fkb_ref, fvb_ref, mq_ref, logf_ref, logfp_ref):
    xb = x_ref[...].astype(BF16)
    tm = xb.shape[0]

    def proj(c0, width):
        return _dot(xb, w_ref[:, c0:c0 + width])

    cos = cos_ref[...]
    sin = sin_ref[...]
    qn = _rms(proj(C_QLAT, Q_LORA), qg_ref[...]).astype(BF16)
    q2 = _dot(qn, wq_ref[...])
    nq = H_MLA * NOPE_DIM
    nr = H_MLA * ROPE_DIM
    q_rope = (q2[:, nq:nq + nr] * cos + q2[:, nq + nr:nq + 2 * nr] * sin) * (MLA_SCALE * LOG2E)
    lane = lax.broadcasted_iota(jnp.int32, (tm, LANES), 1)
    heads_per_group = LANES // ROPE_DIM
    for j in range(H_MLA // 2):
        q_abs = _dot(q2[:, j * LANES:(j + 1) * LANES].astype(BF16), wabs_ref[j]) * (MLA_SCALE * LOG2E)
        for half in range(2):
            h = 2 * j + half
            grp = h // heads_per_group
            qr = q_rope[:, grp * LANES:(grp + 1) * LANES]
            qr = jnp.where(lane // ROPE_DIM == h % heads_per_group, qr, 0.0)
            qcat_ref[h, :, 0:LANES] = q_abs[:, half * LANES:(half + 1) * LANES].astype(BF16)
            qcat_ref[h, :, LANES:2 * LANES] = qr.astype(BF16)
    lat = _rms(proj(C_CKV, KV_LORA), kvg_ref[...])
    lat_ref[...] = lat
    ones = jnp.ones((tm, LANES), BF16)
    latb_ref[:, 0:KV_LORA] = lat.astype(BF16)
    latb_ref[:, KV_LORA:KV_LORA + LANES] = ones
    kr = proj(C_KR, LANES) * cos[:, :LANES] + proj(C_KRR, LANES) * sin[:, :LANES]
    kr_ref[...] = kr[:, :ROPE_DIM]
    kcat_ref[:, 0:KV_LORA] = lat.astype(BF16)
    kcat_ref[:, KV_LORA:KV_LORA + LANES] = kr.astype(BF16)
    fq_ref[...] = (proj(C_FQ, H_FOX * HD_FOX) * (FOX_SCALE * LOG2E)).astype(BF16)
    fk = proj(C_FK, H_FOX * HD_FOX)
    fk_ref[...] = fk
    fkb_ref[...] = fk.astype(BF16)
    fv = proj(C_FV, H_FOX * HD_FOX)
    fv_ref[...] = fv
    for j in range(H_FOX * HD_FOX // LANES):
        fvb_ref[:, 2 * j * LANES:(2 * j + 1) * LANES] = fv[:, j * LANES:(j + 1) * LANES].astype(BF16)
        fvb_ref[:, (2 * j + 1) * LANES:(2 * j + 2) * LANES] = ones
    mq_ref[...] = proj(C_MQ, H_MEM * HD_MEM).astype(BF16)
    z = proj(C_FL, LANES) + bf_ref[...]
    log_f = jnp.minimum(z, 0.0) - jnp.log1p(jnp.exp(-jnp.abs(z)))
    logfp_ref[...] = log_f
    logf_ref[...] = log_f[:, :H_FOX]


def _proj(x, w_main, w_q, w_abs, q_g, kv_g, b_f, cos_t, sin_t, tm):
    t = x.shape[0]
    n_tab = cos_t.shape[0] // tm
    row = lambda w: pl.BlockSpec((tm, w), lambda i: (i, 0))
    tab = pl.BlockSpec((tm, 2 * LANES), lambda i: (i % n_tab, 0))
    hw = H_FOX * HD_FOX
    out_shape = (
        jax.ShapeDtypeStruct((H_MLA, t, 2 * LANES), BF16),
        jax.ShapeDtypeStruct((t, 2 * LANES), BF16),
        jax.ShapeDtypeStruct((t, KV_LORA), F32),
        jax.ShapeDtypeStruct((t, KV_LORA + LANES), BF16),
        jax.ShapeDtypeStruct((t, ROPE_DIM), F32),
        jax.ShapeDtypeStruct((t, hw), BF16),
        jax.ShapeDtypeStruct((t, hw), F32),
        jax.ShapeDtypeStruct((t, hw), F32),
        jax.ShapeDtypeStruct((t, hw), BF16),
        jax.ShapeDtypeStruct((t, 2 * hw), BF16),
        jax.ShapeDtypeStruct((t, H_MEM * HD_MEM), BF16),
        jax.ShapeDtypeStruct((t, H_FOX), F32),
        jax.ShapeDtypeStruct((t, LANES), F32),
    )
    out_specs = (
        pl.BlockSpec((H_MLA, tm, 2 * LANES), lambda i: (0, i, 0)),
        row(2 * LANES), row(KV_LORA), row(KV_LORA + LANES), row(ROPE_DIM), row(hw), row(hw), row(hw),
        row(hw), row(2 * hw), row(H_MEM * HD_MEM), row(H_FOX), row(LANES),
    )
    return pl.pallas_call(
        _proj_kernel,
        grid=(t // tm,),
        in_specs=[row(D_MODEL), _full(w_main.shape), _full(w_q.shape), _full(w_abs.shape),
                  _full(q_g.shape), _full(kv_g.shape), _full(b_f.shape), tab, tab],
        out_specs=out_specs,
        out_shape=out_shape,
        compiler_params=_params(1),
    )(x, w_main, w_q, w_abs, q_g, kv_g, b_f, cos_t, sin_t)


def _memkv_kernel(x_ref, wk_ref, wv_ref, k_ref, v_ref, kb_ref, vb_ref):
    xb = x_ref[...].astype(BF16)
    k = _dot(xb, wk_ref[...])
    v = _dot(xb, wv_ref[...])
    k_ref[...] = k
    v_ref[...] = v
    kb_ref[...] = k.astype(BF16)
    vb_ref[...] = v.astype(BF16)


def _memkv(x, wk, wv, tm):
    t = x.shape[0]
    w = H_MEM * HD_MEM
    row = lambda n: pl.BlockSpec((tm, n), lambda i: (i, 0))
    return pl.pallas_call(
        _memkv_kernel,
        grid=(t // tm,),
        in_specs=[row(D_MODEL), _full(wk.shape), _full(wv.shape)],
        out_specs=(row(w), row(w), row(w), row(w)),
        out_shape=(jax.ShapeDtypeStruct((t, w), F32), jax.ShapeDtypeStruct((t, w), F32),
                   jax.ShapeDtypeStruct((t, w), BF16), jax.ShapeDtypeStruct((t, w), BF16)),
        compiler_params=_params(1),
    )(x, wk, wv)


CUM_BLOCK = 256


def _bf16_pieces(x):
    hi = x.astype(BF16)
    r1 = x - hi.astype(F32)
    mid = r1.astype(BF16)
    lo = (r1 - mid.astype(F32)).astype(BF16)
    return hi, mid, lo


def _cumsum_kernel(x_ref, c_ref):
    s = x_ref.shape[1]
    r = lax.broadcasted_iota(jnp.int32, (CUM_BLOCK, CUM_BLOCK), 0)
    c = lax.broadcasted_iota(jnp.int32, (CUM_BLOCK, CUM_BLOCK), 1)
    tri = jnp.where(r >= c, 1.0, 0.0).astype(BF16)

    def body(i, carry):
        r0 = pl.multiple_of(i * CUM_BLOCK, CUM_BLOCK)
        x = x_ref[0, pl.ds(r0, CUM_BLOCK), :]
        cs = sum(_dot(tri, piece) for piece in _bf16_pieces(x)) + carry
        c_ref[0, pl.ds(r0, CUM_BLOCK), :] = cs * LOG2E
        return cs[CUM_BLOCK - 1:CUM_BLOCK, :]

    lax.fori_loop(0, s // CUM_BLOCK, body, jnp.zeros((1, LANES), F32))


def _cumsum(x):
    b, s, _ = x.shape
    spec = pl.BlockSpec((1, s, LANES), lambda i: (i, 0, 0))
    return pl.pallas_call(
        _cumsum_kernel, grid=(b,), in_specs=[spec], out_specs=spec,
        out_shape=jax.ShapeDtypeStruct(x.shape, F32), compiler_params=_params(1),
    )(x)


def _online_softmax_step(s, v_ones, m_ref, acc_ref, shift=None):
    def lanes(x, width):
        return jnp.concatenate([x] * (width // LANES), axis=-1)

    m = m_ref[...]
    r = jnp.max(s, axis=-1, keepdims=True)
    m_new = jnp.maximum(m, r if shift is None else r + shift)
    alpha = jnp.exp2(m - m_new)
    p = jnp.exp2(s - lanes(m_new if shift is None else m_new - shift, s.shape[-1]))
    m_ref[...] = m_new
    acc_ref[...] = lanes(alpha, acc_ref.shape[-1]) * acc_ref[...] + _dot(p.astype(BF16), v_ones)


def _init_softmax_state(m_ref, acc_ref):
    m_ref[...] = jnp.full(m_ref.shape, NEG_INF, F32)
    acc_ref[...] = jnp.zeros(acc_ref.shape, F32)


def _normalised(acc):
    return acc[:, :LANES] / acc[:, LANES:]


def _kv_tile_counts(first_visible_end, last_visible_end, tk, kv_len, sk):
    n_full = jnp.minimum(first_visible_end // tk, kv_len // tk)
    n_any = jnp.minimum((jnp.minimum(last_visible_end, kv_len) + tk - 1) // tk, sk // tk)
    return n_full, n_any


def _kv_loop(n_full, n_any, consume):
    def body(kt, carry, masked):
        consume(kt, masked)
        return carry

    lax.fori_loop(0, n_full, functools.partial(body, masked=False), 0)
    lax.fori_loop(n_full, n_any, functools.partial(body, masked=True), 0)


def _kv_loop_pipelined(n_any, produce, consume):
    produce(0, 0)
    n_pairs = (n_any - 1) // 2

    def body(i, carry):
        kt = 2 * i
        produce(1, kt + 1)
        consume(0, kt, False)
        produce(0, kt + 2)
        consume(1, kt + 1, False)
        return carry

    lax.fori_loop(0, n_pairs, body, 0)
    kt0 = 2 * n_pairs

    @pl.when(kt0 == n_any - 1)
    def _():
        consume(0, kt0, True)

    @pl.when(kt0 < n_any - 1)
    def _():
        produce(1, kt0 + 1)
        consume(0, kt0, True)
        consume(1, kt0 + 1, True)


def _mla_kernel(q_ref, k_ref, v_ref, o_ref, m_ref, acc_ref, s_ref, *, tq, tk, kv_len, q_pos0):
    sk = k_ref.shape[1]
    rows = H_MLA * tq
    q_start = q_pos0 + pl.program_id(1) * tq
    _, n_any = _kv_tile_counts((q_start // CHUNK + 1) * CHUNK,
                               ((q_start + tq - 1) // CHUNK + 1) * CHUNK, tk, kv_len, sk)
    q = q_ref[...].reshape(rows, q_ref.shape[-1])
    q_chunk = (q_start + lax.broadcasted_iota(jnp.int32, (rows, 1), 0) % tq) // CHUNK

    def produce(slot, kt):
        s_ref[slot] = _dot_nt(q, k_ref[0, pl.ds(pl.multiple_of(kt * tk, tk), tk), :])

    def consume(slot, kt, masked):
        k0 = pl.multiple_of(kt * tk, tk)
        s = s_ref[slot]
        if masked:
            k_pos = k0 + lax.broadcasted_iota(jnp.int32, (1, tk), 1)
            mask = k_pos // CHUNK <= q_chunk
            if kv_len < sk:
                mask = jnp.logical_and(mask, k_pos < kv_len)
            s = jnp.where(mask, s, NEG_INF)
        _online_softmax_step(s, v_ref[0, pl.ds(k0, tk), :], m_ref, acc_ref)

    _init_softmax_state(m_ref, acc_ref)
    _kv_loop_pipelined(n_any, produce, consume)
    o = _normalised(acc_ref[...]).astype(BF16)
    for h in range(H_MLA):
        o_ref[:, h * KV_LORA:(h + 1) * KV_LORA] = o[h * tq:(h + 1) * tq, :]


def _mla_attention(qcat, kcat, latb, batch, sq, tq, tk, kv_len, q_pos0):
    sk = kcat.shape[1]
    nq = sq // tq
    assert tk % tq == 0 and q_pos0 % tq == 0 and tq % CHUNK in (0, tq)
    kern = functools.partial(_mla_kernel, tq=tq, tk=tk, kv_len=kv_len, q_pos0=q_pos0)
    return pl.pallas_call(
        kern,
        grid=(batch, nq),
        in_specs=[pl.BlockSpec((H_MLA, tq, 2 * LANES), lambda b, i: (0, b * nq + i, 0)),
                  pl.BlockSpec((1, sk, 2 * LANES), lambda b, i: (b, 0, 0)),
                  pl.BlockSpec((1, sk, 2 * LANES), lambda b, i: (b, 0, 0))],
        out_specs=pl.BlockSpec((tq, H_MLA * KV_LORA), lambda b, i: (b * nq + i, 0)),
        out_shape=jax.ShapeDtypeStruct((batch * sq, H_MLA * KV_LORA), BF16),
        scratch_shapes=[pltpu.VMEM((H_MLA * tq, LANES), F32),
                        pltpu.VMEM((H_MLA * tq, 2 * LANES), F32),
                        pltpu.VMEM((2, H_MLA * tq, tk), F32)],
        compiler_params=_params(2),
    )(qcat, kcat, latb)


def _fox_kernel(q_ref, k_ref, v_ref, cq_ref, ck_ref, o_ref, m_ref, acc_ref, cqb_ref, *, tq, tk,
                kv_len, q_pos0):
    sk = k_ref.shape[1]
    q_start = q_pos0 + pl.program_id(1) * tq
    n_full, n_any = _kv_tile_counts(q_start + 1, q_start + tq, tk, kv_len, sk)
    lane = lax.broadcasted_iota(jnp.int32, (tq, LANES), 1)
    n_pairs = H_FOX * HD_FOX // LANES
    q_pos = q_start + lax.broadcasted_iota(jnp.int32, (2 * tq, 1), 0) % tq
    q2 = []
    for j in range(n_pairs):
        q_pair = q_ref[:, j * LANES:(j + 1) * LANES]
        zero = jnp.zeros_like(q_pair)
        q2.append(jnp.concatenate([jnp.where(lane < HD_FOX, q_pair, zero),
                                   jnp.where(lane < HD_FOX, zero, q_pair)], axis=0))
        for half in range(2):
            col = jnp.sum(jnp.where(lane == 2 * j + half, cq_ref[0], 0.0), axis=-1, keepdims=True)
            cqb_ref[j, half * tq:(half + 1) * tq, :] = jnp.broadcast_to(col, (tq, LANES))

    def consume(kt, masked):
        k0 = pl.multiple_of(kt * tk, tk)
        if masked:
            k_pos = k0 + lax.broadcasted_iota(jnp.int32, (1, tk), 1)
            mask = k_pos <= q_pos
            if kv_len < sk:
                mask = jnp.logical_and(mask, k_pos < kv_len)
        for j in range(n_pairs):
            s = _dot_nt(q2[j], k_ref[0, pl.ds(k0, tk), j * LANES:(j + 1) * LANES])
            c_k = jnp.concatenate(
                [jnp.broadcast_to(ck_ref[0, 2 * j:2 * j + 1, pl.ds(k0, tk)], (tq, tk)),
                 jnp.broadcast_to(ck_ref[0, 2 * j + 1:2 * j + 2, pl.ds(k0, tk)], (tq, tk))], axis=0)
            s = s - c_k
            if masked:
                s = jnp.where(mask, s, NEG_INF)
            v_ones = v_ref[0, pl.ds(k0, tk), 2 * j * LANES:2 * (j + 1) * LANES]
            _online_softmax_step(s, v_ones, m_ref.at[j], acc_ref.at[j], shift=cqb_ref[j])

    _init_softmax_state(m_ref, acc_ref)
    _kv_loop(n_full, n_any, consume)
    for j in range(n_pairs):
        o = _normalised(acc_ref[j])
        o_ref[:, j * LANES:(j + 1) * LANES] = jnp.where(lane < HD_FOX, o[:tq], o[tq:]).astype(BF16)


def _fox_attention(fq, fkb, fvb, c_q, c_k, batch, sq, tq, tk, kv_len, q_pos0):
    sk = fkb.shape[1]
    nq = sq // tq
    w = H_FOX * HD_FOX
    kern = functools.partial(_fox_kernel, tq=tq, tk=tk, kv_len=kv_len, q_pos0=q_pos0)
    return pl.pallas_call(
        kern,
        grid=(batch, nq),
        in_specs=[pl.BlockSpec((tq, w), lambda b, i: (b * nq + i, 0)),
                  pl.BlockSpec((1, sk, w), lambda b, i: (b, 0, 0)),
                  pl.BlockSpec((1, sk, 2 * w), lambda b, i: (b, 0, 0)),
                  pl.BlockSpec((1, tq, LANES), lambda b, i: (b, i, 0)),
                  pl.BlockSpec((1, H_FOX, sk), lambda b, i: (b, 0, 0))],
        out_specs=pl.BlockSpec((tq, w), lambda b, i: (b * nq + i, 0)),
        out_shape=jax.ShapeDtypeStruct((batch * sq, w), BF16),
        scratch_shapes=[pltpu.VMEM((w // LANES, 2 * tq, LANES), F32),
                        pltpu.VMEM((w // LANES, 2 * tq, 2 * LANES), F32),
                        pltpu.VMEM((w // LANES, 2 * tq, LANES), F32)],
        compiler_params=_params(2),
    )(fq, fkb, fvb, c_q, c_k)


def _mem_kernel(q_ref, k_ref, v_ref, o_ref):
    for h in range(H_MEM):
        cols = slice(h * HD_MEM, (h + 1) * HD_MEM)
        s = _dot_nt(q_ref[:, cols], k_ref[0, :, cols]) * (MEM_SCALE * LOG2E)
        m = jnp.max(s, axis=-1, keepdims=True)
        p = jnp.exp2(s - m)
        l = jnp.sum(p, axis=-1, keepdims=True)
        o = _dot(p.astype(BF16), v_ref[0, :, cols]) / l
        o_ref[:, cols] = o.astype(BF16)


def _mem_attention(mq, mkb, mvb, batch, sq, tq):
    nq = sq // tq
    n_mem = mkb.shape[1]
    w = H_MEM * HD_MEM
    return pl.pallas_call(
        _mem_kernel,
        grid=(batch, nq),
        in_specs=[pl.BlockSpec((tq, w), lambda b, i: (b * nq + i, 0)),
                  pl.BlockSpec((1, n_mem, w), lambda b, i: (b, 0, 0)),
                  pl.BlockSpec((1, n_mem, w), lambda b, i: (b, 0, 0))],
        out_specs=pl.BlockSpec((tq, w), lambda b, i: (b * nq + i, 0)),
        out_shape=jax.ShapeDtypeStruct((batch * sq, w), BF16),
        compiler_params=_params(2),
    )(mq, mkb, mvb)


def _route(x1b, wr_t, bias):
    tm = x1b.shape[0]
    scores = _sigmoid(_dot_nt(wr_t, x1b))
    sel = scores + bias
    sub = lax.broadcasted_iota(jnp.int32, (GROUP_SIZE, tm), 0)
    gs = []
    for g in range(N_GROUPS):
        blk = sel[g * GROUP_SIZE:(g + 1) * GROUP_SIZE, :]
        m1 = jnp.max(blk, axis=0, keepdims=True)
        first = jnp.min(jnp.where(blk == m1, sub, GROUP_SIZE), axis=0, keepdims=True)
        m2 = jnp.max(jnp.where(sub == first, -jnp.inf, blk), axis=0, keepdims=True)
        gs.append(m1 + m2)
    e_idx = lax.broadcasted_iota(jnp.int32, (N_EXPERTS, tm), 0)
    allowed = jnp.zeros((N_EXPERTS, tm), jnp.bool_)
    for g in range(N_GROUPS):
        rank = jnp.zeros((1, tm), jnp.int32)
        for o in range(N_GROUPS):
            if o == g:
                continue
            beats = (gs[o] >= gs[g]) if o < g else (gs[o] > gs[g])
            rank = rank + beats.astype(jnp.int32)
        keep = rank < TOPK_GROUPS
        allowed = jnp.logical_or(allowed, jnp.logical_and(e_idx // GROUP_SIZE == g, keep))
    cand = jnp.where(allowed, sel, -jnp.inf)
    chosen = jnp.zeros((N_EXPERTS, tm), jnp.bool_)
    for _ in range(TOP_K):
        m = jnp.max(cand, axis=0, keepdims=True)
        first = jnp.min(jnp.where(cand == m, e_idx, N_EXPERTS), axis=0, keepdims=True)
        pick = e_idx == first
        chosen = jnp.logical_or(chosen, pick)
        cand = jnp.where(pick, -jnp.inf, cand)
    w = jnp.where(chosen, scores, 0.0)
    return w / jnp.sum(w, axis=0, keepdims=True) * ROUTED_SCALE, chosen


def _swiglu(xb, wg, wu):
    a = _dot(xb, wg)
    return (a * _sigmoid(a)) * _dot(xb, wu)


def _merge_kernel(x_ref, olat_ref, ofox_ref, omem_ref, wg_ref, wuv_ref, wbm_ref, wbf_ref, wbc_ref,
                  wo_ref, g1_ref, b1_ref, wsg_ref, wsu_ref, wsd_ref, x1b_ref, base_ref):
    x = x_ref[...]
    xb = x.astype(BF16)
    o_mla = jnp.concatenate(
        [_dot(olat_ref[:, j * 2 * KV_LORA:(j + 1) * 2 * KV_LORA], wuv_ref[j]).astype(BF16)
         for j in range(H_MLA // 2)], axis=-1)

    def gate(n):
        return _sigmoid(_dot(xb, wg_ref[:, n * D_MODEL:(n + 1) * D_MODEL]))

    y = gate(0) * _dot(o_mla, wbm_ref[...])
    y = y + gate(1) * _dot(ofox_ref[...], wbf_ref[...])
    y = y + gate(2) * _dot(omem_ref[...], wbc_ref[...])
    mix = _dot(y.astype(BF16), wo_ref[...])
    x1 = _layer_norm(ALPHA * x + mix, g1_ref[...], b1_ref[...])
    x1b = x1.astype(BF16)
    x1b_ref[...] = x1b
    shared = _dot(_swiglu(x1b, wsg_ref[...], wsu_ref[...]).astype(BF16), wsd_ref[...])
    base_ref[...] = ALPHA * x1 + shared


def _merge(x, olat, ofox, omem, wg, wuv, wbm, wbf, wbc, wo, g1, b1, wsg, wsu, wsd, tm):
    t = x.shape[0]
    row = lambda n: pl.BlockSpec((tm, n), lambda i: (i, 0))
    ws = (wg, wuv, wbm, wbf, wbc, wo, g1, b1, wsg, wsu, wsd)
    return pl.pallas_call(
        _merge_kernel,
        grid=(t // tm,),
        in_specs=[row(D_MODEL), row(H_MLA * KV_LORA), row(H_FOX * HD_FOX), row(H_MEM * HD_MEM)]
        + [_full(w.shape) for w in ws],
        out_specs=(row(D_MODEL), row(D_MODEL)),
        out_shape=(jax.ShapeDtypeStruct((t, D_MODEL), BF16), jax.ShapeDtypeStruct((t, D_MODEL), F32)),
        compiler_params=_params(1),
    )(x, olat, ofox, omem, *ws)


BF16_ROWS = 16
PLACE_ROWS = 512
EXPERT_ROWS = 4096
EXPERT_CHUNK = 512
EXPERTS_PER_STEP = 4


def _capacity(tm):
    return min(tm, _round_up(2 * tm * TOP_K // N_EXPERTS, BF16_ROWS))


def _dispatch_kernel(x1b_ref, wr_ref, rb_ref, xs_ref, gate_ref, slot_ref, govf_ref, flag_ref,
                     pm_ref, *, tm, cap):
    n_tiles = x1b_ref.shape[0] // tm
    slots = []
    for h in range(n_tiles):
        rows = slice(h * tm, (h + 1) * tm)
        xb = x1b_ref[rows, :]
        gate, chosen = _route(xb, wr_ref[...], rb_ref[...])
        r = lax.broadcasted_iota(jnp.int32, (tm, tm), 0)
        c = lax.broadcasted_iota(jnp.int32, (tm, tm), 1)
        earlier = jnp.where(r < c, 1.0, 0.0).astype(BF16)
        rank = _dot(jnp.where(chosen, 1.0, 0.0).astype(BF16), earlier)
        fits = jnp.logical_and(chosen, rank < cap)
        slot = jnp.where(fits, rank, -1.0)
        gate_ref[:, rows] = jnp.where(fits, gate, 0.0)
        slot_ref[:, rows] = slot
        g_ovf = jnp.where(fits, 0.0, gate)
        govf_ref[rows, :] = jnp.concatenate(
            [g_ovf, jnp.zeros((LANES - N_EXPERTS, tm), F32)], axis=0).T
        any_ovf = jnp.max(jnp.max(g_ovf, axis=0, keepdims=True), axis=1, keepdims=True)
        flag_ref[h] = jnp.broadcast_to(jnp.where(any_ovf > 0.0, 1, 0),
                                       flag_ref.shape[1:]).astype(jnp.int32)
        slots.append(slot)
    s_iota = lax.broadcasted_iota(jnp.int32, (cap, tm), 0).astype(F32)
    for h in range(n_tiles):
        slot = slots[h]
        xb = x1b_ref[h * tm:(h + 1) * tm, :]
        for e in range(N_EXPERTS):
            pm_ref[h, e * cap:(e + 1) * cap, :] = jnp.where(
                s_iota == slot[e:e + 1, :], 1.0, 0.0).astype(BF16)
        per = PLACE_ROWS // cap
        for g in range(N_EXPERTS // per):
            xs = _dot(pm_ref[h, g * PLACE_ROWS:(g + 1) * PLACE_ROWS, :], xb).astype(BF16)
            xs_ref[g * per:(g + 1) * per, h * cap:(h + 1) * cap, :] = xs.reshape(per, cap, D_MODEL)


def _dispatch(x1b, wr_t, rb, tm, cap):
    t = x1b.shape[0]
    nt = t // tm
    ts = 2 if nt % 2 == 0 else 1
    col = pl.BlockSpec((N_EXPERTS, ts * tm), lambda i: (0, i))
    return pl.pallas_call(
        functools.partial(_dispatch_kernel, tm=tm, cap=cap),
        grid=(nt // ts,),
        in_specs=[pl.BlockSpec((ts * tm, D_MODEL), lambda i: (i, 0)), _full(wr_t.shape),
                  _full(rb.shape)],
        out_specs=(pl.BlockSpec((N_EXPERTS, ts * cap, D_MODEL), lambda i: (0, i, 0)), col, col,
                   pl.BlockSpec((ts * tm, LANES), lambda i: (i, 0)),
                   pl.BlockSpec((ts, 8, LANES), lambda i: (i, 0, 0))),
        out_shape=(jax.ShapeDtypeStruct((N_EXPERTS, nt * cap, D_MODEL), BF16),
                   jax.ShapeDtypeStruct((N_EXPERTS, t), F32),
                   jax.ShapeDtypeStruct((N_EXPERTS, t), F32),
                   jax.ShapeDtypeStruct((t, LANES), F32),
                   jax.ShapeDtypeStruct((nt, 8, LANES), jnp.int32)),
        scratch_shapes=[pltpu.VMEM((ts, N_EXPERTS * cap, tm), BF16)],
        compiler_params=_params(1),
    )(x1b, wr_t, rb)


def _expert_kernel(xs_ref, wg_ref, wu_ref, wd_ref, ys_ref, wgb_ref, wub_ref, wdb_ref, *, chunk):
    @pl.when(pl.program_id(1) == 0)
    def _():
        wgb_ref[...] = wg_ref[0].astype(BF16)
        wub_ref[...] = wu_ref[0].astype(BF16)
        wdb_ref[...] = wd_ref[0].astype(BF16)

    def body(c, carry):
        r0 = pl.multiple_of(c * chunk, chunk)
        hid = _swiglu(xs_ref[0, pl.ds(r0, chunk), :], wgb_ref[...], wub_ref[...])
        ys_ref[0, pl.ds(r0, chunk), :] = _dot(hid.astype(BF16), wdb_ref[...]).astype(BF16)
        return carry

    lax.fori_loop(0, xs_ref.shape[1] // chunk, body, 0)


def _experts(xs, weg, weu, wed):
    n_e, rows, _ = xs.shape
    rb = min(EXPERT_ROWS, rows)
    blk = pl.BlockSpec((1, rb, D_MODEL), lambda e, i: (e, i, 0))
    return pl.pallas_call(
        functools.partial(_expert_kernel, chunk=min(EXPERT_CHUNK, rb)),
        grid=(n_e, rows // rb),
        in_specs=[blk,
                  pl.BlockSpec((1, D_MODEL, D_EXPERT), lambda e, i: (e, 0, 0)),
                  pl.BlockSpec((1, D_MODEL, D_EXPERT), lambda e, i: (e, 0, 0)),
                  pl.BlockSpec((1, D_EXPERT, D_MODEL), lambda e, i: (e, 0, 0))],
        out_specs=blk,
        out_shape=jax.ShapeDtypeStruct(xs.shape, BF16),
        scratch_shapes=[pltpu.VMEM((D_MODEL, D_EXPERT), BF16), pltpu.VMEM((D_MODEL, D_EXPERT), BF16),
                        pltpu.VMEM((D_EXPERT, D_MODEL), BF16)],
        compiler_params=_params(2),
    )(xs, weg, weu, wed)


def _dense_routed_kernel(flag_ref, x1b_ref, gate_ref, weg_ref, weu_ref, wed_ref, out_ref):
    step = pl.program_id(1)

    @pl.when(step == 0)
    def _():
        out_ref[...] = jnp.zeros_like(out_ref)

    @pl.when(flag_ref[pl.program_id(0)] > 0)
    def _():
        xb = x1b_ref[...]
        gate = gate_ref[...]
        lane = lax.broadcasted_iota(jnp.int32, gate.shape, 1)
        for j in range(EXPERTS_PER_STEP):
            e = step * EXPERTS_PER_STEP + j
            g_col = jnp.sum(jnp.where(lane == e, gate, 0.0), axis=-1, keepdims=True)
            hid = _swiglu(xb, weg_ref[j], weu_ref[j])
            out_ref[...] += g_col * _dot(hid.astype(BF16), wed_ref[j])


def _dense_routed(flags, x1b, gate, weg, weu, wed, tm):
    t = x1b.shape[0]
    eb = EXPERTS_PER_STEP
    row = lambda n: pl.BlockSpec((tm, n), lambda i, s, f: (i, 0))
    wspec = lambda a, b: pl.BlockSpec((eb, a, b), lambda i, s, f: (jnp.where(f[i] > 0, s, 0), 0, 0))
    return pl.pallas_call(
        _dense_routed_kernel,
        grid_spec=pltpu.PrefetchScalarGridSpec(
            num_scalar_prefetch=1,
            grid=(t // tm, N_EXPERTS // eb),
            in_specs=[row(D_MODEL), row(LANES), wspec(D_MODEL, D_EXPERT), wspec(D_MODEL, D_EXPERT),
                      wspec(D_EXPERT, D_MODEL)],
            out_specs=row(D_MODEL)),
        out_shape=jax.ShapeDtypeStruct((t, D_MODEL), F32),
        compiler_params=_params(2),
    )(flags, x1b, gate, weg, weu, wed)


def _combine_kernel(gate_ref, slot_ref, ys_ref, base_ref, *rest, cap, has_overflow):
    if has_overflow:
        rovf_ref, g2_ref, b2_ref, y_ref, pw_ref = rest
    else:
        g2_ref, b2_ref, y_ref, pw_ref = rest
    tm = base_ref.shape[0]
    gate = gate_ref[...]
    slot = slot_ref[...]
    s_iota = lax.broadcasted_iota(jnp.int32, (cap, tm), 0).astype(F32)
    for e in range(N_EXPERTS):
        pw_ref[e * cap:(e + 1) * cap, :] = jnp.where(
            s_iota == slot[e:e + 1, :], gate[e:e + 1, :], 0.0).astype(BF16)
    routed = rovf_ref[...] if has_overflow else jnp.zeros((tm, D_MODEL), F32)
    per = PLACE_ROWS // cap
    for g in range(N_EXPERTS // per):
        ys = ys_ref[g * per:(g + 1) * per].reshape(PLACE_ROWS, D_MODEL)
        routed = routed + lax.dot_general(pw_ref[g * PLACE_ROWS:(g + 1) * PLACE_ROWS, :], ys,
                                          (((0,), (0,)), ((), ())), preferred_element_type=F32)
    y_ref[...] = _layer_norm(base_ref[...] + routed, g2_ref[...], b2_ref[...])


def _combine(gate_t, slot_t, ys, base, rovf, g2, b2, tm, cap):
    t = base.shape[0]
    col = pl.BlockSpec((N_EXPERTS, tm), lambda i: (0, i))
    row = pl.BlockSpec((tm, D_MODEL), lambda i: (i, 0))
    has_overflow = rovf is not None
    operands = (gate_t, slot_t, ys, base) + ((rovf,) if has_overflow else ()) + (g2, b2)
    return pl.pallas_call(
        functools.partial(_combine_kernel, cap=cap, has_overflow=has_overflow),
        grid=(t // tm,),
        in_specs=[col, col, pl.BlockSpec((N_EXPERTS, cap, D_MODEL), lambda i: (0, i, 0)), row]
        + ([row] if has_overflow else []) + [_full(g2.shape), _full(b2.shape)],
        out_specs=row,
        out_shape=jax.ShapeDtypeStruct((t, D_MODEL), F32),
        scratch_shapes=[pltpu.VMEM((N_EXPERTS * cap, tm), BF16)],
        compiler_params=_params(1),
    )(*operands)


def _moe_ln2(x1b, base, wr_t, rb, weg, weu, wed, g2, b2, tm):
    t = x1b.shape[0]
    cap = _capacity(tm)
    assert PLACE_ROWS % cap == 0 and (N_EXPERTS * cap) % PLACE_ROWS == 0
    xs, gate_t, slot_t, gate_ovf, flags = _dispatch(x1b, wr_t, rb, tm, cap)
    ys = _experts(xs, weg, weu, wed)
    flags = flags[:, 0, 0]

    def with_overflow():
        rovf = _dense_routed(flags, x1b, gate_ovf, weg.astype(BF16), weu.astype(BF16),
                             wed.astype(BF16), tm)
        return _combine(gate_t, slot_t, ys, base, rovf, g2, b2, tm, cap)

    return lax.cond(jnp.any(flags > 0), with_overflow,
                    lambda: _combine(gate_t, slot_t, ys, base, None, g2, b2, tm, cap))


def _rot_half_cols(w):
    half = w.shape[-1] // 2
    return jnp.concatenate([-w[..., half:], w[..., :half]], axis=-1)


def _block_diag2(a, b):
    za = jnp.zeros((a.shape[0], b.shape[1]), a.dtype)
    zb = jnp.zeros((b.shape[0], a.shape[1]), a.dtype)
    return jnp.concatenate([jnp.concatenate([a, za], 1), jnp.concatenate([zb, b], 1)], 0)


def _prep_weights(w_in, b_f, q_norm_g, w_q_up, kv_norm_g, w_kv_up):
    widths = (Q_LORA, KV_LORA, ROPE_DIM, H_FOX * HD_FOX, H_FOX * HD_FOX, H_FOX * HD_FOX, H_FOX,
              H_MEM * HD_MEM, N_BRANCH * D_MODEL)
    offs = np.cumsum((0,) + widths)
    w_ql, w_ckv, w_kr, w_fq, w_fk, w_fv, w_fl, w_mq, w_g = (
        w_in[:, offs[i]:offs[i + 1]] for i in range(len(widths)))
    rep = LANES // ROPE_DIM
    w_main = jnp.concatenate(
        [w_ql, w_ckv, jnp.tile(w_kr, (1, rep)), jnp.tile(_rot_half_cols(w_kr), (1, rep)),
         w_fq, w_fk, w_fv, w_mq, jnp.pad(w_fl, ((0, 0), (0, LANES - H_FOX)))], axis=1).astype(BF16)
    assert w_main.shape[1] == C_END
    wq3 = w_q_up.reshape(Q_LORA, H_MLA, NOPE_DIM + ROPE_DIM)
    wq_nope = wq3[:, :, :NOPE_DIM].reshape(Q_LORA, H_MLA * NOPE_DIM)
    wq_rope = wq3[:, :, NOPE_DIM:]
    w_q = jnp.concatenate(
        [wq_nope, wq_rope.reshape(Q_LORA, -1), _rot_half_cols(wq_rope).reshape(Q_LORA, -1)],
        axis=1).astype(BF16)
    wkv3 = w_kv_up.reshape(KV_LORA, H_MLA, NOPE_DIM + V_DIM)
    w_uk = wkv3[:, :, :NOPE_DIM]
    w_uv = wkv3[:, :, NOPE_DIM:]
    w_abs = jnp.stack([_block_diag2(w_uk[:, 2 * j].T, w_uk[:, 2 * j + 1].T)
                       for j in range(H_MLA // 2)]).astype(BF16)
    w_uvp = jnp.stack([_block_diag2(w_uv[:, 2 * j], w_uv[:, 2 * j + 1])
                       for j in range(H_MLA // 2)]).astype(BF16)
    b_fp = jnp.pad(b_f, (0, LANES - H_FOX)).reshape(1, LANES)
    return (w_main, w_q, w_abs, q_norm_g.reshape(1, -1), kv_norm_g.reshape(1, -1), b_fp,
            w_g.astype(BF16), w_uvp)


def _rope_tables(pos):
    half = ROPE_DIM // 2
    inv_freq = ROPE_THETA ** (-jnp.arange(half, dtype=F32) / half)
    ang = pos.astype(F32)[:, None] * inv_freq[None, :]
    rep = 2 * LANES // ROPE_DIM
    cos = jnp.tile(jnp.concatenate([jnp.cos(ang)] * 2, axis=-1), (1, rep))
    sin = jnp.tile(jnp.concatenate([jnp.sin(ang)] * 2, axis=-1), (1, rep))
    return cos, sin


def _round_up(n, m):
    return (n + m - 1) // m * m


def kernel(x_prompt, x_sample, mem_prompt, cache_mla_latent, cache_mla_krope, cache_fox_k,
           cache_fox_v, cache_fox_logf, cache_mem_k, cache_mem_v, ln1_g, ln1_b, w_in, b_f,
           q_norm_g, w_q_up, kv_norm_g, w_kv_up, w_mem_k, w_mem_v, w_br_mla, w_br_fox,
           w_br_mem, w_out, ln2_g, ln2_b, w_router, router_bias, w_e_gate, w_e_up, w_e_down,
           w_s_gate, w_s_up, w_s_down):
    assert w_in.shape[0] == DEPTH == 1
    batch, seq, _ = x_prompt.shape
    dec_batch, dec_seq, _ = x_sample.shape
    past = cache_mla_latent.shape[2]
    n_mem = mem_prompt.shape[1]
    l = 0

    (w_main, w_q, w_abs, q_g, kv_g, b_fp, w_gate, w_uvp) = _prep_weights(
        w_in[l], b_f[l], q_norm_g[l], w_q_up[l], kv_norm_g[l], w_kv_up[l])
    merge_w = (w_gate, w_uvp, w_br_mla[l].astype(BF16), w_br_fox[l].astype(BF16),
               w_br_mem[l].astype(BF16), w_out[l].astype(BF16), ln1_g[l].reshape(1, -1),
               ln1_b[l].reshape(1, -1), w_s_gate[l].astype(BF16), w_s_up[l].astype(BF16),
               w_s_down[l].astype(BF16))
    moe_w = (w_router[l].T.astype(BF16), router_bias[l].reshape(-1, 1),
             w_e_gate[l], w_e_up[l], w_e_down[l],
             ln2_g[l].reshape(1, -1), ln2_b[l].reshape(1, -1))

    def tail(x, olat, ofox, omem, tm, tm_moe):
        x1b, base = _merge(x, olat, ofox, omem, *merge_w, tm)
        return _moe_ln2(x1b, base, *moe_w, tm_moe)

    tp = batch * seq
    xp = x_prompt.reshape(tp, D_MODEL)
    tm_p = min(512, seq)
    tq_p = min(256, seq)
    tk_p = min(512, seq)
    cos_p, sin_p = _rope_tables(jnp.arange(seq, dtype=jnp.int32))
    (qcat, kcat, lat, latb, kr, fq, fk, fv, fkb, fvb, mq, logf, logfp) = _proj(
        xp, w_main, w_q, w_abs, q_g, kv_g, b_fp, cos_p, sin_p, tm_p)
    tmem = batch * n_mem
    m_k, m_v, m_kb, m_vb = _memkv(mem_prompt.reshape(tmem, D_MODEL), w_mem_k[l].astype(BF16),
                                  w_mem_v[l].astype(BF16), min(512, tmem))
    c_p = _cumsum(logfp.reshape(batch, seq, LANES))
    c_keys = jnp.swapaxes(c_p[:, :, :H_FOX], 1, 2)
    olat = _mla_attention(qcat, kcat.reshape(batch, seq, -1), latb.reshape(batch, seq, -1),
                          batch, seq, tq_p, tk_p, seq, 0)
    ofox = _fox_attention(fq, fkb.reshape(batch, seq, -1), fvb.reshape(batch, seq, -1), c_p, c_keys,
                          batch, seq, tk_p, tk_p, seq, 0)
    omem = _mem_attention(mq, m_kb.reshape(batch, n_mem, -1), m_vb.reshape(batch, n_mem, -1),
                          batch, seq, tk_p)
    y_p = tail(xp, olat, ofox, omem, tm_p, min(256, seq)).reshape(batch, seq, D_MODEL)

    ts = dec_batch * dec_seq
    xs = x_sample.reshape(ts, D_MODEL)
    cos_s, sin_s = _rope_tables(jnp.tile(past + jnp.arange(dec_seq, dtype=jnp.int32), dec_batch))
    (qcat_s, kcat_s, lat_s, latb_s, kr_s, fq_s, fk_s, fv_s, fkb_s, fvb_s, mq_s, logf_s,
     logfp_s) = _proj(xs, w_main, w_q, w_abs, q_g, kv_g, b_fp, cos_s, sin_s, ts)
    kv_len = past + dec_seq
    tk_s = 1024
    sk = _round_up(kv_len, tk_s)

    def with_cache(cache, new):
        new = new.reshape(dec_batch, dec_seq, -1)
        pad = jnp.zeros((dec_batch, sk - kv_len, new.shape[-1]), new.dtype)
        return jnp.concatenate([cache.astype(new.dtype), new, pad], axis=1)

    rep = LANES // ROPE_DIM
    kcat_all = with_cache(
        jnp.concatenate([cache_mla_latent[l], jnp.tile(cache_mla_krope[l], (1, 1, rep))], axis=-1),
        kcat_s)
    lat_all = with_cache(
        jnp.concatenate([cache_mla_latent[l], jnp.ones((dec_batch, past, LANES), F32)], axis=-1), latb_s)
    fk_all = with_cache(cache_fox_k[l].reshape(dec_batch, past, -1), fkb_s)
    n_pairs = H_FOX * HD_FOX // LANES
    fv_pairs = cache_fox_v[l].reshape(dec_batch, past, n_pairs, LANES)
    fv_all = with_cache(
        jnp.concatenate([fv_pairs, jnp.ones_like(fv_pairs)], axis=-1).reshape(dec_batch, past, -1),
        fvb_s)
    logf_all = with_cache(jnp.pad(cache_fox_logf[l], ((0, 0), (0, 0), (0, LANES - H_FOX))), logfp_s)
    c_s = _cumsum(logf_all)
    c_keys_s = jnp.swapaxes(c_s[:, :, :H_FOX], 1, 2)
    c_q_s = c_s[:, past:kv_len]
    olat_s = _mla_attention(qcat_s, kcat_all, lat_all, dec_batch, dec_seq, dec_seq, tk_s, kv_len, past)
    ofox_s = _fox_attention(fq_s, fk_all, fv_all, c_q_s, c_keys_s, dec_batch, dec_seq, dec_seq, tk_s,
                            kv_len, past)
    omem_s = _mem_attention(mq_s, cache_mem_k[l].reshape(dec_batch, n_mem, -1).astype(BF16),
                            cache_mem_v[l].reshape(dec_batch, n_mem, -1).astype(BF16),
                            dec_batch, dec_seq, dec_seq)
    y_s = tail(xs, olat_s, ofox_s, omem_s, ts, ts).reshape(dec_batch, dec_seq, D_MODEL)

    def stack(a, b, s, *tail_shape):
        return a.reshape(1, b, s, *tail_shape)

    return (y_p, y_s,
            stack(lat, batch, seq, KV_LORA), stack(kr, batch, seq, ROPE_DIM),
            stack(fk, batch, seq, H_FOX, HD_FOX), stack(fv, batch, seq, H_FOX, HD_FOX),
            stack(logf, batch, seq, H_FOX),
            stack(m_k, batch, n_mem, H_MEM, HD_MEM), stack(m_v, batch, n_mem, H_MEM, HD_MEM),
            stack(lat_s, dec_batch, dec_seq, KV_LORA), stack(kr_s, dec_batch, dec_seq, ROPE_DIM),
            stack(fk_s, dec_batch, dec_seq, H_FOX, HD_FOX), stack(fv_s, dec_batch, dec_seq, H_FOX, HD_FOX),
            stack(logf_s, dec_batch, dec_seq, H_FOX))
```

```python
import functools

import jax
import jax.numpy as jnp
import numpy as np
from jax import lax
from jax.experimental import pallas as pl
from jax.experimental.pallas import tpu as pltpu

F32 = jnp.float32
BF16 = jnp.bfloat16

D_MODEL = 1024
CHUNK = 64
H_MLA = 8
Q_LORA = 256
KV_LORA = 128
NOPE_DIM = 64
ROPE_DIM = 32
V_DIM = 64
ROPE_THETA = 10000.0
MLA_SCALE = (NOPE_DIM + ROPE_DIM) ** -0.5
H_FOX = 8
HD_FOX = 64
FOX_SCALE = HD_FOX ** -0.5
H_MEM = 4
HD_MEM = 128
MEM_SCALE = HD_MEM ** -0.5
N_BRANCH = 3
N_EXPERTS = 64
TOP_K = 8
N_GROUPS = 8
GROUP_SIZE = N_EXPERTS // N_GROUPS
TOPK_GROUPS = 4
D_EXPERT = 256
ROUTED_SCALE = 2.5
DEPTH = 1
ALPHA = (2 * DEPTH) ** 0.25
NORM_EPS = 1e-5
NEG_INF = -1e30

LOG2E = 1.4426950408889634
LANES = 128
VMEM_LIMIT = 56 * 1024 * 1024

C_QLAT = 0
C_CKV = C_QLAT + Q_LORA
C_KR = C_CKV + KV_LORA
C_KRR = C_KR + LANES
C_FQ = C_KRR + LANES
C_FK = C_FQ + H_FOX * HD_FOX
C_FV = C_FK + H_FOX * HD_FOX
C_MQ = C_FV + H_FOX * HD_FOX
C_FL = C_MQ + H_MEM * HD_MEM
C_END = C_FL + LANES


def _params(n_grid_axes):
    return pltpu.CompilerParams(
        dimension_semantics=("arbitrary",) * n_grid_axes, vmem_limit_bytes=VMEM_LIMIT)


def _full(shape):
    return pl.BlockSpec(shape, lambda *_: (0,) * len(shape))


def _dot(a, b):
    return jnp.dot(a, b, preferred_element_type=F32)


def _dot_nt(a, b):
    return lax.dot_general(a, b, (((1,), (1,)), ((), ())), preferred_element_type=F32)


def _sigmoid(x):
    return 1.0 / (1.0 + jnp.exp(-x))


def _rms(x, g):
    ms = jnp.mean(x * x, axis=-1, keepdims=True)
    return x * lax.rsqrt(ms + NORM_EPS) * g


def _layer_norm(x, g, b):
    mu = jnp.mean(x, axis=-1, keepdims=True)
    xc = x - mu
    var = jnp.mean(xc * xc, axis=-1, keepdims=True)
    return xc * lax.rsqrt(var + NORM_EPS) * g + b


def _proj_kernel(x_ref, w_ref, wq_ref, wabs_ref, qg_ref, kvg_ref, bf_ref, cos_ref, sin_ref,
                 qcat_ref, kcat_ref, lat_ref, latb_ref, kr_ref, fq_ref, fk_ref, fv_ref,
                 fkb_ref, fvb_ref, mq_ref, logf_ref, logfp_ref):
    xb = x_ref[...].astype(BF16)
    tm = xb.shape[0]

    def proj(c0, width):
        return _dot(xb, w_ref[:, c0:c0 + width])

    cos = cos_ref[...]
    sin = sin_ref[...]
    qn = _rms(proj(C_QLAT, Q_LORA), qg_ref[...]).astype(BF16)
    q2 = _dot(qn, wq_ref[...])
    nq = H_MLA * NOPE_DIM
    nr = H_MLA * ROPE_DIM
    q_rope = (q2[:, nq:nq + nr] * cos + q2[:, nq + nr:nq + 2 * nr] * sin) * (MLA_SCALE * LOG2E)
    lane = lax.broadcasted_iota(jnp.int32, (tm, LANES), 1)
    heads_per_group = LANES // ROPE_DIM
    for j in range(H_MLA // 2):
        q_abs = _dot(q2[:, j * LANES:(j + 1) * LANES].astype(BF16), wabs_ref[j]) * (MLA_SCALE * LOG2E)
        for half in range(2):
            h = 2 * j + half
            grp = h // heads_per_group
            qr = q_rope[:, grp * LANES:(grp + 1) * LANES]
            qr = jnp.where(lane // ROPE_DIM == h % heads_per_group, qr, 0.0)
            qcat_ref[h, :, 0:LANES] = q_abs[:, half * LANES:(half + 1) * LANES].astype(BF16)
            qcat_ref[h, :, LANES:2 * LANES] = qr.astype(BF16)
    lat = _rms(proj(C_CKV, KV_LORA), kvg_ref[...])
    lat_ref[...] = lat
    ones = jnp.ones((tm, LANES), BF16)
    latb_ref[:, 0:KV_LORA] = lat.astype(BF16)
    latb_ref[:, KV_LORA:KV_LORA + LANES] = ones
    kr = proj(C_KR, LANES) * cos[:, :LANES] + proj(C_KRR, LANES) * sin[:, :LANES]
    kr_ref[...] = kr[:, :ROPE_DIM]
    kcat_ref[:, 0:KV_LORA] = lat.astype(BF16)
    kcat_ref[:, KV_LORA:KV_LORA + LANES] = kr.astype(BF16)
    fq_ref[...] = (proj(C_FQ, H_FOX * HD_FOX) * (FOX_SCALE * LOG2E)).astype(BF16)
    fk = proj(C_FK, H_FOX * HD_FOX)
    fk_ref[...] = fk
    fkb_ref[...] = fk.astype(BF16)
    fv = proj(C_FV, H_FOX * HD_FOX)
    fv_ref[...] = fv
    for j in range(H_FOX * HD_FOX // LANES):
        fvb_ref[:, 2 * j * LANES:(2 * j + 1) * LANES] = fv[:, j * LANES:(j + 1) * LANES].astype(BF16)
        fvb_ref[:, (2 * j + 1) * LANES:(2 * j + 2) * LANES] = ones
    mq_ref[...] = proj(C_MQ, H_MEM * HD_MEM).astype(BF16)
    z = proj(C_FL, LANES) + bf_ref[...]
    log_f = jnp.minimum(z, 0.0) - jnp.log1p(jnp.exp(-jnp.abs(z)))
    logfp_ref[...] = log_f
    logf_ref[...] = log_f[:, :H_FOX]


def _proj(x, w_main, w_q, w_abs, q_g, kv_g, b_f, cos_t, sin_t, tm):
    t = x.shape[0]
    n_tab = cos_t.shape[0] // tm
    row = lambda w: pl.BlockSpec((tm, w), lambda i: (i, 0))
    tab = pl.BlockSpec((tm, 2 * LANES), lambda i: (i % n_tab, 0))
    hw = H_FOX * HD_FOX
    out_shape = (
        jax.ShapeDtypeStruct((H_MLA, t, 2 * LANES), BF16),
        jax.ShapeDtypeStruct((t, 2 * LANES), BF16),
        jax.ShapeDtypeStruct((t, KV_LORA), F32),
        jax.ShapeDtypeStruct((t, KV_LORA + LANES), BF16),
        jax.ShapeDtypeStruct((t, ROPE_DIM), F32),
        jax.ShapeDtypeStruct((t, hw), BF16),
        jax.ShapeDtypeStruct((t, hw), F32),
        jax.ShapeDtypeStruct((t, hw), F32),
        jax.ShapeDtypeStruct((t, hw), BF16),
        jax.ShapeDtypeStruct((t, 2 * hw), BF16),
        jax.ShapeDtypeStruct((t, H_MEM * HD_MEM), BF16),
        jax.ShapeDtypeStruct((t, H_FOX), F32),
        jax.ShapeDtypeStruct((t, LANES), F32),
    )
    out_specs = (
        pl.BlockSpec((H_MLA, tm, 2 * LANES), lambda i: (0, i, 0)),
        row(2 * LANES), row(KV_LORA), row(KV_LORA + LANES), row(ROPE_DIM), row(hw), row(hw), row(hw),
        row(hw), row(2 * hw), row(H_MEM * HD_MEM), row(H_FOX), row(LANES),
    )
    return pl.pallas_call(
        _proj_kernel,
        grid=(t // tm,),
        in_specs=[row(D_MODEL), _full(w_main.shape), _full(w_q.shape), _full(w_abs.shape),
                  _full(q_g.shape), _full(kv_g.shape), _full(b_f.shape), tab, tab],
        out_specs=out_specs,
        out_shape=out_shape,
        compiler_params=_params(1),
    )(x, w_main, w_q, w_abs, q_g, kv_g, b_f, cos_t, sin_t)


def _memkv_kernel(x_ref, wk_ref, wv_ref, k_ref, v_ref, kb_ref, vb_ref):
    xb = x_ref[...].astype(BF16)
    k = _dot(xb, wk_ref[...])
    v = _dot(xb, wv_ref[...])
    k_ref[...] = k
    v_ref[...] = v
    kb_ref[...] = k.astype(BF16)
    vb_ref[...] = v.astype(BF16)


def _memkv(x, wk, wv, tm):
    t = x.shape[0]
    w = H_MEM * HD_MEM
    row = lambda n: pl.BlockSpec((tm, n), lambda i: (i, 0))
    return pl.pallas_call(
        _memkv_kernel,
        grid=(t // tm,),
        in_specs=[row(D_MODEL), _full(wk.shape), _full(wv.shape)],
        out_specs=(row(w), row(w), row(w), row(w)),
        out_shape=(jax.ShapeDtypeStruct((t, w), F32), jax.ShapeDtypeStruct((t, w), F32),
                   jax.ShapeDtypeStruct((t, w), BF16), jax.ShapeDtypeStruct((t, w), BF16)),
        compiler_params=_params(1),
    )(x, wk, wv)


CUM_BLOCK = 256


def _bf16_pieces(x):
    hi = x.astype(BF16)
    r1 = x - hi.astype(F32)
    mid = r1.astype(BF16)
    lo = (r1 - mid.astype(F32)).astype(BF16)
    return hi, mid, lo


def _cumsum_kernel(x_ref, c_ref):
    s = x_ref.shape[1]
    r = lax.broadcasted_iota(jnp.int32, (CUM_BLOCK, CUM_BLOCK), 0)
    c = lax.broadcasted_iota(jnp.int32, (CUM_BLOCK, CUM_BLOCK), 1)
    tri = jnp.where(r >= c, 1.0, 0.0).astype(BF16)

    def body(i, carry):
        r0 = pl.multiple_of(i * CUM_BLOCK, CUM_BLOCK)
        x = x_ref[0, pl.ds(r0, CUM_BLOCK), :]
        cs = sum(_dot(tri, piece) for piece in _bf16_pieces(x)) + carry
        c_ref[0, pl.ds(r0, CUM_BLOCK), :] = cs * LOG2E
        return cs[CUM_BLOCK - 1:CUM_BLOCK, :]

    lax.fori_loop(0, s // CUM_BLOCK, body, jnp.zeros((1, LANES), F32))


def _cumsum(x):
    b, s, _ = x.shape
    spec = pl.BlockSpec((1, s, LANES), lambda i: (i, 0, 0))
    return pl.pallas_call(
        _cumsum_kernel, grid=(b,), in_specs=[spec], out_specs=spec,
        out_shape=jax.ShapeDtypeStruct(x.shape, F32), compiler_params=_params(1),
    )(x)


def _online_softmax_step(s, v_ones, m_ref, acc_ref, shift=None):
    def lanes(x, width):
        return jnp.concatenate([x] * (width // LANES), axis=-1)

    m = m_ref[...]
    r = jnp.max(s, axis=-1, keepdims=True)
    m_new = jnp.maximum(m, r if shift is None else r + shift)
    alpha = jnp.exp2(m - m_new)
    p = jnp.exp2(s - lanes(m_new if shift is None else m_new - shift, s.shape[-1]))
    m_ref[...] = m_new
    acc_ref[...] = lanes(alpha, acc_ref.shape[-1]) * acc_ref[...] + _dot(p.astype(BF16), v_ones)


def _init_softmax_state(m_ref, acc_ref):
    m_ref[...] = jnp.full(m_ref.shape, NEG_INF, F32)
    acc_ref[...] = jnp.zeros(acc_ref.shape, F32)


def _normalised(acc):
    return acc[:, :LANES] / acc[:, LANES:]


def _kv_tile_counts(first_visible_end, last_visible_end, tk, kv_len, sk):
    n_full = jnp.minimum(first_visible_end // tk, kv_len // tk)
    n_any = jnp.minimum((jnp.minimum(last_visible_end, kv_len) + tk - 1) // tk, sk // tk)
    return n_full, n_any


def _kv_loop(n_full, n_any, consume):
    def body(kt, carry, masked):
        consume(kt, masked)
        return carry

    lax.fori_loop(0, n_full, functools.partial(body, masked=False), 0)
    lax.fori_loop(n_full, n_any, functools.partial(body, masked=True), 0)


def _kv_loop_pipelined(n_any, produce, consume):
    produce(0, 0)
    n_pairs = (n_any - 1) // 2

    def body(i, carry):
        kt = 2 * i
        produce(1, kt + 1)
        consume(0, kt, False)
        produce(0, kt + 2)
        consume(1, kt + 1, False)
        return carry

    lax.fori_loop(0, n_pairs, body, 0)
    kt0 = 2 * n_pairs

    @pl.when(kt0 == n_any - 1)
    def _():
        consume(0, kt0, True)

    @pl.when(kt0 < n_any - 1)
    def _():
        produce(1, kt0 + 1)
        consume(0, kt0, False)
        consume(1, kt0 + 1, True)


def _mla_kernel(q_ref, k_ref, v_ref, o_ref, m_ref, acc_ref, s_ref, *, tq, tk, kv_len, q_pos0):
    sk = k_ref.shape[1]
    rows = H_MLA * tq
    q_start = q_pos0 + pl.program_id(1) * tq
    _, n_any = _kv_tile_counts((q_start // CHUNK + 1) * CHUNK,
                               ((q_start + tq - 1) // CHUNK + 1) * CHUNK, tk, kv_len, sk)
    q = q_ref[...].reshape(rows, q_ref.shape[-1])
    q_chunk = (q_start + lax.broadcasted_iota(jnp.int32, (rows, 1), 0) % tq) // CHUNK

    def produce(slot, kt):
        s_ref[slot] = _dot_nt(q, k_ref[0, pl.ds(pl.multiple_of(kt * tk, tk), tk), :])

    def consume(slot, kt, masked):
        k0 = pl.multiple_of(kt * tk, tk)
        s = s_ref[slot]
        if masked:
            k_pos = k0 + lax.broadcasted_iota(jnp.int32, (1, tk), 1)
            mask = k_pos // CHUNK <= q_chunk
            if kv_len < sk:
                mask = jnp.logical_and(mask, k_pos < kv_len)
            s = jnp.where(mask, s, NEG_INF)
        _online_softmax_step(s, v_ref[0, pl.ds(k0, tk), :], m_ref, acc_ref)

    _init_softmax_state(m_ref, acc_ref)
    _kv_loop_pipelined(n_any, produce, consume)
    o = _normalised(acc_ref[...]).astype(BF16)
    for h in range(H_MLA):
        o_ref[:, h * KV_LORA:(h + 1) * KV_LORA] = o[h * tq:(h + 1) * tq, :]


def _mla_attention(qcat, kcat, latb, batch, sq, tq, tk, kv_len, q_pos0):
    sk = kcat.shape[1]
    nq = sq // tq
    assert tk % tq == 0 and q_pos0 % tq == 0 and tq % CHUNK in (0, tq)
    kern = functools.partial(_mla_kernel, tq=tq, tk=tk, kv_len=kv_len, q_pos0=q_pos0)
    return pl.pallas_call(
        kern,
        grid=(batch, nq),
        in_specs=[pl.BlockSpec((H_MLA, tq, 2 * LANES), lambda b, i: (0, b * nq + i, 0)),
                  pl.BlockSpec((1, sk, 2 * LANES), lambda b, i: (b, 0, 0)),
                  pl.BlockSpec((1, sk, 2 * LANES), lambda b, i: (b, 0, 0))],
        out_specs=pl.BlockSpec((tq, H_MLA * KV_LORA), lambda b, i: (b * nq + i, 0)),
        out_shape=jax.ShapeDtypeStruct((batch * sq, H_MLA * KV_LORA), BF16),
        scratch_shapes=[pltpu.VMEM((H_MLA * tq, LANES), F32),
                        pltpu.VMEM((H_MLA * tq, 2 * LANES), F32),
                        pltpu.VMEM((2, H_MLA * tq, tk), F32)],
        compiler_params=_params(2),
    )(qcat, kcat, latb)


def _fox_kernel(q_ref, k_ref, v_ref, cq_ref, ck_ref, o_ref, m_ref, acc_ref, cqb_ref, *, tq, tk,
                kv_len, q_pos0):
    sk = k_ref.shape[1]
    q_start = q_pos0 + pl.program_id(1) * tq
    n_full, n_any = _kv_tile_counts(q_start + 1, q_start + tq, tk, kv_len, sk)
    lane = lax.broadcasted_iota(jnp.int32, (tq, LANES), 1)
    n_pairs = H_FOX * HD_FOX // LANES
    q_pos = q_start + lax.broadcasted_iota(jnp.int32, (2 * tq, 1), 0) % tq
    q2 = []
    for j in range(n_pairs):
        q_pair = q_ref[:, j * LANES:(j + 1) * LANES]
        zero = jnp.zeros_like(q_pair)
        q2.append(jnp.concatenate([jnp.where(lane < HD_FOX, q_pair, zero),
                                   jnp.where(lane < HD_FOX, zero, q_pair)], axis=0))
        for half in range(2):
            col = jnp.sum(jnp.where(lane == 2 * j + half, cq_ref[0], 0.0), axis=-1, keepdims=True)
            cqb_ref[j, half * tq:(half + 1) * tq, :] = jnp.broadcast_to(col, (tq, LANES))

    def consume(kt, masked):
        k0 = pl.multiple_of(kt * tk, tk)
        if masked:
            k_pos = k0 + lax.broadcasted_iota(jnp.int32, (1, tk), 1)
            mask = k_pos <= q_pos
            if kv_len < sk:
                mask = jnp.logical_and(mask, k_pos < kv_len)
        for j in range(n_pairs):
            s = _dot_nt(q2[j], k_ref[0, pl.ds(k0, tk), j * LANES:(j + 1) * LANES])
            c_k = jnp.concatenate(
                [jnp.broadcast_to(ck_ref[0, 2 * j:2 * j + 1, pl.ds(k0, tk)], (tq, tk)),
                 jnp.broadcast_to(ck_ref[0, 2 * j + 1:2 * j + 2, pl.ds(k0, tk)], (tq, tk))], axis=0)
            s = s - c_k
            if masked:
                s = jnp.where(mask, s, NEG_INF)
            v_ones = v_ref[0, pl.ds(k0, tk), 2 * j * LANES:2 * (j + 1) * LANES]
            _online_softmax_step(s, v_ones, m_ref.at[j], acc_ref.at[j], shift=cqb_ref[j])

    _init_softmax_state(m_ref, acc_ref)
    _kv_loop(n_full, n_any, consume)
    for j in range(n_pairs):
        o = _normalised(acc_ref[j])
        o_ref[:, j * LANES:(j + 1) * LANES] = jnp.where(lane < HD_FOX, o[:tq], o[tq:]).astype(BF16)


def _fox_attention(fq, fkb, fvb, c_q, c_k, batch, sq, tq, tk, kv_len, q_pos0):
    sk = fkb.shape[1]
    nq = sq // tq
    w = H_FOX * HD_FOX
    kern = functools.partial(_fox_kernel, tq=tq, tk=tk, kv_len=kv_len, q_pos0=q_pos0)
    return pl.pallas_call(
        kern,
        grid=(batch, nq),
        in_specs=[pl.BlockSpec((tq, w), lambda b, i: (b * nq + i, 0)),
                  pl.BlockSpec((1, sk, w), lambda b, i: (b, 0, 0)),
                  pl.BlockSpec((1, sk, 2 * w), lambda b, i: (b, 0, 0)),
                  pl.BlockSpec((1, tq, LANES), lambda b, i: (b, i, 0)),
                  pl.BlockSpec((1, H_FOX, sk), lambda b, i: (b, 0, 0))],
        out_specs=pl.BlockSpec((tq, w), lambda b, i: (b * nq + i, 0)),
        out_shape=jax.ShapeDtypeStruct((batch * sq, w), BF16),
        scratch_shapes=[pltpu.VMEM((w // LANES, 2 * tq, LANES), F32),
                        pltpu.VMEM((w // LANES, 2 * tq, 2 * LANES), F32),
                        pltpu.VMEM((w // LANES, 2 * tq, LANES), F32)],
        compiler_params=_params(2),
    )(fq, fkb, fvb, c_q, c_k)


def _mem_kernel(q_ref, k_ref, v_ref, o_ref):
    for h in range(H_MEM):
        cols = slice(h * HD_MEM, (h + 1) * HD_MEM)
        s = _dot_nt(q_ref[:, cols], k_ref[0, :, cols]) * (MEM_SCALE * LOG2E)
        m = jnp.max(s, axis=-1, keepdims=True)
        p = jnp.exp2(s - m)
        l = jnp.sum(p, axis=-1, keepdims=True)
        o = _dot(p.astype(BF16), v_ref[0, :, cols]) / l
        o_ref[:, cols] = o.astype(BF16)


def _mem_attention(mq, mkb, mvb, batch, sq, tq):
    nq = sq // tq
    n_mem = mkb.shape[1]
    w = H_MEM * HD_MEM
    return pl.pallas_call(
        _mem_kernel,
        grid=(batch, nq),
        in_specs=[pl.BlockSpec((tq, w), lambda b, i: (b * nq + i, 0)),
                  pl.BlockSpec((1, n_mem, w), lambda b, i: (b, 0, 0)),
                  pl.BlockSpec((1, n_mem, w), lambda b, i: (b, 0, 0))],
        out_specs=pl.BlockSpec((tq, w), lambda b, i: (b * nq + i, 0)),
        out_shape=jax.ShapeDtypeStruct((batch * sq, w), BF16),
        compiler_params=_params(2),
    )(mq, mkb, mvb)


def _route(x1b, wr_t, bias):
    tm = x1b.shape[0]
    scores = _sigmoid(_dot_nt(wr_t, x1b))
    sel = scores + bias
    sub = lax.broadcasted_iota(jnp.int32, (GROUP_SIZE, tm), 0)
    gs = []
    for g in range(N_GROUPS):
        blk = sel[g * GROUP_SIZE:(g + 1) * GROUP_SIZE, :]
        m1 = jnp.max(blk, axis=0, keepdims=True)
        first = jnp.min(jnp.where(blk == m1, sub, GROUP_SIZE), axis=0, keepdims=True)
        m2 = jnp.max(jnp.where(sub == first, -jnp.inf, blk), axis=0, keepdims=True)
        gs.append(m1 + m2)
    e_idx = lax.broadcasted_iota(jnp.int32, (N_EXPERTS, tm), 0)
    allowed = jnp.zeros((N_EXPERTS, tm), jnp.bool_)
    for g in range(N_GROUPS):
        rank = jnp.zeros((1, tm), jnp.int32)
        for o in range(N_GROUPS):
            if o == g:
                continue
            beats = (gs[o] >= gs[g]) if o < g else (gs[o] > gs[g])
            rank = rank + beats.astype(jnp.int32)
        keep = rank < TOPK_GROUPS
        allowed = jnp.logical_or(allowed, jnp.logical_and(e_idx // GROUP_SIZE == g, keep))
    cand = jnp.where(allowed, sel, -jnp.inf)
    chosen = jnp.zeros((N_EXPERTS, tm), jnp.bool_)
    for _ in range(TOP_K):
        m = jnp.max(cand, axis=0, keepdims=True)
        first = jnp.min(jnp.where(cand == m, e_idx, N_EXPERTS), axis=0, keepdims=True)
        pick = e_idx == first
        chosen = jnp.logical_or(chosen, pick)
        cand = jnp.where(pick, -jnp.inf, cand)
    w = jnp.where(chosen, scores, 0.0)
    return w / jnp.sum(w, axis=0, keepdims=True) * ROUTED_SCALE, chosen


def _swiglu(xb, wg, wu):
    a = _dot(xb, wg)
    return (a * _sigmoid(a)) * _dot(xb, wu)


def _merge_kernel(x_ref, olat_ref, ofox_ref, omem_ref, wg_ref, wuv_ref, wbm_ref, wbf_ref, wbc_ref,
                  wo_ref, g1_ref, b1_ref, wsg_ref, wsu_ref, wsd_ref, x1b_ref, base_ref):
    x = x_ref[...]
    xb = x.astype(BF16)
    o_mla = jnp.concatenate(
        [_dot(olat_ref[:, j * 2 * KV_LORA:(j + 1) * 2 * KV_LORA], wuv_ref[j]).astype(BF16)
         for j in range(H_MLA // 2)], axis=-1)

    def gate(n):
        return _sigmoid(_dot(xb, wg_ref[:, n * D_MODEL:(n + 1) * D_MODEL]))

    y = gate(0) * _dot(o_mla, wbm_ref[...])
    y = y + gate(1) * _dot(ofox_ref[...], wbf_ref[...])
    y = y + gate(2) * _dot(omem_ref[...], wbc_ref[...])
    mix = _dot(y.astype(BF16), wo_ref[...])
    x1 = _layer_norm(ALPHA * x + mix, g1_ref[...], b1_ref[...])
    x1b = x1.astype(BF16)
    x1b_ref[...] = x1b
    shared = _dot(_swiglu(x1b, wsg_ref[...], wsu_ref[...]).astype(BF16), wsd_ref[...])
    base_ref[...] = ALPHA * x1 + shared


def _merge(x, olat, ofox, omem, wg, wuv, wbm, wbf, wbc, wo, g1, b1, wsg, wsu, wsd, tm):
    t = x.shape[0]
    row = lambda n: pl.BlockSpec((tm, n), lambda i: (i, 0))
    ws = (wg, wuv, wbm, wbf, wbc, wo, g1, b1, wsg, wsu, wsd)
    return pl.pallas_call(
        _merge_kernel,
        grid=(t // tm,),
        in_specs=[row(D_MODEL), row(H_MLA * KV_LORA), row(H_FOX * HD_FOX), row(H_MEM * HD_MEM)]
        + [_full(w.shape) for w in ws],
        out_specs=(row(D_MODEL), row(D_MODEL)),
        out_shape=(jax.ShapeDtypeStruct((t, D_MODEL), BF16), jax.ShapeDtypeStruct((t, D_MODEL), F32)),
        compiler_params=_params(1),
    )(x, olat, ofox, omem, *ws)


BF16_ROWS = 16
PLACE_ROWS = 512
EXPERT_ROWS = 4096
EXPERT_CHUNK = 512
EXPERTS_PER_STEP = 4


def _capacity(tm):
    return min(tm, _round_up(2 * tm * TOP_K // N_EXPERTS, BF16_ROWS))


def _dispatch_kernel(x1b_ref, wr_ref, rb_ref, xs_ref, gate_ref, slot_ref, govf_ref, flag_ref,
                     pm_ref, *, tm, cap):
    n_tiles = x1b_ref.shape[0] // tm
    slots = []
    for h in range(n_tiles):
        rows = slice(h * tm, (h + 1) * tm)
        xb = x1b_ref[rows, :]
        gate, chosen = _route(xb, wr_ref[...], rb_ref[...])
        r = lax.broadcasted_iota(jnp.int32, (tm, tm), 0)
        c = lax.broadcasted_iota(jnp.int32, (tm, tm), 1)
        earlier = jnp.where(r < c, 1.0, 0.0).astype(BF16)
        rank = _dot(jnp.where(chosen, 1.0, 0.0).astype(BF16), earlier)
        fits = jnp.logical_and(chosen, rank < cap)
        slot = jnp.where(fits, rank, -1.0)
        gate_ref[:, rows] = jnp.where(fits, gate, 0.0)
        slot_ref[:, rows] = slot
        g_ovf = jnp.where(fits, 0.0, gate)
        govf_ref[rows, :] = jnp.concatenate(
            [g_ovf, jnp.zeros((LANES - N_EXPERTS, tm), F32)], axis=0).T
        any_ovf = jnp.max(jnp.max(g_ovf, axis=0, keepdims=True), axis=1, keepdims=True)
        flag_ref[h] = jnp.broadcast_to(jnp.where(any_ovf > 0.0, 1, 0),
                                       flag_ref.shape[1:]).astype(jnp.int32)
        slots.append(slot)
    s_iota = lax.broadcasted_iota(jnp.int32, (cap, tm), 0).astype(F32)
    for h in range(n_tiles):
        slot = slots[h]
        xb = x1b_ref[h * tm:(h + 1) * tm, :]
        for e in range(N_EXPERTS):
            pm_ref[h, e * cap:(e + 1) * cap, :] = jnp.where(
                s_iota == slot[e:e + 1, :], 1.0, 0.0).astype(BF16)
        per = PLACE_ROWS // cap
        for g in range(N_EXPERTS // per):
            xs = _dot(pm_ref[h, g * PLACE_ROWS:(g + 1) * PLACE_ROWS, :], xb).astype(BF16)
            xs_ref[g * per:(g + 1) * per, h * cap:(h + 1) * cap, :] = xs.reshape(per, cap, D_MODEL)


def _dispatch(x1b, wr_t, rb, tm, cap):
    t = x1b.shape[0]
    nt = t // tm
    ts = 2 if nt % 2 == 0 else 1
    col = pl.BlockSpec((N_EXPERTS, ts * tm), lambda i: (0, i))
    return pl.pallas_call(
        functools.partial(_dispatch_kernel, tm=tm, cap=cap),
        grid=(nt // ts,),
        in_specs=[pl.BlockSpec((ts * tm, D_MODEL), lambda i: (i, 0)), _full(wr_t.shape),
                  _full(rb.shape)],
        out_specs=(pl.BlockSpec((N_EXPERTS, ts * cap, D_MODEL), lambda i: (0, i, 0)), col, col,
                   pl.BlockSpec((ts * tm, LANES), lambda i: (i, 0)),
                   pl.BlockSpec((ts, 8, LANES), lambda i: (i, 0, 0))),
        out_shape=(jax.ShapeDtypeStruct((N_EXPERTS, nt * cap, D_MODEL), BF16),
                   jax.ShapeDtypeStruct((N_EXPERTS, t), F32),
                   jax.ShapeDtypeStruct((N_EXPERTS, t), F32),
                   jax.ShapeDtypeStruct((t, LANES), F32),
                   jax.ShapeDtypeStruct((nt, 8, LANES), jnp.int32)),
        scratch_shapes=[pltpu.VMEM((ts, N_EXPERTS * cap, tm), BF16)],
        compiler_params=_params(1),
    )(x1b, wr_t, rb)


def _expert_kernel(xs_ref, xs2_ref, wg_ref, wu_ref, wd_ref, ys_ref, ys2_ref, wgb_ref, wub_ref,
                   wdb_ref, *, chunk):
    def ffn(x):
        hid = _swiglu(x, wgb_ref[...], wub_ref[...])
        return _dot(hid.astype(BF16), wdb_ref[...]).astype(BF16)

    @pl.when(pl.program_id(1) == 0)
    def _():
        wgb_ref[...] = wg_ref[0].astype(BF16)
        wub_ref[...] = wu_ref[0].astype(BF16)
        wdb_ref[...] = wd_ref[0].astype(BF16)
        ys2_ref[0] = ffn(xs2_ref[0])

    def body(c, carry):
        r0 = pl.multiple_of(c * chunk, chunk)
        ys_ref[0, pl.ds(r0, chunk), :] = ffn(xs_ref[0, pl.ds(r0, chunk), :])
        return carry

    lax.fori_loop(0, xs_ref.shape[1] // chunk, body, 0)


def _experts(xs, xs2, weg, weu, wed):
    n_e, rows, _ = xs.shape
    rb = min(EXPERT_ROWS, rows)
    blk = pl.BlockSpec((1, rb, D_MODEL), lambda e, i: (e, i, 0))
    blk2 = pl.BlockSpec((1, xs2.shape[1], D_MODEL), lambda e, i: (e, 0, 0))
    return pl.pallas_call(
        functools.partial(_expert_kernel, chunk=min(EXPERT_CHUNK, rb)),
        grid=(n_e, rows // rb),
        in_specs=[blk, blk2,
                  pl.BlockSpec((1, D_MODEL, D_EXPERT), lambda e, i: (e, 0, 0)),
                  pl.BlockSpec((1, D_MODEL, D_EXPERT), lambda e, i: (e, 0, 0)),
                  pl.BlockSpec((1, D_EXPERT, D_MODEL), lambda e, i: (e, 0, 0))],
        out_specs=(blk, blk2),
        out_shape=(jax.ShapeDtypeStruct(xs.shape, BF16), jax.ShapeDtypeStruct(xs2.shape, BF16)),
        scratch_shapes=[pltpu.VMEM((D_MODEL, D_EXPERT), BF16), pltpu.VMEM((D_MODEL, D_EXPERT), BF16),
                        pltpu.VMEM((D_EXPERT, D_MODEL), BF16)],
        compiler_params=_params(2),
    )(xs, xs2, weg, weu, wed)


def _dense_routed_kernel(flag_ref, x1b_ref, gate_ref, weg_ref, weu_ref, wed_ref, out_ref):
    step = pl.program_id(1)

    @pl.when(step == 0)
    def _():
        out_ref[...] = jnp.zeros_like(out_ref)

    @pl.when(flag_ref[pl.program_id(0)] > 0)
    def _():
        xb = x1b_ref[...]
        gate = gate_ref[...]
        lane = lax.broadcasted_iota(jnp.int32, gate.shape, 1)
        for j in range(EXPERTS_PER_STEP):
            e = step * EXPERTS_PER_STEP + j
            g_col = jnp.sum(jnp.where(lane == e, gate, 0.0), axis=-1, keepdims=True)
            hid = _swiglu(xb, weg_ref[j], weu_ref[j])
            out_ref[...] += g_col * _dot(hid.astype(BF16), wed_ref[j])


def _dense_routed(flags, x1b, gate, weg, weu, wed, tm):
    t = x1b.shape[0]
    eb = EXPERTS_PER_STEP
    row = lambda n: pl.BlockSpec((tm, n), lambda i, s, f: (i, 0))
    wspec = lambda a, b: pl.BlockSpec((eb, a, b), lambda i, s, f: (jnp.where(f[i] > 0, s, 0), 0, 0))
    return pl.pallas_call(
        _dense_routed_kernel,
        grid_spec=pltpu.PrefetchScalarGridSpec(
            num_scalar_prefetch=1,
            grid=(t // tm, N_EXPERTS // eb),
            in_specs=[row(D_MODEL), row(LANES), wspec(D_MODEL, D_EXPERT), wspec(D_MODEL, D_EXPERT),
                      wspec(D_EXPERT, D_MODEL)],
            out_specs=row(D_MODEL)),
        out_shape=jax.ShapeDtypeStruct((t, D_MODEL), F32),
        compiler_params=_params(2),
    )(flags, x1b, gate, weg, weu, wed)


def _combine_kernel(gate_ref, slot_ref, ys_ref, base_ref, *rest, cap, has_overflow):
    if has_overflow:
        rovf_ref, g2_ref, b2_ref, y_ref, pw_ref = rest
    else:
        g2_ref, b2_ref, y_ref, pw_ref = rest
    tm = base_ref.shape[0]
    gate = gate_ref[...]
    slot = slot_ref[...]
    s_iota = lax.broadcasted_iota(jnp.int32, (cap, tm), 0).astype(F32)
    for e in range(N_EXPERTS):
        pw_ref[e * cap:(e + 1) * cap, :] = jnp.where(
            s_iota == slot[e:e + 1, :], gate[e:e + 1, :], 0.0).astype(BF16)
    routed = rovf_ref[...] if has_overflow else jnp.zeros((tm, D_MODEL), F32)
    per = PLACE_ROWS // cap
    for g in range(N_EXPERTS // per):
        ys = ys_ref[g * per:(g + 1) * per].reshape(PLACE_ROWS, D_MODEL)
        routed = routed + lax.dot_general(pw_ref[g * PLACE_ROWS:(g + 1) * PLACE_ROWS, :], ys,
                                          (((0,), (0,)), ((), ())), preferred_element_type=F32)
    y_ref[...] = _layer_norm(base_ref[...] + routed, g2_ref[...], b2_ref[...])


def _combine(gate_t, slot_t, ys, base, rovf, g2, b2, tm, cap):
    t = base.shape[0]
    col = pl.BlockSpec((N_EXPERTS, tm), lambda i: (0, i))
    row = pl.BlockSpec((tm, D_MODEL), lambda i: (i, 0))
    has_overflow = rovf is not None
    operands = (gate_t, slot_t, ys, base) + ((rovf,) if has_overflow else ()) + (g2, b2)
    return pl.pallas_call(
        functools.partial(_combine_kernel, cap=cap, has_overflow=has_overflow),
        grid=(t // tm,),
        in_specs=[col, col, pl.BlockSpec((N_EXPERTS, cap, D_MODEL), lambda i: (0, i, 0)), row]
        + ([row] if has_overflow else []) + [_full(g2.shape), _full(b2.shape)],
        out_specs=row,
        out_shape=jax.ShapeDtypeStruct((t, D_MODEL), F32),
        scratch_shapes=[pltpu.VMEM((N_EXPERTS * cap, tm), BF16)],
        compiler_params=_params(1),
    )(*operands)


def _moe_place(x1b, wr_t, rb, tm):
    cap = _capacity(tm)
    assert PLACE_ROWS % cap == 0 and (N_EXPERTS * cap) % PLACE_ROWS == 0
    return _dispatch(x1b, wr_t, rb, tm, cap)


def _moe_ln2(placed, ys, x1b, base, weg, weu, wed, g2, b2, tm):
    cap = _capacity(tm)
    _, gate_t, slot_t, gate_ovf, flags = placed
    flags = flags[:, 0, 0]

    def with_overflow():
        rovf = _dense_routed(flags, x1b, gate_ovf, weg.astype(BF16), weu.astype(BF16),
                             wed.astype(BF16), tm)
        return _combine(gate_t, slot_t, ys, base, rovf, g2, b2, tm, cap)

    return lax.cond(jnp.any(flags > 0), with_overflow,
                    lambda: _combine(gate_t, slot_t, ys, base, None, g2, b2, tm, cap))


def _rot_half_cols(w):
    half = w.shape[-1] // 2
    return jnp.concatenate([-w[..., half:], w[..., :half]], axis=-1)


def _block_diag2(a, b):
    za = jnp.zeros((a.shape[0], b.shape[1]), a.dtype)
    zb = jnp.zeros((b.shape[0], a.shape[1]), a.dtype)
    return jnp.concatenate([jnp.concatenate([a, za], 1), jnp.concatenate([zb, b], 1)], 0)


def _prep_weights(w_in, b_f, q_norm_g, w_q_up, kv_norm_g, w_kv_up):
    widths = (Q_LORA, KV_LORA, ROPE_DIM, H_FOX * HD_FOX, H_FOX * HD_FOX, H_FOX * HD_FOX, H_FOX,
              H_MEM * HD_MEM, N_BRANCH * D_MODEL)
    offs = np.cumsum((0,) + widths)
    w_ql, w_ckv, w_kr, w_fq, w_fk, w_fv, w_fl, w_mq, w_g = (
        w_in[:, offs[i]:offs[i + 1]] for i in range(len(widths)))
    rep = LANES // ROPE_DIM
    w_main = jnp.concatenate(
        [w_ql, w_ckv, jnp.tile(w_kr, (1, rep)), jnp.tile(_rot_half_cols(w_kr), (1, rep)),
         w_fq, w_fk, w_fv, w_mq, jnp.pad(w_fl, ((0, 0), (0, LANES - H_FOX)))], axis=1).astype(BF16)
    assert w_main.shape[1] == C_END
    wq3 = w_q_up.reshape(Q_LORA, H_MLA, NOPE_DIM + ROPE_DIM)
    wq_nope = wq3[:, :, :NOPE_DIM].reshape(Q_LORA, H_MLA * NOPE_DIM)
    wq_rope = wq3[:, :, NOPE_DIM:]
    w_q = jnp.concatenate(
        [wq_nope, wq_rope.reshape(Q_LORA, -1), _rot_half_cols(wq_rope).reshape(Q_LORA, -1)],
        axis=1).astype(BF16)
    wkv3 = w_kv_up.reshape(KV_LORA, H_MLA, NOPE_DIM + V_DIM)
    w_uk = wkv3[:, :, :NOPE_DIM]
    w_uv = wkv3[:, :, NOPE_DIM:]
    w_abs = jnp.stack([_block_diag2(w_uk[:, 2 * j].T, w_uk[:, 2 * j + 1].T)
                       for j in range(H_MLA // 2)]).astype(BF16)
    w_uvp = jnp.stack([_block_diag2(w_uv[:, 2 * j], w_uv[:, 2 * j + 1])
                       for j in range(H_MLA // 2)]).astype(BF16)
    b_fp = jnp.pad(b_f, (0, LANES - H_FOX)).reshape(1, LANES)
    return (w_main, w_q, w_abs, q_norm_g.reshape(1, -1), kv_norm_g.reshape(1, -1), b_fp,
            w_g.astype(BF16), w_uvp)


def _rope_tables(pos):
    half = ROPE_DIM // 2
    inv_freq = ROPE_THETA ** (-jnp.arange(half, dtype=F32) / half)
    ang = pos.astype(F32)[:, None] * inv_freq[None, :]
    rep = 2 * LANES // ROPE_DIM
    cos = jnp.tile(jnp.concatenate([jnp.cos(ang)] * 2, axis=-1), (1, rep))
    sin = jnp.tile(jnp.concatenate([jnp.sin(ang)] * 2, axis=-1), (1, rep))
    return cos, sin


def _round_up(n, m):
    return (n + m - 1) // m * m


def kernel(x_prompt, x_sample, mem_prompt, cache_mla_latent, cache_mla_krope, cache_fox_k,
           cache_fox_v, cache_fox_logf, cache_mem_k, cache_mem_v, ln1_g, ln1_b, w_in, b_f,
           q_norm_g, w_q_up, kv_norm_g, w_kv_up, w_mem_k, w_mem_v, w_br_mla, w_br_fox,
           w_br_mem, w_out, ln2_g, ln2_b, w_router, router_bias, w_e_gate, w_e_up, w_e_down,
           w_s_gate, w_s_up, w_s_down):
    assert w_in.shape[0] == DEPTH == 1
    batch, seq, _ = x_prompt.shape
    dec_batch, dec_seq, _ = x_sample.shape
    past = cache_mla_latent.shape[2]
    n_mem = mem_prompt.shape[1]
    l = 0

    (w_main, w_q, w_abs, q_g, kv_g, b_fp, w_gate, w_uvp) = _prep_weights(
        w_in[l], b_f[l], q_norm_g[l], w_q_up[l], kv_norm_g[l], w_kv_up[l])
    merge_w = (w_gate, w_uvp, w_br_mla[l].astype(BF16), w_br_fox[l].astype(BF16),
               w_br_mem[l].astype(BF16), w_out[l].astype(BF16), ln1_g[l].reshape(1, -1),
               ln1_b[l].reshape(1, -1), w_s_gate[l].astype(BF16), w_s_up[l].astype(BF16),
               w_s_down[l].astype(BF16))
    route_w = (w_router[l].T.astype(BF16), router_bias[l].reshape(-1, 1))
    expert_w = (w_e_gate[l], w_e_up[l], w_e_down[l])
    ln2_w = (ln2_g[l].reshape(1, -1), ln2_b[l].reshape(1, -1))

    tp = batch * seq
    xp = x_prompt.reshape(tp, D_MODEL)
    tm_p = min(512, seq)
    tq_p = min(256, seq)
    tk_p = min(512, seq)
    cos_p, sin_p = _rope_tables(jnp.arange(seq, dtype=jnp.int32))
    (qcat, kcat, lat, latb, kr, fq, fk, fv, fkb, fvb, mq, logf, logfp) = _proj(
        xp, w_main, w_q, w_abs, q_g, kv_g, b_fp, cos_p, sin_p, tm_p)
    tmem = batch * n_mem
    m_k, m_v, m_kb, m_vb = _memkv(mem_prompt.reshape(tmem, D_MODEL), w_mem_k[l].astype(BF16),
                                  w_mem_v[l].astype(BF16), min(512, tmem))
    c_p = _cumsum(logfp.reshape(batch, seq, LANES))
    c_keys = jnp.swapaxes(c_p[:, :, :H_FOX], 1, 2)
    olat = _mla_attention(qcat, kcat.reshape(batch, seq, -1), latb.reshape(batch, seq, -1),
                          batch, seq, tq_p, tk_p, seq, 0)
    ofox = _fox_attention(fq, fkb.reshape(batch, seq, -1), fvb.reshape(batch, seq, -1), c_p, c_keys,
                          batch, seq, tk_p, tk_p, seq, 0)
    omem = _mem_attention(mq, m_kb.reshape(batch, n_mem, -1), m_vb.reshape(batch, n_mem, -1),
                          batch, seq, tk_p)
    x1b_p, base_p = _merge(xp, olat, ofox, omem, *merge_w, tm_p)
    tm_moe = min(256, seq)
    placed_p = _moe_place(x1b_p, *route_w, tm_moe)

    ts = dec_batch * dec_seq
    xs = x_sample.reshape(ts, D_MODEL)
    cos_s, sin_s = _rope_tables(jnp.tile(past + jnp.arange(dec_seq, dtype=jnp.int32), dec_batch))
    (qcat_s, kcat_s, lat_s, latb_s, kr_s, fq_s, fk_s, fv_s, fkb_s, fvb_s, mq_s, logf_s,
     logfp_s) = _proj(xs, w_main, w_q, w_abs, q_g, kv_g, b_fp, cos_s, sin_s, ts)
    kv_len = past + dec_seq
    tk_s = 1024
    sk = _round_up(kv_len, tk_s)

    def with_cache(cache, new):
        new = new.reshape(dec_batch, dec_seq, -1)
        pad = jnp.zeros((dec_batch, sk - kv_len, new.shape[-1]), new.dtype)
        return jnp.concatenate([cache.astype(new.dtype), new, pad], axis=1)

    rep = LANES // ROPE_DIM
    kcat_all = with_cache(
        jnp.concatenate([cache_mla_latent[l], jnp.tile(cache_mla_krope[l], (1, 1, rep))], axis=-1),
        kcat_s)
    lat_all = with_cache(
        jnp.concatenate([cache_mla_latent[l], jnp.ones((dec_batch, past, LANES), F32)], axis=-1), latb_s)
    fk_all = with_cache(cache_fox_k[l].astype(BF16).reshape(dec_batch, past, -1), fkb_s)
    n_pairs = H_FOX * HD_FOX // LANES
    fv_pairs = cache_fox_v[l].astype(BF16).reshape(dec_batch, past, n_pairs, LANES)
    fv_all = with_cache(
        jnp.concatenate([fv_pairs, jnp.ones_like(fv_pairs)], axis=-1).reshape(dec_batch, past, -1),
        fvb_s)
    logf_all = with_cache(jnp.pad(cache_fox_logf[l], ((0, 0), (0, 0), (0, LANES - H_FOX))), logfp_s)
    c_s = _cumsum(logf_all)
    c_keys_s = jnp.swapaxes(c_s[:, :, :H_FOX], 1, 2)
    c_q_s = c_s[:, past:kv_len]
    olat_s = _mla_attention(qcat_s, kcat_all, lat_all, dec_batch, dec_seq, dec_seq, tk_s, kv_len, past)
    ofox_s = _fox_attention(fq_s, fk_all, fv_all, c_q_s, c_keys_s, dec_batch, dec_seq, dec_seq, tk_s,
                            kv_len, past)
    omem_s = _mem_attention(mq_s, cache_mem_k[l].reshape(dec_batch, n_mem, -1).astype(BF16),
                            cache_mem_v[l].reshape(dec_batch, n_mem, -1).astype(BF16),
                            dec_batch, dec_seq, dec_seq)
    x1b_s, base_s = _merge(xs, olat_s, ofox_s, omem_s, *merge_w, ts)
    placed_s = _moe_place(x1b_s, *route_w, ts)

    ys_p, ys_s = _experts(placed_p[0], placed_s[0], *expert_w)
    y_p = _moe_ln2(placed_p, ys_p, x1b_p, base_p, *expert_w, *ln2_w, tm_moe)
    y_s = _moe_ln2(placed_s, ys_s, x1b_s, base_s, *expert_w, *ln2_w, ts)
    y_p = y_p.reshape(batch, seq, D_MODEL)
    y_s = y_s.reshape(dec_batch, dec_seq, D_MODEL)

    def stack(a, b, s, *tail_shape):
        return a.reshape(1, b, s, *tail_shape)

    return (y_p, y_s,
            stack(lat, batch, seq, KV_LORA), stack(kr, batch, seq, ROPE_DIM),
            stack(fk, batch, seq, H_FOX, HD_FOX), stack(fv, batch, seq, H_FOX, HD_FOX),
            stack(logf, batch, seq, H_FOX),
            stack(m_k, batch, n_mem, H_MEM, HD_MEM), stack(m_v, batch, n_mem, H_MEM, HD_MEM),
            stack(lat_s, dec_batch, dec_seq, KV_LORA), stack(kr_s, dec_batch, dec_seq, ROPE_DIM),
            stack(fk_s, dec_batch, dec_seq, H_FOX, HD_FOX), stack(fv_s, dec_batch, dec_seq, H_FOX, HD_FOX),
            stack(logf_s, dec_batch, dec_seq, H_FOX))
```

```python
import functools

import jax
import jax.numpy as jnp
import numpy as np
from jax import lax
from jax.experimental import pallas as pl
from jax.experimental.pallas import tpu as pltpu

F32 = jnp.float32
BF16 = jnp.bfloat16

D_MODEL = 1024
CHUNK = 64
H_MLA = 8
Q_LORA = 256
KV_LORA = 128
NOPE_DIM = 64
ROPE_DIM = 32
V_DIM = 64
ROPE_THETA = 10000.0
MLA_SCALE = (NOPE_DIM + ROPE_DIM) ** -0.5
H_FOX = 8
HD_FOX = 64
FOX_SCALE = HD_FOX ** -0.5
H_MEM = 4
HD_MEM = 128
MEM_SCALE = HD_MEM ** -0.5
N_BRANCH = 3
N_EXPERTS = 64
TOP_K = 8
N_GROUPS = 8
GROUP_SIZE = N_EXPERTS // N_GROUPS
TOPK_GROUPS = 4
D_EXPERT = 256
ROUTED_SCALE = 2.5
DEPTH = 1
ALPHA = (2 * DEPTH) ** 0.25
NORM_EPS = 1e-5
NEG_INF = -1e30

LOG2E = 1.4426950408889634
LANES = 128
VMEM_LIMIT = 56 * 1024 * 1024

C_QLAT = 0
C_CKV = C_QLAT + Q_LORA
C_KR = C_CKV + KV_LORA
C_KRR = C_KR + LANES
C_FQ = C_KRR + LANES
C_FK = C_FQ + H_FOX * HD_FOX
C_FV = C_FK + H_FOX * HD_FOX
C_MQ = C_FV + H_FOX * HD_FOX
C_FL = C_MQ + H_MEM * HD_MEM
C_END = C_FL + LANES


def _params(n_grid_axes):
    return pltpu.CompilerParams(
        dimension_semantics=("arbitrary",) * n_grid_axes, vmem_limit_bytes=VMEM_LIMIT)


def _full(shape):
    return pl.BlockSpec(shape, lambda *_: (0,) * len(shape))


def _dot(a, b):
    return jnp.dot(a, b, preferred_element_type=F32)


def _dot_nt(a, b):
    return lax.dot_general(a, b, (((1,), (1,)), ((), ())), preferred_element_type=F32)


def _sigmoid(x):
    return 1.0 / (1.0 + jnp.exp(-x))


def _rms(x, g):
    ms = jnp.mean(x * x, axis=-1, keepdims=True)
    return x * lax.rsqrt(ms + NORM_EPS) * g


def _layer_norm(x, g, b):
    mu = jnp.mean(x, axis=-1, keepdims=True)
    xc = x - mu
    var = jnp.mean(xc * xc, axis=-1, keepdims=True)
    return xc * lax.rsqrt(var + NORM_EPS) * g + b


def _proj_kernel(x_ref, w_ref, wq_ref, wabs_ref, qg_ref, kvg_ref, bf_ref, cos_ref, sin_ref,
                 qcat_ref, kcat_ref, lat_ref, latb_ref, kr_ref, fq_ref, fk_ref, fv_ref,
                 fkb_ref, fvb_ref, mq_ref, logf_ref, logfp_ref):
    xb = x_ref[...].astype(BF16)
    tm = xb.shape[0]

    def proj(c0, width):
        return _dot(xb, w_ref[:, c0:c0 + width])

    cos = cos_ref[...]
    sin = sin_ref[...]
    qn = _rms(proj(C_QLAT, Q_LORA), qg_ref[...]).astype(BF16)
    q2 = _dot(qn, wq_ref[...])
    nq = H_MLA * NOPE_DIM
    nr = H_MLA * ROPE_DIM
    q_rope = (q2[:, nq:nq + nr] * cos + q2[:, nq + nr:nq + 2 * nr] * sin) * (MLA_SCALE * LOG2E)
    lane = lax.broadcasted_iota(jnp.int32, (tm, LANES), 1)
    heads_per_group = LANES // ROPE_DIM
    for j in range(H_MLA // 2):
        q_abs = _dot(q2[:, j * LANES:(j + 1) * LANES].astype(BF16), wabs_ref[j]) * (MLA_SCALE * LOG2E)
        for half in range(2):
            h = 2 * j + half
            grp = h // heads_per_group
            qr = q_rope[:, grp * LANES:(grp + 1) * LANES]
            qr = jnp.where(lane // ROPE_DIM == h % heads_per_group, qr, 0.0)
            qcat_ref[h, :, 0:LANES] = q_abs[:, half * LANES:(half + 1) * LANES].astype(BF16)
            qcat_ref[h, :, LANES:2 * LANES] = qr.astype(BF16)
    lat = _rms(proj(C_CKV, KV_LORA), kvg_ref[...])
    lat_ref[...] = lat
    ones = jnp.ones((tm, LANES), BF16)
    latb_ref[:, 0:KV_LORA] = lat.astype(BF16)
    latb_ref[:, KV_LORA:KV_LORA + LANES] = ones
    kr = proj(C_KR, LANES) * cos[:, :LANES] + proj(C_KRR, LANES) * sin[:, :LANES]
    kr_ref[...] = kr[:, :ROPE_DIM]
    kcat_ref[:, 0:KV_LORA] = lat.astype(BF16)
    kcat_ref[:, KV_LORA:KV_LORA + LANES] = kr.astype(BF16)
    fq_ref[...] = (proj(C_FQ, H_FOX * HD_FOX) * (FOX_SCALE * LOG2E)).astype(BF16)
    fk = proj(C_FK, H_FOX * HD_FOX)
    fk_ref[...] = fk
    fkb_ref[...] = fk.astype(BF16)
    fv = proj(C_FV, H_FOX * HD_FOX)
    fv_ref[...] = fv
    for j in range(H_FOX * HD_FOX // LANES):
        fvb_ref[:, 2 * j * LANES:(2 * j + 1) * LANES] = fv[:, j * LANES:(j + 1) * LANES].astype(BF16)
        fvb_ref[:, (2 * j + 1) * LANES:(2 * j + 2) * LANES] = ones
    mq_ref[...] = proj(C_MQ, H_MEM * HD_MEM).astype(BF16)
    z = proj(C_FL, LANES) + bf_ref[...]
    log_f = jnp.minimum(z, 0.0) - jnp.log1p(jnp.exp(-jnp.abs(z)))
    logfp_ref[...] = log_f
    logf_ref[...] = log_f[:, :H_FOX]


def _proj(x, w_main, w_q, w_abs, q_g, kv_g, b_f, cos_t, sin_t, tm):
    t = x.shape[0]
    n_tab = cos_t.shape[0] // tm
    row = lambda w: pl.BlockSpec((tm, w), lambda i: (i, 0))
    tab = pl.BlockSpec((tm, 2 * LANES), lambda i: (i % n_tab, 0))
    hw = H_FOX * HD_FOX
    out_shape = (
        jax.ShapeDtypeStruct((H_MLA, t, 2 * LANES), BF16),
        jax.ShapeDtypeStruct((t, 2 * LANES), BF16),
        jax.ShapeDtypeStruct((t, KV_LORA), F32),
        jax.ShapeDtypeStruct((t, KV_LORA + LANES), BF16),
        jax.ShapeDtypeStruct((t, ROPE_DIM), F32),
        jax.ShapeDtypeStruct((t, hw), BF16),
        jax.ShapeDtypeStruct((t, hw), F32),
        jax.ShapeDtypeStruct((t, hw), F32),
        jax.ShapeDtypeStruct((t, hw), BF16),
        jax.ShapeDtypeStruct((t, 2 * hw), BF16),
        jax.ShapeDtypeStruct((t, H_MEM * HD_MEM), BF16),
        jax.ShapeDtypeStruct((t, H_FOX), F32),
        jax.ShapeDtypeStruct((t, LANES), F32),
    )
    out_specs = (
        pl.BlockSpec((H_MLA, tm, 2 * LANES), lambda i: (0, i, 0)),
        row(2 * LANES), row(KV_LORA), row(KV_LORA + LANES), row(ROPE_DIM), row(hw), row(hw), row(hw),
        row(hw), row(2 * hw), row(H_MEM * HD_MEM), row(H_FOX), row(LANES),
    )
    return pl.pallas_call(
        _proj_kernel,
        grid=(t // tm,),
        in_specs=[row(D_MODEL), _full(w_main.shape), _full(w_q.shape), _full(w_abs.shape),
                  _full(q_g.shape), _full(kv_g.shape), _full(b_f.shape), tab, tab],
        out_specs=out_specs,
        out_shape=out_shape,
        compiler_params=_params(1),
    )(x, w_main, w_q, w_abs, q_g, kv_g, b_f, cos_t, sin_t)


def _memkv_kernel(x_ref, wk_ref, wv_ref, k_ref, v_ref, kb_ref, vb_ref):
    xb = x_ref[...].astype(BF16)
    k = _dot(xb, wk_ref[...])
    v = _dot(xb, wv_ref[...])
    k_ref[...] = k
    v_ref[...] = v
    kb_ref[...] = k.astype(BF16)
    vb_ref[...] = v.astype(BF16)


def _memkv(x, wk, wv, tm):
    t = x.shape[0]
    w = H_MEM * HD_MEM
    row = lambda n: pl.BlockSpec((tm, n), lambda i: (i, 0))
    return pl.pallas_call(
        _memkv_kernel,
        grid=(t // tm,),
        in_specs=[row(D_MODEL), _full(wk.shape), _full(wv.shape)],
        out_specs=(row(w), row(w), row(w), row(w)),
        out_shape=(jax.ShapeDtypeStruct((t, w), F32), jax.ShapeDtypeStruct((t, w), F32),
                   jax.ShapeDtypeStruct((t, w), BF16), jax.ShapeDtypeStruct((t, w), BF16)),
        compiler_params=_params(1),
    )(x, wk, wv)


CUM_BLOCK = 256


def _bf16_pieces(x):
    hi = x.astype(BF16)
    r1 = x - hi.astype(F32)
    mid = r1.astype(BF16)
    lo = (r1 - mid.astype(F32)).astype(BF16)
    return hi, mid, lo


def _cumsum_kernel(x_ref, c_ref):
    s = x_ref.shape[1]
    r = lax.broadcasted_iota(jnp.int32, (CUM_BLOCK, CUM_BLOCK), 0)
    c = lax.broadcasted_iota(jnp.int32, (CUM_BLOCK, CUM_BLOCK), 1)
    tri = jnp.where(r >= c, 1.0, 0.0).astype(BF16)

    def body(i, carry):
        r0 = pl.multiple_of(i * CUM_BLOCK, CUM_BLOCK)
        x = x_ref[0, pl.ds(r0, CUM_BLOCK), :]
        cs = sum(_dot(tri, piece) for piece in _bf16_pieces(x)) + carry
        c_ref[0, pl.ds(r0, CUM_BLOCK), :] = cs * LOG2E
        return cs[CUM_BLOCK - 1:CUM_BLOCK, :]

    lax.fori_loop(0, s // CUM_BLOCK, body, jnp.zeros((1, LANES), F32))


def _cumsum(x):
    b, s, _ = x.shape
    spec = pl.BlockSpec((1, s, LANES), lambda i: (i, 0, 0))
    return pl.pallas_call(
        _cumsum_kernel, grid=(b,), in_specs=[spec], out_specs=spec,
        out_shape=jax.ShapeDtypeStruct(x.shape, F32), compiler_params=_params(1),
    )(x)


def _online_softmax_step(s, v_ones, m_ref, acc_ref, shift=None):
    def lanes(x, width):
        return jnp.concatenate([x] * (width // LANES), axis=-1)

    m = m_ref[...]
    r = jnp.max(s, axis=-1, keepdims=True)
    m_new = jnp.maximum(m, r if shift is None else r + shift)
    alpha = jnp.exp2(m - m_new)
    p = jnp.exp2(s - lanes(m_new if shift is None else m_new - shift, s.shape[-1]))
    m_ref[...] = m_new
    acc_ref[...] = lanes(alpha, acc_ref.shape[-1]) * acc_ref[...] + _dot(p.astype(BF16), v_ones)


def _init_softmax_state(m_ref, acc_ref):
    m_ref[...] = jnp.full(m_ref.shape, NEG_INF, F32)
    acc_ref[...] = jnp.zeros(acc_ref.shape, F32)


def _normalised(acc):
    return acc[:, :LANES] / acc[:, LANES:]


def _kv_tile_counts(first_visible_end, last_visible_end, tk, kv_len, sk):
    n_full = jnp.minimum(first_visible_end // tk, kv_len // tk)
    n_any = jnp.minimum((jnp.minimum(last_visible_end, kv_len) + tk - 1) // tk, sk // tk)
    return n_full, n_any


def _kv_loop(n_full, n_any, consume):
    def body(kt, carry, masked):
        consume(kt, masked)
        return carry

    lax.fori_loop(0, n_full, functools.partial(body, masked=False), 0)
    lax.fori_loop(n_full, n_any, functools.partial(body, masked=True), 0)


def _kv_loop_pipelined(n_any, produce, consume):
    produce(0, 0)
    n_pairs = (n_any - 1) // 2

    def body(i, carry):
        kt = 2 * i
        produce(1, kt + 1)
        consume(0, kt, False)
        produce(0, kt + 2)
        consume(1, kt + 1, False)
        return carry

    lax.fori_loop(0, n_pairs, body, 0)
    kt0 = 2 * n_pairs

    @pl.when(kt0 == n_any - 1)
    def _():
        consume(0, kt0, True)

    @pl.when(kt0 < n_any - 1)
    def _():
        produce(1, kt0 + 1)
        consume(0, kt0, False)
        consume(1, kt0 + 1, True)


def _mla_kernel(q_ref, k_ref, v_ref, o_ref, m_ref, acc_ref, s_ref, *, tq, tk, kv_len, q_pos0):
    sk = k_ref.shape[1]
    rows = H_MLA * tq
    q_start = q_pos0 + pl.program_id(1) * tq
    _, n_any = _kv_tile_counts((q_start // CHUNK + 1) * CHUNK,
                               ((q_start + tq - 1) // CHUNK + 1) * CHUNK, tk, kv_len, sk)
    q = q_ref[...].reshape(rows, q_ref.shape[-1])
    q_chunk = (q_start + lax.broadcasted_iota(jnp.int32, (rows, 1), 0) % tq) // CHUNK

    def produce(slot, kt):
        s_ref[slot] = _dot_nt(q, k_ref[0, pl.ds(pl.multiple_of(kt * tk, tk), tk), :])

    def consume(slot, kt, masked):
        k0 = pl.multiple_of(kt * tk, tk)
        s = s_ref[slot]
        if masked:
            k_pos = k0 + lax.broadcasted_iota(jnp.int32, (1, tk), 1)
            mask = k_pos // CHUNK <= q_chunk
            if kv_len < sk:
                mask = jnp.logical_and(mask, k_pos < kv_len)
            s = jnp.where(mask, s, NEG_INF)
        _online_softmax_step(s, v_ref[0, pl.ds(k0, tk), :], m_ref, acc_ref)

    _init_softmax_state(m_ref, acc_ref)
    _kv_loop_pipelined(n_any, produce, consume)
    o = _normalised(acc_ref[...]).astype(BF16)
    for h in range(H_MLA):
        o_ref[:, h * KV_LORA:(h + 1) * KV_LORA] = o[h * tq:(h + 1) * tq, :]


def _mla_attention(qcat, kcat, latb, batch, sq, tq, tk, kv_len, q_pos0):
    sk = kcat.shape[1]
    nq = sq // tq
    assert tk % tq == 0 and q_pos0 % tq == 0 and tq % CHUNK in (0, tq)
    kern = functools.partial(_mla_kernel, tq=tq, tk=tk, kv_len=kv_len, q_pos0=q_pos0)
    return pl.pallas_call(
        kern,
        grid=(batch, nq),
        in_specs=[pl.BlockSpec((H_MLA, tq, 2 * LANES), lambda b, i: (0, b * nq + i, 0)),
                  pl.BlockSpec((1, sk, 2 * LANES), lambda b, i: (b, 0, 0)),
                  pl.BlockSpec((1, sk, 2 * LANES), lambda b, i: (b, 0, 0))],
        out_specs=pl.BlockSpec((tq, H_MLA * KV_LORA), lambda b, i: (b * nq + i, 0)),
        out_shape=jax.ShapeDtypeStruct((batch * sq, H_MLA * KV_LORA), BF16),
        scratch_shapes=[pltpu.VMEM((H_MLA * tq, LANES), F32),
                        pltpu.VMEM((H_MLA * tq, 2 * LANES), F32),
                        pltpu.VMEM((2, H_MLA * tq, tk), F32)],
        compiler_params=_params(2),
    )(qcat, kcat, latb)


def _fox_kernel(q_ref, k_ref, v_ref, cq_ref, ck_ref, o_ref, m_ref, acc_ref, cqb_ref, *, tq, tk,
                kv_len, q_pos0):
    sk = k_ref.shape[1]
    q_start = q_pos0 + pl.program_id(1) * tq
    n_full, n_any = _kv_tile_counts(q_start + 1, q_start + tq, tk, kv_len, sk)
    lane = lax.broadcasted_iota(jnp.int32, (tq, LANES), 1)
    n_pairs = H_FOX * HD_FOX // LANES
    q_pos = q_start + lax.broadcasted_iota(jnp.int32, (2 * tq, 1), 0) % tq
    q2 = []
    for j in range(n_pairs):
        q_pair = q_ref[:, j * LANES:(j + 1) * LANES]
        zero = jnp.zeros_like(q_pair)
        q2.append(jnp.concatenate([jnp.where(lane < HD_FOX, q_pair, zero),
                                   jnp.where(lane < HD_FOX, zero, q_pair)], axis=0))
        for half in range(2):
            col = jnp.sum(jnp.where(lane == 2 * j + half, cq_ref[0], 0.0), axis=-1, keepdims=True)
            cqb_ref[j, half * tq:(half + 1) * tq, :] = jnp.broadcast_to(col, (tq, LANES))

    def consume(kt, masked):
        k0 = pl.multiple_of(kt * tk, tk)
        if masked:
            k_pos = k0 + lax.broadcasted_iota(jnp.int32, (1, tk), 1)
            mask = k_pos <= q_pos
            if kv_len < sk:
                mask = jnp.logical_and(mask, k_pos < kv_len)
        for j in range(n_pairs):
            s = _dot_nt(q2[j], k_ref[0, pl.ds(k0, tk), j * LANES:(j + 1) * LANES])
            c_k = jnp.concatenate(
                [jnp.broadcast_to(ck_ref[0, 2 * j:2 * j + 1, pl.ds(k0, tk)], (tq, tk)),
                 jnp.broadcast_to(ck_ref[0, 2 * j + 1:2 * j + 2, pl.ds(k0, tk)], (tq, tk))], axis=0)
            s = s - c_k
            if masked:
                s = jnp.where(mask, s, NEG_INF)
            v_ones = v_ref[0, pl.ds(k0, tk), 2 * j * LANES:2 * (j + 1) * LANES]
            _online_softmax_step(s, v_ones, m_ref.at[j], acc_ref.at[j], shift=cqb_ref[j])

    _init_softmax_state(m_ref, acc_ref)
    _kv_loop(n_full, n_any, consume)
    for j in range(n_pairs):
        o = _normalised(acc_ref[j])
        o_ref[:, j * LANES:(j + 1) * LANES] = jnp.where(lane < HD_FOX, o[:tq], o[tq:]).astype(BF16)


def _fox_attention(fq, fkb, fvb, c_q, c_k, batch, sq, tq, tk, kv_len, q_pos0):
    sk = fkb.shape[1]
    nq = sq // tq
    w = H_FOX * HD_FOX
    kern = functools.partial(_fox_kernel, tq=tq, tk=tk, kv_len=kv_len, q_pos0=q_pos0)
    return pl.pallas_call(
        kern,
        grid=(batch, nq),
        in_specs=[pl.BlockSpec((tq, w), lambda b, i: (b * nq + i, 0)),
                  pl.BlockSpec((1, sk, w), lambda b, i: (b, 0, 0)),
                  pl.BlockSpec((1, sk, 2 * w), lambda b, i: (b, 0, 0)),
                  pl.BlockSpec((1, tq, LANES), lambda b, i: (b, i, 0)),
                  pl.BlockSpec((1, H_FOX, sk), lambda b, i: (b, 0, 0))],
        out_specs=pl.BlockSpec((tq, w), lambda b, i: (b * nq + i, 0)),
        out_shape=jax.ShapeDtypeStruct((batch * sq, w), BF16),
        scratch_shapes=[pltpu.VMEM((w // LANES, 2 * tq, LANES), F32),
                        pltpu.VMEM((w // LANES, 2 * tq, 2 * LANES), F32),
                        pltpu.VMEM((w // LANES, 2 * tq, LANES), F32)],
        compiler_params=_params(2),
    )(fq, fkb, fvb, c_q, c_k)


def _fox_cached_kernel(q_ref, ck_ref, cv_ref, kn_ref, vn_ref, cq_ref, crow_ref, o_ref, *, past):
    tq = q_ref.shape[0]
    pad = jnp.zeros((LANES - tq, HD_FOX), BF16)
    q_idx = lax.broadcasted_iota(jnp.int32, (tq, LANES), 0)
    k_idx = lax.broadcasted_iota(jnp.int32, (tq, LANES), 1)
    visible = k_idx <= q_idx
    for h in range(H_FOX):
        cols = slice(h * HD_FOX, (h + 1) * HD_FOX)
        q = q_ref[:, cols]
        c_q = cq_ref[0, :, h:h + 1]
        k_new = jnp.concatenate([kn_ref[:, cols].astype(BF16), pad], axis=0)
        v_new = jnp.concatenate([vn_ref[:, cols].astype(BF16), pad], axis=0)
        head_rows = pl.ds(h, past, stride=H_FOX)
        s_old = (_dot_nt(q, ck_ref[0, head_rows, :].astype(BF16))
                 + (c_q - crow_ref[0, h:h + 1, 0:past]))
        s_new = _dot_nt(q, k_new) + (c_q - crow_ref[0, h:h + 1, past:past + LANES])
        s_new = jnp.where(visible, s_new, NEG_INF)
        m = jnp.maximum(jnp.max(s_old, axis=-1, keepdims=True), jnp.max(s_new, axis=-1, keepdims=True))
        p_old = jnp.exp2(s_old - m)
        p_new = jnp.exp2(s_new - m)
        l = jnp.sum(p_old, axis=-1, keepdims=True) + jnp.sum(p_new, axis=-1, keepdims=True)
        o = (_dot(p_old.astype(BF16), cv_ref[0, head_rows, :].astype(BF16))
             + _dot(p_new.astype(BF16), v_new))
        o_ref[:, cols] = (o / l).astype(BF16)


def _fox_cached_attention(fq, cache_k, cache_v, k_new, v_new, c_q, c_row, dec_seq):
    batch, past = cache_k.shape[:2]
    w = H_FOX * HD_FOX
    assert past % LANES == 0 and dec_seq <= LANES and c_row.shape[2] >= past + LANES
    new = pl.BlockSpec((dec_seq, w), lambda b: (b, 0))
    cache_k = cache_k.reshape(batch, past * H_FOX, HD_FOX)
    cache_v = cache_v.reshape(batch, past * H_FOX, HD_FOX)
    old = pl.BlockSpec((1, past * H_FOX, HD_FOX), lambda b: (b, 0, 0))
    return pl.pallas_call(
        functools.partial(_fox_cached_kernel, past=past),
        grid=(batch,),
        in_specs=[new, old, old, new, new,
                  pl.BlockSpec((1, dec_seq, LANES), lambda b: (b, 0, 0)),
                  pl.BlockSpec((1, H_FOX, c_row.shape[2]), lambda b: (b, 0, 0))],
        out_specs=new,
        out_shape=jax.ShapeDtypeStruct((batch * dec_seq, w), BF16),
        compiler_params=_params(1),
    )(fq, cache_k, cache_v, k_new, v_new, c_q, c_row)


def _mem_kernel(q_ref, k_ref, v_ref, o_ref):
    for h in range(H_MEM):
        cols = slice(h * HD_MEM, (h + 1) * HD_MEM)
        s = _dot_nt(q_ref[:, cols], k_ref[0, :, cols]) * (MEM_SCALE * LOG2E)
        m = jnp.max(s, axis=-1, keepdims=True)
        p = jnp.exp2(s - m)
        l = jnp.sum(p, axis=-1, keepdims=True)
        o = _dot(p.astype(BF16), v_ref[0, :, cols]) / l
        o_ref[:, cols] = o.astype(BF16)


def _mem_attention(mq, mkb, mvb, batch, sq, tq):
    nq = sq // tq
    n_mem = mkb.shape[1]
    w = H_MEM * HD_MEM
    return pl.pallas_call(
        _mem_kernel,
        grid=(batch, nq),
        in_specs=[pl.BlockSpec((tq, w), lambda b, i: (b * nq + i, 0)),
                  pl.BlockSpec((1, n_mem, w), lambda b, i: (b, 0, 0)),
                  pl.BlockSpec((1, n_mem, w), lambda b, i: (b, 0, 0))],
        out_specs=pl.BlockSpec((tq, w), lambda b, i: (b * nq + i, 0)),
        out_shape=jax.ShapeDtypeStruct((batch * sq, w), BF16),
        compiler_params=_params(2),
    )(mq, mkb, mvb)


def _route(x1b, wr_t, bias):
    tm = x1b.shape[0]
    scores = _sigmoid(_dot_nt(wr_t, x1b))
    sel = scores + bias
    sub = lax.broadcasted_iota(jnp.int32, (GROUP_SIZE, tm), 0)
    gs = []
    for g in range(N_GROUPS):
        blk = sel[g * GROUP_SIZE:(g + 1) * GROUP_SIZE, :]
        m1 = jnp.max(blk, axis=0, keepdims=True)
        first = jnp.min(jnp.where(blk == m1, sub, GROUP_SIZE), axis=0, keepdims=True)
        m2 = jnp.max(jnp.where(sub == first, -jnp.inf, blk), axis=0, keepdims=True)
        gs.append(m1 + m2)
    e_idx = lax.broadcasted_iota(jnp.int32, (N_EXPERTS, tm), 0)
    allowed = jnp.zeros((N_EXPERTS, tm), jnp.bool_)
    for g in range(N_GROUPS):
        rank = jnp.zeros((1, tm), jnp.int32)
        for o in range(N_GROUPS):
            if o == g:
                continue
            beats = (gs[o] >= gs[g]) if o < g else (gs[o] > gs[g])
            rank = rank + beats.astype(jnp.int32)
        keep = rank < TOPK_GROUPS
        allowed = jnp.logical_or(allowed, jnp.logical_and(e_idx // GROUP_SIZE == g, keep))
    cand = jnp.where(allowed, sel, -jnp.inf)
    chosen = jnp.zeros((N_EXPERTS, tm), jnp.bool_)
    for _ in range(TOP_K):
        m = jnp.max(cand, axis=0, keepdims=True)
        first = jnp.min(jnp.where(cand == m, e_idx, N_EXPERTS), axis=0, keepdims=True)
        pick = e_idx == first
        chosen = jnp.logical_or(chosen, pick)
        cand = jnp.where(pick, -jnp.inf, cand)
    w = jnp.where(chosen, scores, 0.0)
    return w / jnp.sum(w, axis=0, keepdims=True) * ROUTED_SCALE, chosen


def _swiglu(xb, wg, wu):
    a = _dot(xb, wg)
    return (a * _sigmoid(a)) * _dot(xb, wu)


def _merge_kernel(x_ref, olat_ref, ofox_ref, omem_ref, wg_ref, wuv_ref, wbm_ref, wbf_ref, wbc_ref,
                  wo_ref, g1_ref, b1_ref, wsg_ref, wsu_ref, wsd_ref, x1b_ref, base_ref):
    x = x_ref[...]
    xb = x.astype(BF16)
    o_mla = jnp.concatenate(
        [_dot(olat_ref[:, j * 2 * KV_LORA:(j + 1) * 2 * KV_LORA], wuv_ref[j]).astype(BF16)
         for j in range(H_MLA // 2)], axis=-1)

    def gate(n):
        return _sigmoid(_dot(xb, wg_ref[:, n * D_MODEL:(n + 1) * D_MODEL]))

    y = gate(0) * _dot(o_mla, wbm_ref[...])
    y = y + gate(1) * _dot(ofox_ref[...], wbf_ref[...])
    y = y + gate(2) * _dot(omem_ref[...], wbc_ref[...])
    mix = _dot(y.astype(BF16), wo_ref[...])
    x1 = _layer_norm(ALPHA * x + mix, g1_ref[...], b1_ref[...])
    x1b = x1.astype(BF16)
    x1b_ref[...] = x1b
    shared = _dot(_swiglu(x1b, wsg_ref[...], wsu_ref[...]).astype(BF16), wsd_ref[...])
    base_ref[...] = ALPHA * x1 + shared


def _merge(x, olat, ofox, omem, wg, wuv, wbm, wbf, wbc, wo, g1, b1, wsg, wsu, wsd, tm):
    t = x.shape[0]
    row = lambda n: pl.BlockSpec((tm, n), lambda i: (i, 0))
    ws = (wg, wuv, wbm, wbf, wbc, wo, g1, b1, wsg, wsu, wsd)
    return pl.pallas_call(
        _merge_kernel,
        grid=(t // tm,),
        in_specs=[row(D_MODEL), row(H_MLA * KV_LORA), row(H_FOX * HD_FOX), row(H_MEM * HD_MEM)]
        + [_full(w.shape) for w in ws],
        out_specs=(row(D_MODEL), row(D_MODEL)),
        out_shape=(jax.ShapeDtypeStruct((t, D_MODEL), BF16), jax.ShapeDtypeStruct((t, D_MODEL), F32)),
        compiler_params=_params(1),
    )(x, olat, ofox, omem, *ws)


BF16_ROWS = 16
PLACE_ROWS = 512
EXPERT_ROWS = 4096
EXPERT_CHUNK = 512
EXPERTS_PER_STEP = 4


def _capacity(tm):
    return min(tm, _round_up(2 * tm * TOP_K // N_EXPERTS, BF16_ROWS))


def _dispatch_kernel(x1b_ref, wr_ref, rb_ref, xs_ref, gate_ref, slot_ref, govf_ref, flag_ref,
                     pm_ref, *, tm, cap):
    n_tiles = x1b_ref.shape[0] // tm
    slots = []
    for h in range(n_tiles):
        rows = slice(h * tm, (h + 1) * tm)
        xb = x1b_ref[rows, :]
        gate, chosen = _route(xb, wr_ref[...], rb_ref[...])
        r = lax.broadcasted_iota(jnp.int32, (tm, tm), 0)
        c = lax.broadcasted_iota(jnp.int32, (tm, tm), 1)
        earlier = jnp.where(r < c, 1.0, 0.0).astype(BF16)
        rank = _dot(jnp.where(chosen, 1.0, 0.0).astype(BF16), earlier)
        fits = jnp.logical_and(chosen, rank < cap)
        slot = jnp.where(fits, rank, -1.0)
        gate_ref[:, rows] = jnp.where(fits, gate, 0.0)
        slot_ref[:, rows] = slot
        g_ovf = jnp.where(fits, 0.0, gate)
        govf_ref[rows, :] = jnp.concatenate(
            [g_ovf, jnp.zeros((LANES - N_EXPERTS, tm), F32)], axis=0).T
        any_ovf = jnp.max(jnp.max(g_ovf, axis=0, keepdims=True), axis=1, keepdims=True)
        flag_ref[h] = jnp.broadcast_to(jnp.where(any_ovf > 0.0, 1, 0),
                                       flag_ref.shape[1:]).astype(jnp.int32)
        slots.append(slot)
    s_iota = lax.broadcasted_iota(jnp.int32, (cap, tm), 0).astype(F32)
    for h in range(n_tiles):
        slot = slots[h]
        xb = x1b_ref[h * tm:(h + 1) * tm, :]
        for e in range(N_EXPERTS):
            pm_ref[h, e * cap:(e + 1) * cap, :] = jnp.where(
                s_iota == slot[e:e + 1, :], 1.0, 0.0).astype(BF16)
        per = PLACE_ROWS // cap
        for g in range(N_EXPERTS // per):
            xs = _dot(pm_ref[h, g * PLACE_ROWS:(g + 1) * PLACE_ROWS, :], xb).astype(BF16)
            xs_ref[g * per:(g + 1) * per, h * cap:(h + 1) * cap, :] = xs.reshape(per, cap, D_MODEL)


def _dispatch(x1b, wr_t, rb, tm, cap):
    t = x1b.shape[0]
    nt = t // tm
    ts = 2 if nt % 2 == 0 else 1
    col = pl.BlockSpec((N_EXPERTS, ts * tm), lambda i: (0, i))
    return pl.pallas_call(
        functools.partial(_dispatch_kernel, tm=tm, cap=cap),
        grid=(nt // ts,),
        in_specs=[pl.BlockSpec((ts * tm, D_MODEL), lambda i: (i, 0)), _full(wr_t.shape),
                  _full(rb.shape)],
        out_specs=(pl.BlockSpec((N_EXPERTS, ts * cap, D_MODEL), lambda i: (0, i, 0)), col, col,
                   pl.BlockSpec((ts * tm, LANES), lambda i: (i, 0)),
                   pl.BlockSpec((ts, 8, LANES), lambda i: (i, 0, 0))),
        out_shape=(jax.ShapeDtypeStruct((N_EXPERTS, nt * cap, D_MODEL), BF16),
                   jax.ShapeDtypeStruct((N_EXPERTS, t), F32),
                   jax.ShapeDtypeStruct((N_EXPERTS, t), F32),
                   jax.ShapeDtypeStruct((t, LANES), F32),
                   jax.ShapeDtypeStruct((nt, 8, LANES), jnp.int32)),
        scratch_shapes=[pltpu.VMEM((ts, N_EXPERTS * cap, tm), BF16)],
        compiler_params=_params(1),
    )(x1b, wr_t, rb)


def _expert_kernel(xs_ref, xs2_ref, wg_ref, wu_ref, wd_ref, ys_ref, ys2_ref, wgb_ref, wub_ref,
                   wdb_ref, *, chunk):
    def ffn(x):
        hid = _swiglu(x, wgb_ref[...], wub_ref[...])
        return _dot(hid.astype(BF16), wdb_ref[...]).astype(BF16)

    @pl.when(pl.program_id(1) == 0)
    def _():
        wgb_ref[...] = wg_ref[0].astype(BF16)
        wub_ref[...] = wu_ref[0].astype(BF16)
        wdb_ref[...] = wd_ref[0].astype(BF16)
        ys2_ref[0] = ffn(xs2_ref[0])

    def body(c, carry):
        r0 = pl.multiple_of(c * chunk, chunk)
        ys_ref[0, pl.ds(r0, chunk), :] = ffn(xs_ref[0, pl.ds(r0, chunk), :])
        return carry

    lax.fori_loop(0, xs_ref.shape[1] // chunk, body, 0)


def _experts(xs, xs2, weg, weu, wed):
    n_e, rows, _ = xs.shape
    rb = min(EXPERT_ROWS, rows)
    blk = pl.BlockSpec((1, rb, D_MODEL), lambda e, i: (e, i, 0))
    blk2 = pl.BlockSpec((1, xs2.shape[1], D_MODEL), lambda e, i: (e, 0, 0))
    return pl.pallas_call(
        functools.partial(_expert_kernel, chunk=min(EXPERT_CHUNK, rb)),
        grid=(n_e, rows // rb),
        in_specs=[blk, blk2,
                  pl.BlockSpec((1, D_MODEL, D_EXPERT), lambda e, i: (e, 0, 0)),
                  pl.BlockSpec((1, D_MODEL, D_EXPERT), lambda e, i: (e, 0, 0)),
                  pl.BlockSpec((1, D_EXPERT, D_MODEL), lambda e, i: (e, 0, 0))],
        out_specs=(blk, blk2),
        out_shape=(jax.ShapeDtypeStruct(xs.shape, BF16), jax.ShapeDtypeStruct(xs2.shape, BF16)),
        scratch_shapes=[pltpu.VMEM((D_MODEL, D_EXPERT), BF16), pltpu.VMEM((D_MODEL, D_EXPERT), BF16),
                        pltpu.VMEM((D_EXPERT, D_MODEL), BF16)],
        compiler_params=_params(2),
    )(xs, xs2, weg, weu, wed)


def _dense_routed_kernel(flag_ref, x1b_ref, gate_ref, weg_ref, weu_ref, wed_ref, out_ref):
    step = pl.program_id(1)

    @pl.when(step == 0)
    def _():
        out_ref[...] = jnp.zeros_like(out_ref)

    @pl.when(flag_ref[pl.program_id(0)] > 0)
    def _():
        xb = x1b_ref[...]
        gate = gate_ref[...]
        lane = lax.broadcasted_iota(jnp.int32, gate.shape, 1)
        for j in range(EXPERTS_PER_STEP):
            e = step * EXPERTS_PER_STEP + j
            g_col = jnp.sum(jnp.where(lane == e, gate, 0.0), axis=-1, keepdims=True)
            hid = _swiglu(xb, weg_ref[j], weu_ref[j])
            out_ref[...] += g_col * _dot(hid.astype(BF16), wed_ref[j])


def _dense_routed(flags, x1b, gate, weg, weu, wed, tm):
    t = x1b.shape[0]
    eb = EXPERTS_PER_STEP
    row = lambda n: pl.BlockSpec((tm, n), lambda i, s, f: (i, 0))
    wspec = lambda a, b: pl.BlockSpec((eb, a, b), lambda i, s, f: (jnp.where(f[i] > 0, s, 0), 0, 0))
    return pl.pallas_call(
        _dense_routed_kernel,
        grid_spec=pltpu.PrefetchScalarGridSpec(
            num_scalar_prefetch=1,
            grid=(t // tm, N_EXPERTS // eb),
            in_specs=[row(D_MODEL), row(LANES), wspec(D_MODEL, D_EXPERT), wspec(D_MODEL, D_EXPERT),
                      wspec(D_EXPERT, D_MODEL)],
            out_specs=row(D_MODEL)),
        out_shape=jax.ShapeDtypeStruct((t, D_MODEL), F32),
        compiler_params=_params(2),
    )(flags, x1b, gate, weg, weu, wed)


def _combine_kernel(gate_ref, slot_ref, ys_ref, base_ref, *rest, cap, has_overflow):
    if has_overflow:
        rovf_ref, g2_ref, b2_ref, y_ref, pw_ref = rest
    else:
        g2_ref, b2_ref, y_ref, pw_ref = rest
    tm = base_ref.shape[0]
    gate = gate_ref[...]
    slot = slot_ref[...]
    s_iota = lax.broadcasted_iota(jnp.int32, (cap, tm), 0).astype(F32)
    for e in range(N_EXPERTS):
        pw_ref[e * cap:(e + 1) * cap, :] = jnp.where(
            s_iota == slot[e:e + 1, :], gate[e:e + 1, :], 0.0).astype(BF16)
    routed = rovf_ref[...] if has_overflow else jnp.zeros((tm, D_MODEL), F32)
    per = PLACE_ROWS // cap
    for g in range(N_EXPERTS // per):
        ys = ys_ref[g * per:(g + 1) * per].reshape(PLACE_ROWS, D_MODEL)
        routed = routed + lax.dot_general(pw_ref[g * PLACE_ROWS:(g + 1) * PLACE_ROWS, :], ys,
                                          (((0,), (0,)), ((), ())), preferred_element_type=F32)
    y_ref[...] = _layer_norm(base_ref[...] + routed, g2_ref[...], b2_ref[...])


def _combine(gate_t, slot_t, ys, base, rovf, g2, b2, tm, cap):
    t = base.shape[0]
    col = pl.BlockSpec((N_EXPERTS, tm), lambda i: (0, i))
    row = pl.BlockSpec((tm, D_MODEL), lambda i: (i, 0))
    has_overflow = rovf is not None
    operands = (gate_t, slot_t, ys, base) + ((rovf,) if has_overflow else ()) + (g2, b2)
    return pl.pallas_call(
        functools.partial(_combine_kernel, cap=cap, has_overflow=has_overflow),
        grid=(t // tm,),
        in_specs=[col, col, pl.BlockSpec((N_EXPERTS, cap, D_MODEL), lambda i: (0, i, 0)), row]
        + ([row] if has_overflow else []) + [_full(g2.shape), _full(b2.shape)],
        out_specs=row,
        out_shape=jax.ShapeDtypeStruct((t, D_MODEL), F32),
        scratch_shapes=[pltpu.VMEM((N_EXPERTS * cap, tm), BF16)],
        compiler_params=_params(1),
    )(*operands)


def _moe_place(x1b, wr_t, rb, tm):
    cap = _capacity(tm)
    assert PLACE_ROWS % cap == 0 and (N_EXPERTS * cap) % PLACE_ROWS == 0
    return _dispatch(x1b, wr_t, rb, tm, cap)


def _moe_ln2(placed, ys, x1b, base, weg, weu, wed, g2, b2, tm):
    cap = _capacity(tm)
    _, gate_t, slot_t, gate_ovf, flags = placed
    flags = flags[:, 0, 0]

    def with_overflow():
        rovf = _dense_routed(flags, x1b, gate_ovf, weg.astype(BF16), weu.astype(BF16),
                             wed.astype(BF16), tm)
        return _combine(gate_t, slot_t, ys, base, rovf, g2, b2, tm, cap)

    return lax.cond(jnp.any(flags > 0), with_overflow,
                    lambda: _combine(gate_t, slot_t, ys, base, None, g2, b2, tm, cap))


def _rot_half_cols(w):
    half = w.shape[-1] // 2
    return jnp.concatenate([-w[..., half:], w[..., :half]], axis=-1)


def _block_diag2(a, b):
    za = jnp.zeros((a.shape[0], b.shape[1]), a.dtype)
    zb = jnp.zeros((b.shape[0], a.shape[1]), a.dtype)
    return jnp.concatenate([jnp.concatenate([a, za], 1), jnp.concatenate([zb, b], 1)], 0)


def _prep_weights(w_in, b_f, q_norm_g, w_q_up, kv_norm_g, w_kv_up):
    widths = (Q_LORA, KV_LORA, ROPE_DIM, H_FOX * HD_FOX, H_FOX * HD_FOX, H_FOX * HD_FOX, H_FOX,
              H_MEM * HD_MEM, N_BRANCH * D_MODEL)
    offs = np.cumsum((0,) + widths)
    w_ql, w_ckv, w_kr, w_fq, w_fk, w_fv, w_fl, w_mq, w_g = (
        w_in[:, offs[i]:offs[i + 1]] for i in range(len(widths)))
    rep = LANES // ROPE_DIM
    w_main = jnp.concatenate(
        [w_ql, w_ckv, jnp.tile(w_kr, (1, rep)), jnp.tile(_rot_half_cols(w_kr), (1, rep)),
         w_fq, w_fk, w_fv, w_mq, jnp.pad(w_fl, ((0, 0), (0, LANES - H_FOX)))], axis=1).astype(BF16)
    assert w_main.shape[1] == C_END
    wq3 = w_q_up.reshape(Q_LORA, H_MLA, NOPE_DIM + ROPE_DIM)
    wq_nope = wq3[:, :, :NOPE_DIM].reshape(Q_LORA, H_MLA * NOPE_DIM)
    wq_rope = wq3[:, :, NOPE_DIM:]
    w_q = jnp.concatenate(
        [wq_nope, wq_rope.reshape(Q_LORA, -1), _rot_half_cols(wq_rope).reshape(Q_LORA, -1)],
        axis=1).astype(BF16)
    wkv3 = w_kv_up.reshape(KV_LORA, H_MLA, NOPE_DIM + V_DIM)
    w_uk = wkv3[:, :, :NOPE_DIM]
    w_uv = wkv3[:, :, NOPE_DIM:]
    w_abs = jnp.stack([_block_diag2(w_uk[:, 2 * j].T, w_uk[:, 2 * j + 1].T)
                       for j in range(H_MLA // 2)]).astype(BF16)
    w_uvp = jnp.stack([_block_diag2(w_uv[:, 2 * j], w_uv[:, 2 * j + 1])
                       for j in range(H_MLA // 2)]).astype(BF16)
    b_fp = jnp.pad(b_f, (0, LANES - H_FOX)).reshape(1, LANES)
    return (w_main, w_q, w_abs, q_norm_g.reshape(1, -1), kv_norm_g.reshape(1, -1), b_fp,
            w_g.astype(BF16), w_uvp)


def _rope_tables(pos):
    half = ROPE_DIM // 2
    inv_freq = ROPE_THETA ** (-jnp.arange(half, dtype=F32) / half)
    ang = pos.astype(F32)[:, None] * inv_freq[None, :]
    rep = 2 * LANES // ROPE_DIM
    cos = jnp.tile(jnp.concatenate([jnp.cos(ang)] * 2, axis=-1), (1, rep))
    sin = jnp.tile(jnp.concatenate([jnp.sin(ang)] * 2, axis=-1), (1, rep))
    return cos, sin


def _round_up(n, m):
    return (n + m - 1) // m * m


def kernel(x_prompt, x_sample, mem_prompt, cache_mla_latent, cache_mla_krope, cache_fox_k,
           cache_fox_v, cache_fox_logf, cache_mem_k, cache_mem_v, ln1_g, ln1_b, w_in, b_f,
           q_norm_g, w_q_up, kv_norm_g, w_kv_up, w_mem_k, w_mem_v, w_br_mla, w_br_fox,
           w_br_mem, w_out, ln2_g, ln2_b, w_router, router_bias, w_e_gate, w_e_up, w_e_down,
           w_s_gate, w_s_up, w_s_down):
    assert w_in.shape[0] == DEPTH == 1
    batch, seq, _ = x_prompt.shape
    dec_batch, dec_seq, _ = x_sample.shape
    past = cache_mla_latent.shape[2]
    n_mem = mem_prompt.shape[1]
    l = 0

    (w_main, w_q, w_abs, q_g, kv_g, b_fp, w_gate, w_uvp) = _prep_weights(
        w_in[l], b_f[l], q_norm_g[l], w_q_up[l], kv_norm_g[l], w_kv_up[l])
    merge_w = (w_gate, w_uvp, w_br_mla[l].astype(BF16), w_br_fox[l].astype(BF16),
               w_br_mem[l].astype(BF16), w_out[l].astype(BF16), ln1_g[l].reshape(1, -1),
               ln1_b[l].reshape(1, -1), w_s_gate[l].astype(BF16), w_s_up[l].astype(BF16),
               w_s_down[l].astype(BF16))
    route_w = (w_router[l].T.astype(BF16), router_bias[l].reshape(-1, 1))
    expert_w = (w_e_gate[l], w_e_up[l], w_e_down[l])
    ln2_w = (ln2_g[l].reshape(1, -1), ln2_b[l].reshape(1, -1))

    tp = batch * seq
    xp = x_prompt.reshape(tp, D_MODEL)
    tm_p = min(512, seq)
    tq_p = min(256, seq)
    tk_p = min(512, seq)
    cos_p, sin_p = _rope_tables(jnp.arange(seq, dtype=jnp.int32))
    (qcat, kcat, lat, latb, kr, fq, fk, fv, fkb, fvb, mq, logf, logfp) = _proj(
        xp, w_main, w_q, w_abs, q_g, kv_g, b_fp, cos_p, sin_p, tm_p)
    tmem = batch * n_mem
    m_k, m_v, m_kb, m_vb = _memkv(mem_prompt.reshape(tmem, D_MODEL), w_mem_k[l].astype(BF16),
                                  w_mem_v[l].astype(BF16), min(512, tmem))
    c_p = _cumsum(logfp.reshape(batch, seq, LANES))
    c_keys = jnp.swapaxes(c_p[:, :, :H_FOX], 1, 2)
    olat = _mla_attention(qcat, kcat.reshape(batch, seq, -1), latb.reshape(batch, seq, -1),
                          batch, seq, tq_p, tk_p, seq, 0)
    ofox = _fox_attention(fq, fkb.reshape(batch, seq, -1), fvb.reshape(batch, seq, -1), c_p, c_keys,
                          batch, seq, tk_p, tk_p, seq, 0)
    omem = _mem_attention(mq, m_kb.reshape(batch, n_mem, -1), m_vb.reshape(batch, n_mem, -1),
                          batch, seq, tk_p)
    x1b_p, base_p = _merge(xp, olat, ofox, omem, *merge_w, tm_p)
    tm_moe = min(256, seq)
    placed_p = _moe_place(x1b_p, *route_w, tm_moe)

    ts = dec_batch * dec_seq
    xs = x_sample.reshape(ts, D_MODEL)
    cos_s, sin_s = _rope_tables(jnp.tile(past + jnp.arange(dec_seq, dtype=jnp.int32), dec_batch))
    (qcat_s, kcat_s, lat_s, latb_s, kr_s, fq_s, fk_s, fv_s, fkb_s, fvb_s, mq_s, logf_s,
     logfp_s) = _proj(xs, w_main, w_q, w_abs, q_g, kv_g, b_fp, cos_s, sin_s, ts)
    kv_len = past + dec_seq
    tk_s = 1024
    sk = _round_up(kv_len, tk_s)

    def with_cache(cache, new):
        new = new.reshape(dec_batch, dec_seq, -1)
        pad = jnp.zeros((dec_batch, sk - kv_len, new.shape[-1]), new.dtype)
        return jnp.concatenate([cache.astype(new.dtype), new, pad], axis=1)

    rep = LANES // ROPE_DIM
    kcat_all = with_cache(
        jnp.concatenate([cache_mla_latent[l], jnp.tile(cache_mla_krope[l], (1, 1, rep))], axis=-1),
        kcat_s)
    lat_all = with_cache(
        jnp.concatenate([cache_mla_latent[l], jnp.ones((dec_batch, past, LANES), F32)], axis=-1), latb_s)
    logf_all = with_cache(jnp.pad(cache_fox_logf[l], ((0, 0), (0, 0), (0, LANES - H_FOX))), logfp_s)
    c_s = _cumsum(logf_all)
    c_keys_s = jnp.swapaxes(c_s[:, :, :H_FOX], 1, 2)
    c_q_s = c_s[:, past:kv_len]
    olat_s = _mla_attention(qcat_s, kcat_all, lat_all, dec_batch, dec_seq, dec_seq, tk_s, kv_len, past)
    ofox_s = _fox_cached_attention(fq_s, cache_fox_k[l], cache_fox_v[l], fk_s, fv_s, c_q_s, c_keys_s,
                                   dec_seq)
    omem_s = _mem_attention(mq_s, cache_mem_k[l].reshape(dec_batch, n_mem, -1).astype(BF16),
                            cache_mem_v[l].reshape(dec_batch, n_mem, -1).astype(BF16),
                            dec_batch, dec_seq, dec_seq)
    x1b_s, base_s = _merge(xs, olat_s, ofox_s, omem_s, *merge_w, ts)
    placed_s = _moe_place(x1b_s, *route_w, ts)

    ys_p, ys_s = _experts(placed_p[0], placed_s[0], *expert_w)
    y_p = _moe_ln2(placed_p, ys_p, x1b_p, base_p, *expert_w, *ln2_w, tm_moe)
    y_s = _moe_ln2(placed_s, ys_s, x1b_s, base_s, *expert_w, *ln2_w, ts)
    y_p = y_p.reshape(batch, seq, D_MODEL)
    y_s = y_s.reshape(dec_batch, dec_seq, D_MODEL)

    def stack(a, b, s, *tail_shape):
        return a.reshape(1, b, s, *tail_shape)

    return (y_p, y_s,
            stack(lat, batch, seq, KV_LORA), stack(kr, batch, seq, ROPE_DIM),
            stack(fk, batch, seq, H_FOX, HD_FOX), stack(fv, batch, seq, H_FOX, HD_FOX),
            stack(logf, batch, seq, H_FOX),
            stack(m_k, batch, n_mem, H_MEM, HD_MEM), stack(m_v, batch, n_mem, H_MEM, HD_MEM),
            stack(lat_s, dec_batch, dec_seq, KV_LORA), stack(kr_s, dec_batch, dec_seq, ROPE_DIM),
            stack(fk_s, dec_batch, dec_seq, H_FOX, HD_FOX), stack(fv_s, dec_batch, dec_seq, H_FOX, HD_FOX),
            stack(logf_s, dec_batch, dec_seq, H_FOX))
```

```python
import functools

import jax
import jax.numpy as jnp
import numpy as np
from jax import lax
from jax.experimental import pallas as pl
from jax.experimental.pallas import tpu as pltpu

F32 = jnp.float32
BF16 = jnp.bfloat16

D_MODEL = 1024
CHUNK = 64
H_MLA = 8
Q_LORA = 256
KV_LORA = 128
NOPE_DIM = 64
ROPE_DIM = 32
V_DIM = 64
ROPE_THETA = 10000.0
MLA_SCALE = (NOPE_DIM + ROPE_DIM) ** -0.5
H_FOX = 8
HD_FOX = 64
FOX_SCALE = HD_FOX ** -0.5
H_MEM = 4
HD_MEM = 128
MEM_SCALE = HD_MEM ** -0.5
N_BRANCH = 3
N_EXPERTS = 64
TOP_K = 8
N_GROUPS = 8
GROUP_SIZE = N_EXPERTS // N_GROUPS
TOPK_GROUPS = 4
D_EXPERT = 256
ROUTED_SCALE = 2.5
DEPTH = 1
ALPHA = (2 * DEPTH) ** 0.25
NORM_EPS = 1e-5
NEG_INF = -1e30

LOG2E = 1.4426950408889634
LANES = 128
VMEM_LIMIT = 56 * 1024 * 1024

C_QLAT = 0
C_CKV = C_QLAT + Q_LORA
C_KR = C_CKV + KV_LORA
C_KRR = C_KR + LANES
C_FQ = C_KRR + LANES
C_FK = C_FQ + H_FOX * HD_FOX
C_FV = C_FK + H_FOX * HD_FOX
C_MQ = C_FV + H_FOX * HD_FOX
C_FL = C_MQ + H_MEM * HD_MEM
C_END = C_FL + LANES


def _params(n_grid_axes):
    return pltpu.CompilerParams(
        dimension_semantics=("arbitrary",) * n_grid_axes, vmem_limit_bytes=VMEM_LIMIT)


def _full(shape):
    return pl.BlockSpec(shape, lambda *_: (0,) * len(shape))


def _dot(a, b):
    return jnp.dot(a, b, preferred_element_type=F32)


def _dot_nt(a, b):
    return lax.dot_general(a, b, (((1,), (1,)), ((), ())), preferred_element_type=F32)


def _sigmoid(x):
    return 1.0 / (1.0 + jnp.exp(-x))


def _rms(x, g):
    ms = jnp.mean(x * x, axis=-1, keepdims=True)
    return x * lax.rsqrt(ms + NORM_EPS) * g


def _layer_norm(x, g, b):
    mu = jnp.mean(x, axis=-1, keepdims=True)
    xc = x - mu
    var = jnp.mean(xc * xc, axis=-1, keepdims=True)
    return xc * lax.rsqrt(var + NORM_EPS) * g + b


def _proj_kernel(x_ref, w_ref, wq_ref, wabs_ref, qg_ref, kvg_ref, bf_ref, cos_ref, sin_ref,
                 qcat_ref, kcat_ref, lat_ref, latb_ref, kr_ref, fq_ref, fk_ref, fv_ref,
                 fkb_ref, fvb_ref, mq_ref, logf_ref, logfp_ref):
    xb = x_ref[...].astype(BF16)
    tm = xb.shape[0]

    def proj(c0, width):
        return _dot(xb, w_ref[:, c0:c0 + width])

    cos = cos_ref[...]
    sin = sin_ref[...]
    qn = _rms(proj(C_QLAT, Q_LORA), qg_ref[...]).astype(BF16)
    q2 = _dot(qn, wq_ref[...])
    nq = H_MLA * NOPE_DIM
    nr = H_MLA * ROPE_DIM
    q_rope = (q2[:, nq:nq + nr] * cos + q2[:, nq + nr:nq + 2 * nr] * sin) * (MLA_SCALE * LOG2E)
    lane = lax.broadcasted_iota(jnp.int32, (tm, LANES), 1)
    heads_per_group = LANES // ROPE_DIM
    for j in range(H_MLA // 2):
        q_abs = _dot(q2[:, j * LANES:(j + 1) * LANES].astype(BF16), wabs_ref[j]) * (MLA_SCALE * LOG2E)
        for half in range(2):
            h = 2 * j + half
            grp = h // heads_per_group
            qr = q_rope[:, grp * LANES:(grp + 1) * LANES]
            qr = jnp.where(lane // ROPE_DIM == h % heads_per_group, qr, 0.0)
            qcat_ref[h, :, 0:LANES] = q_abs[:, half * LANES:(half + 1) * LANES].astype(BF16)
            qcat_ref[h, :, LANES:2 * LANES] = qr.astype(BF16)
    lat = _rms(proj(C_CKV, KV_LORA), kvg_ref[...])
    lat_ref[...] = lat
    ones = jnp.ones((tm, LANES), BF16)
    latb_ref[:, 0:KV_LORA] = lat.astype(BF16)
    latb_ref[:, KV_LORA:KV_LORA + LANES] = ones
    kr = proj(C_KR, LANES) * cos[:, :LANES] + proj(C_KRR, LANES) * sin[:, :LANES]
    kr_ref[...] = kr[:, :ROPE_DIM]
    kcat_ref[:, 0:KV_LORA] = lat.astype(BF16)
    kcat_ref[:, KV_LORA:KV_LORA + LANES] = kr.astype(BF16)
    fq_ref[...] = (proj(C_FQ, H_FOX * HD_FOX) * (FOX_SCALE * LOG2E)).astype(BF16)
    fk = proj(C_FK, H_FOX * HD_FOX)
    fk_ref[...] = fk
    fkb_ref[...] = fk.astype(BF16)
    fv = proj(C_FV, H_FOX * HD_FOX)
    fv_ref[...] = fv
    for j in range(H_FOX * HD_FOX // LANES):
        fvb_ref[:, 2 * j * LANES:(2 * j + 1) * LANES] = fv[:, j * LANES:(j + 1) * LANES].astype(BF16)
        fvb_ref[:, (2 * j + 1) * LANES:(2 * j + 2) * LANES] = ones
    mq_ref[...] = proj(C_MQ, H_MEM * HD_MEM).astype(BF16)
    z = proj(C_FL, LANES) + bf_ref[...]
    log_f = jnp.minimum(z, 0.0) - jnp.log1p(jnp.exp(-jnp.abs(z)))
    logfp_ref[...] = log_f
    logf_ref[...] = log_f[:, :H_FOX]


def _proj(x, w_main, w_q, w_abs, q_g, kv_g, b_f, cos_t, sin_t, tm):
    t = x.shape[0]
    n_tab = cos_t.shape[0] // tm
    row = lambda w: pl.BlockSpec((tm, w), lambda i: (i, 0))
    tab = pl.BlockSpec((tm, 2 * LANES), lambda i: (i % n_tab, 0))
    hw = H_FOX * HD_FOX
    out_shape = (
        jax.ShapeDtypeStruct((H_MLA, t, 2 * LANES), BF16),
        jax.ShapeDtypeStruct((t, 2 * LANES), BF16),
        jax.ShapeDtypeStruct((t, KV_LORA), F32),
        jax.ShapeDtypeStruct((t, KV_LORA + LANES), BF16),
        jax.ShapeDtypeStruct((t, ROPE_DIM), F32),
        jax.ShapeDtypeStruct((t, hw), BF16),
        jax.ShapeDtypeStruct((t, hw), F32),
        jax.ShapeDtypeStruct((t, hw), F32),
        jax.ShapeDtypeStruct((t, hw), BF16),
        jax.ShapeDtypeStruct((t, 2 * hw), BF16),
        jax.ShapeDtypeStruct((t, H_MEM * HD_MEM), BF16),
        jax.ShapeDtypeStruct((t, H_FOX), F32),
        jax.ShapeDtypeStruct((t, LANES), F32),
    )
    out_specs = (
        pl.BlockSpec((H_MLA, tm, 2 * LANES), lambda i: (0, i, 0)),
        row(2 * LANES), row(KV_LORA), row(KV_LORA + LANES), row(ROPE_DIM), row(hw), row(hw), row(hw),
        row(hw), row(2 * hw), row(H_MEM * HD_MEM), row(H_FOX), row(LANES),
    )
    return pl.pallas_call(
        _proj_kernel,
        grid=(t // tm,),
        in_specs=[row(D_MODEL), _full(w_main.shape), _full(w_q.shape), _full(w_abs.shape),
                  _full(q_g.shape), _full(kv_g.shape), _full(b_f.shape), tab, tab],
        out_specs=out_specs,
        out_shape=out_shape,
        compiler_params=_params(1),
    )(x, w_main, w_q, w_abs, q_g, kv_g, b_f, cos_t, sin_t)


def _memkv_kernel(x_ref, wk_ref, wv_ref, k_ref, v_ref, kb_ref, vb_ref):
    xb = x_ref[...].astype(BF16)
    k = _dot(xb, wk_ref[...])
    v = _dot(xb, wv_ref[...])
    k_ref[...] = k
    v_ref[...] = v
    kb_ref[...] = k.astype(BF16)
    vb_ref[...] = v.astype(BF16)


def _memkv(x, wk, wv, tm):
    t = x.shape[0]
    w = H_MEM * HD_MEM
    row = lambda n: pl.BlockSpec((tm, n), lambda i: (i, 0))
    return pl.pallas_call(
        _memkv_kernel,
        grid=(t // tm,),
        in_specs=[row(D_MODEL), _full(wk.shape), _full(wv.shape)],
        out_specs=(row(w), row(w), row(w), row(w)),
        out_shape=(jax.ShapeDtypeStruct((t, w), F32), jax.ShapeDtypeStruct((t, w), F32),
                   jax.ShapeDtypeStruct((t, w), BF16), jax.ShapeDtypeStruct((t, w), BF16)),
        compiler_params=_params(1),
    )(x, wk, wv)


CUM_BLOCK = 256


def _bf16_pieces(x):
    hi = x.astype(BF16)
    r1 = x - hi.astype(F32)
    mid = r1.astype(BF16)
    lo = (r1 - mid.astype(F32)).astype(BF16)
    return hi, mid, lo


def _cumsum_kernel(x_ref, c_ref):
    s = x_ref.shape[1]
    r = lax.broadcasted_iota(jnp.int32, (CUM_BLOCK, CUM_BLOCK), 0)
    c = lax.broadcasted_iota(jnp.int32, (CUM_BLOCK, CUM_BLOCK), 1)
    tri = jnp.where(r >= c, 1.0, 0.0).astype(BF16)

    def body(i, carry):
        r0 = pl.multiple_of(i * CUM_BLOCK, CUM_BLOCK)
        x = x_ref[0, pl.ds(r0, CUM_BLOCK), :]
        cs = sum(_dot(tri, piece) for piece in _bf16_pieces(x)) + carry
        c_ref[0, pl.ds(r0, CUM_BLOCK), :] = cs * LOG2E
        return cs[CUM_BLOCK - 1:CUM_BLOCK, :]

    lax.fori_loop(0, s // CUM_BLOCK, body, jnp.zeros((1, LANES), F32))


def _cumsum(x):
    b, s, _ = x.shape
    spec = pl.BlockSpec((1, s, LANES), lambda i: (i, 0, 0))
    return pl.pallas_call(
        _cumsum_kernel, grid=(b,), in_specs=[spec], out_specs=spec,
        out_shape=jax.ShapeDtypeStruct(x.shape, F32), compiler_params=_params(1),
    )(x)


def _online_softmax_step(s, v_ones, m_ref, acc_ref, shift=None):
    def lanes(x, width):
        return jnp.concatenate([x] * (width // LANES), axis=-1)

    m = m_ref[...]
    r = jnp.max(s, axis=-1, keepdims=True)
    m_new = jnp.maximum(m, r if shift is None else r + shift)
    alpha = jnp.exp2(m - m_new)
    p = jnp.exp2(s - lanes(m_new if shift is None else m_new - shift, s.shape[-1]))
    m_ref[...] = m_new
    acc_ref[...] = lanes(alpha, acc_ref.shape[-1]) * acc_ref[...] + _dot(p.astype(BF16), v_ones)


def _init_softmax_state(m_ref, acc_ref):
    m_ref[...] = jnp.full(m_ref.shape, NEG_INF, F32)
    acc_ref[...] = jnp.zeros(acc_ref.shape, F32)


def _normalised(acc):
    return acc[:, :LANES] / acc[:, LANES:]


def _kv_tile_counts(first_visible_end, last_visible_end, tk, kv_len, sk):
    n_full = jnp.minimum(first_visible_end // tk, kv_len // tk)
    n_any = jnp.minimum((jnp.minimum(last_visible_end, kv_len) + tk - 1) // tk, sk // tk)
    return n_full, n_any


def _kv_loop(n_full, n_any, consume):
    def body(kt, carry, masked):
        consume(kt, masked)
        return carry

    lax.fori_loop(0, n_full, functools.partial(body, masked=False), 0)
    lax.fori_loop(n_full, n_any, functools.partial(body, masked=True), 0)


def _kv_loop_pipelined(n_any, produce, consume):
    produce(0, 0)
    n_pairs = (n_any - 1) // 2

    def body(i, carry):
        kt = 2 * i
        produce(1, kt + 1)
        consume(0, kt, False)
        produce(0, kt + 2)
        consume(1, kt + 1, False)
        return carry

    lax.fori_loop(0, n_pairs, body, 0)
    kt0 = 2 * n_pairs

    @pl.when(kt0 == n_any - 1)
    def _():
        consume(0, kt0, True)

    @pl.when(kt0 < n_any - 1)
    def _():
        produce(1, kt0 + 1)
        consume(0, kt0, False)
        consume(1, kt0 + 1, True)


def _mla_kernel(q_ref, k_ref, v_ref, o_ref, m_ref, acc_ref, s_ref, *, tq, tk, kv_len, q_pos0):
    sk = k_ref.shape[1]
    rows = H_MLA * tq
    q_start = q_pos0 + pl.program_id(1) * tq
    _, n_any = _kv_tile_counts((q_start // CHUNK + 1) * CHUNK,
                               ((q_start + tq - 1) // CHUNK + 1) * CHUNK, tk, kv_len, sk)
    q = q_ref[...].reshape(rows, q_ref.shape[-1])
    q_chunk = (q_start + lax.broadcasted_iota(jnp.int32, (rows, 1), 0) % tq) // CHUNK

    def produce(slot, kt):
        s_ref[slot] = _dot_nt(q, k_ref[0, pl.ds(pl.multiple_of(kt * tk, tk), tk), :])

    def consume(slot, kt, masked):
        k0 = pl.multiple_of(kt * tk, tk)
        s = s_ref[slot]
        if masked:
            k_pos = k0 + lax.broadcasted_iota(jnp.int32, (1, tk), 1)
            mask = k_pos // CHUNK <= q_chunk
            if kv_len < sk:
                mask = jnp.logical_and(mask, k_pos < kv_len)
            s = jnp.where(mask, s, NEG_INF)
        _online_softmax_step(s, v_ref[0, pl.ds(k0, tk), :], m_ref, acc_ref)

    _init_softmax_state(m_ref, acc_ref)
    _kv_loop_pipelined(n_any, produce, consume)
    o = _normalised(acc_ref[...]).astype(BF16)
    for h in range(H_MLA):
        o_ref[:, h * KV_LORA:(h + 1) * KV_LORA] = o[h * tq:(h + 1) * tq, :]


def _mla_attention(qcat, kcat, latb, batch, sq, tq, tk, kv_len, q_pos0):
    sk = kcat.shape[1]
    nq = sq // tq
    assert tk % tq == 0 and q_pos0 % tq == 0 and tq % CHUNK in (0, tq)
    kern = functools.partial(_mla_kernel, tq=tq, tk=tk, kv_len=kv_len, q_pos0=q_pos0)
    return pl.pallas_call(
        kern,
        grid=(batch, nq),
        in_specs=[pl.BlockSpec((H_MLA, tq, 2 * LANES), lambda b, i: (0, b * nq + i, 0)),
                  pl.BlockSpec((1, sk, 2 * LANES), lambda b, i: (b, 0, 0)),
                  pl.BlockSpec((1, sk, 2 * LANES), lambda b, i: (b, 0, 0))],
        out_specs=pl.BlockSpec((tq, H_MLA * KV_LORA), lambda b, i: (b * nq + i, 0)),
        out_shape=jax.ShapeDtypeStruct((batch * sq, H_MLA * KV_LORA), BF16),
        scratch_shapes=[pltpu.VMEM((H_MLA * tq, LANES), F32),
                        pltpu.VMEM((H_MLA * tq, 2 * LANES), F32),
                        pltpu.VMEM((2, H_MLA * tq, tk), F32)],
        compiler_params=_params(2),
    )(qcat, kcat, latb)


def _fox_kernel(q_ref, k_ref, v_ref, cq_ref, ck_ref, o_ref, m_ref, acc_ref, cqb_ref, *, tq, tk,
                kv_len, q_pos0):
    sk = k_ref.shape[1]
    q_start = q_pos0 + pl.program_id(1) * tq
    n_full, n_any = _kv_tile_counts(q_start + 1, q_start + tq, tk, kv_len, sk)
    lane = lax.broadcasted_iota(jnp.int32, (tq, LANES), 1)
    n_pairs = H_FOX * HD_FOX // LANES
    q_pos = q_start + lax.broadcasted_iota(jnp.int32, (2 * tq, 1), 0) % tq
    q2 = []
    for j in range(n_pairs):
        q_pair = q_ref[:, j * LANES:(j + 1) * LANES]
        zero = jnp.zeros_like(q_pair)
        q2.append(jnp.concatenate([jnp.where(lane < HD_FOX, q_pair, zero),
                                   jnp.where(lane < HD_FOX, zero, q_pair)], axis=0))
        for half in range(2):
            col = jnp.sum(jnp.where(lane == 2 * j + half, cq_ref[0], 0.0), axis=-1, keepdims=True)
            cqb_ref[j, half * tq:(half + 1) * tq, :] = jnp.broadcast_to(col, (tq, LANES))

    def consume(kt, masked):
        k0 = pl.multiple_of(kt * tk, tk)
        if masked:
            k_pos = k0 + lax.broadcasted_iota(jnp.int32, (1, tk), 1)
            mask = k_pos <= q_pos
            if kv_len < sk:
                mask = jnp.logical_and(mask, k_pos < kv_len)
        for j in range(n_pairs):
            s = _dot_nt(q2[j], k_ref[0, pl.ds(k0, tk), j * LANES:(j + 1) * LANES])
            c_k = jnp.concatenate(
                [jnp.broadcast_to(ck_ref[0, 2 * j:2 * j + 1, pl.ds(k0, tk)], (tq, tk)),
                 jnp.broadcast_to(ck_ref[0, 2 * j + 1:2 * j + 2, pl.ds(k0, tk)], (tq, tk))], axis=0)
            s = s - c_k
            if masked:
                s = jnp.where(mask, s, NEG_INF)
            v_ones = v_ref[0, pl.ds(k0, tk), 2 * j * LANES:2 * (j + 1) * LANES]
            _online_softmax_step(s, v_ones, m_ref.at[j], acc_ref.at[j], shift=cqb_ref[j])

    _init_softmax_state(m_ref, acc_ref)
    _kv_loop(n_full, n_any, consume)
    for j in range(n_pairs):
        o = _normalised(acc_ref[j])
        o_ref[:, j * LANES:(j + 1) * LANES] = jnp.where(lane < HD_FOX, o[:tq], o[tq:]).astype(BF16)


def _fox_attention(fq, fkb, fvb, c_q, c_k, batch, sq, tq, tk, kv_len, q_pos0):
    sk = fkb.shape[1]
    nq = sq // tq
    w = H_FOX * HD_FOX
    kern = functools.partial(_fox_kernel, tq=tq, tk=tk, kv_len=kv_len, q_pos0=q_pos0)
    return pl.pallas_call(
        kern,
        grid=(batch, nq),
        in_specs=[pl.BlockSpec((tq, w), lambda b, i: (b * nq + i, 0)),
                  pl.BlockSpec((1, sk, w), lambda b, i: (b, 0, 0)),
                  pl.BlockSpec((1, sk, 2 * w), lambda b, i: (b, 0, 0)),
                  pl.BlockSpec((1, tq, LANES), lambda b, i: (b, i, 0)),
                  pl.BlockSpec((1, H_FOX, sk), lambda b, i: (b, 0, 0))],
        out_specs=pl.BlockSpec((tq, w), lambda b, i: (b * nq + i, 0)),
        out_shape=jax.ShapeDtypeStruct((batch * sq, w), BF16),
        scratch_shapes=[pltpu.VMEM((w // LANES, 2 * tq, LANES), F32),
                        pltpu.VMEM((w // LANES, 2 * tq, 2 * LANES), F32),
                        pltpu.VMEM((w // LANES, 2 * tq, LANES), F32)],
        compiler_params=_params(2),
    )(fq, fkb, fvb, c_q, c_k)


def _fox_cached_kernel(q_ref, ck_ref, cv_ref, kn_ref, vn_ref, cq_ref, crow_ref, o_ref, *, past):
    tq = q_ref.shape[0]
    ck_ref = ck_ref.reshape(1, past * H_FOX, HD_FOX)
    cv_ref = cv_ref.reshape(1, past * H_FOX, HD_FOX)
    pad = jnp.zeros((LANES - tq, HD_FOX), BF16)
    q_idx = lax.broadcasted_iota(jnp.int32, (tq, LANES), 0)
    k_idx = lax.broadcasted_iota(jnp.int32, (tq, LANES), 1)
    visible = k_idx <= q_idx
    for h in range(H_FOX):
        cols = slice(h * HD_FOX, (h + 1) * HD_FOX)
        q = q_ref[:, cols]
        c_q = cq_ref[0, :, h:h + 1]
        k_new = jnp.concatenate([kn_ref[:, cols].astype(BF16), pad], axis=0)
        v_new = jnp.concatenate([vn_ref[:, cols].astype(BF16), pad], axis=0)
        head_rows = pl.ds(h, past, stride=H_FOX)
        s_old = (_dot_nt(q, ck_ref[0, head_rows, :].astype(BF16))
                 + (c_q - crow_ref[0, h:h + 1, 0:past]))
        s_new = _dot_nt(q, k_new) + (c_q - crow_ref[0, h:h + 1, past:past + LANES])
        s_new = jnp.where(visible, s_new, NEG_INF)
        m = jnp.maximum(jnp.max(s_old, axis=-1, keepdims=True), jnp.max(s_new, axis=-1, keepdims=True))
        p_old = jnp.exp2(s_old - m)
        p_new = jnp.exp2(s_new - m)
        l = jnp.sum(p_old, axis=-1, keepdims=True) + jnp.sum(p_new, axis=-1, keepdims=True)
        o = (_dot(p_old.astype(BF16), cv_ref[0, head_rows, :].astype(BF16))
             + _dot(p_new.astype(BF16), v_new))
        o_ref[:, cols] = (o / l).astype(BF16)


def _fox_cached_attention(fq, cache_k, cache_v, k_new, v_new, c_q, c_row, dec_seq):
    batch, past = cache_k.shape[:2]
    w = H_FOX * HD_FOX
    assert past % LANES == 0 and dec_seq <= LANES and c_row.shape[2] >= past + LANES
    new = pl.BlockSpec((dec_seq, w), lambda b: (b, 0))
    old = pl.BlockSpec((1, past, H_FOX, HD_FOX), lambda b: (b, 0, 0, 0))
    return pl.pallas_call(
        functools.partial(_fox_cached_kernel, past=past),
        grid=(batch,),
        in_specs=[new, old, old, new, new,
                  pl.BlockSpec((1, dec_seq, LANES), lambda b: (b, 0, 0)),
                  pl.BlockSpec((1, H_FOX, c_row.shape[2]), lambda b: (b, 0, 0))],
        out_specs=new,
        out_shape=jax.ShapeDtypeStruct((batch * dec_seq, w), BF16),
        compiler_params=_params(1),
    )(fq, cache_k, cache_v, k_new, v_new, c_q, c_row)


def _mem_kernel(q_ref, k_ref, v_ref, o_ref):
    for h in range(H_MEM):
        cols = slice(h * HD_MEM, (h + 1) * HD_MEM)
        s = _dot_nt(q_ref[:, cols], k_ref[0, :, cols]) * (MEM_SCALE * LOG2E)
        m = jnp.max(s, axis=-1, keepdims=True)
        p = jnp.exp2(s - m)
        l = jnp.sum(p, axis=-1, keepdims=True)
        o = _dot(p.astype(BF16), v_ref[0, :, cols]) / l
        o_ref[:, cols] = o.astype(BF16)


def _mem_attention(mq, mkb, mvb, batch, sq, tq):
    nq = sq // tq
    n_mem = mkb.shape[1]
    w = H_MEM * HD_MEM
    return pl.pallas_call(
        _mem_kernel,
        grid=(batch, nq),
        in_specs=[pl.BlockSpec((tq, w), lambda b, i: (b * nq + i, 0)),
                  pl.BlockSpec((1, n_mem, w), lambda b, i: (b, 0, 0)),
                  pl.BlockSpec((1, n_mem, w), lambda b, i: (b, 0, 0))],
        out_specs=pl.BlockSpec((tq, w), lambda b, i: (b * nq + i, 0)),
        out_shape=jax.ShapeDtypeStruct((batch * sq, w), BF16),
        compiler_params=_params(2),
    )(mq, mkb, mvb)


def _route(x1b, wr_t, bias):
    tm = x1b.shape[0]
    scores = _sigmoid(_dot_nt(wr_t, x1b))
    sel = scores + bias
    sub = lax.broadcasted_iota(jnp.int32, (GROUP_SIZE, tm), 0)
    gs = []
    for g in range(N_GROUPS):
        blk = sel[g * GROUP_SIZE:(g + 1) * GROUP_SIZE, :]
        m1 = jnp.max(blk, axis=0, keepdims=True)
        first = jnp.min(jnp.where(blk == m1, sub, GROUP_SIZE), axis=0, keepdims=True)
        m2 = jnp.max(jnp.where(sub == first, -jnp.inf, blk), axis=0, keepdims=True)
        gs.append(m1 + m2)
    e_idx = lax.broadcasted_iota(jnp.int32, (N_EXPERTS, tm), 0)
    allowed = jnp.zeros((N_EXPERTS, tm), jnp.bool_)
    for g in range(N_GROUPS):
        rank = jnp.zeros((1, tm), jnp.int32)
        for o in range(N_GROUPS):
            if o == g:
                continue
            beats = (gs[o] >= gs[g]) if o < g else (gs[o] > gs[g])
            rank = rank + beats.astype(jnp.int32)
        keep = rank < TOPK_GROUPS
        allowed = jnp.logical_or(allowed, jnp.logical_and(e_idx // GROUP_SIZE == g, keep))
    cand = jnp.where(allowed, sel, -jnp.inf)
    chosen = jnp.zeros((N_EXPERTS, tm), jnp.bool_)
    for _ in range(TOP_K):
        m = jnp.max(cand, axis=0, keepdims=True)
        first = jnp.min(jnp.where(cand == m, e_idx, N_EXPERTS), axis=0, keepdims=True)
        pick = e_idx == first
        chosen = jnp.logical_or(chosen, pick)
        cand = jnp.where(pick, -jnp.inf, cand)
    w = jnp.where(chosen, scores, 0.0)
    return w / jnp.sum(w, axis=0, keepdims=True) * ROUTED_SCALE, chosen


def _swiglu(xb, wg, wu):
    a = _dot(xb, wg)
    return (a * _sigmoid(a)) * _dot(xb, wu)


def _merge_kernel(x_ref, olat_ref, ofox_ref, omem_ref, wg_ref, wuv_ref, wbm_ref, wbf_ref, wbc_ref,
                  wo_ref, g1_ref, b1_ref, wsg_ref, wsu_ref, wsd_ref, x1b_ref, base_ref):
    x = x_ref[...]
    xb = x.astype(BF16)
    o_mla = jnp.concatenate(
        [_dot(olat_ref[:, j * 2 * KV_LORA:(j + 1) * 2 * KV_LORA], wuv_ref[j]).astype(BF16)
         for j in range(H_MLA // 2)], axis=-1)

    def gate(n):
        return _sigmoid(_dot(xb, wg_ref[:, n * D_MODEL:(n + 1) * D_MODEL]))

    y = gate(0) * _dot(o_mla, wbm_ref[...])
    y = y + gate(1) * _dot(ofox_ref[...], wbf_ref[...])
    y = y + gate(2) * _dot(omem_ref[...], wbc_ref[...])
    mix = _dot(y.astype(BF16), wo_ref[...])
    x1 = _layer_norm(ALPHA * x + mix, g1_ref[...], b1_ref[...])
    x1b = x1.astype(BF16)
    x1b_ref[...] = x1b
    shared = _dot(_swiglu(x1b, wsg_ref[...], wsu_ref[...]).astype(BF16), wsd_ref[...])
    base_ref[...] = ALPHA * x1 + shared


def _merge(x, olat, ofox, omem, wg, wuv, wbm, wbf, wbc, wo, g1, b1, wsg, wsu, wsd, tm):
    t = x.shape[0]
    row = lambda n: pl.BlockSpec((tm, n), lambda i: (i, 0))
    ws = (wg, wuv, wbm, wbf, wbc, wo, g1, b1, wsg, wsu, wsd)
    return pl.pallas_call(
        _merge_kernel,
        grid=(t // tm,),
        in_specs=[row(D_MODEL), row(H_MLA * KV_LORA), row(H_FOX * HD_FOX), row(H_MEM * HD_MEM)]
        + [_full(w.shape) for w in ws],
        out_specs=(row(D_MODEL), row(D_MODEL)),
        out_shape=(jax.ShapeDtypeStruct((t, D_MODEL), BF16), jax.ShapeDtypeStruct((t, D_MODEL), F32)),
        compiler_params=_params(1),
    )(x, olat, ofox, omem, *ws)


BF16_ROWS = 16
PLACE_ROWS = 512
EXPERT_ROWS = 4096
EXPERT_CHUNK = 512
EXPERTS_PER_STEP = 4


def _capacity(tm):
    return min(tm, _round_up(2 * tm * TOP_K // N_EXPERTS, BF16_ROWS))


def _dispatch_kernel(x1b_ref, wr_ref, rb_ref, xs_ref, gate_ref, slot_ref, govf_ref, flag_ref,
                     pm_ref, *, tm, cap):
    n_tiles = x1b_ref.shape[0] // tm
    slots = []
    for h in range(n_tiles):
        rows = slice(h * tm, (h + 1) * tm)
        xb = x1b_ref[rows, :]
        gate, chosen = _route(xb, wr_ref[...], rb_ref[...])
        r = lax.broadcasted_iota(jnp.int32, (tm, tm), 0)
        c = lax.broadcasted_iota(jnp.int32, (tm, tm), 1)
        earlier = jnp.where(r < c, 1.0, 0.0).astype(BF16)
        rank = _dot(jnp.where(chosen, 1.0, 0.0).astype(BF16), earlier)
        fits = jnp.logical_and(chosen, rank < cap)
        slot = jnp.where(fits, rank, -1.0)
        gate_ref[:, rows] = jnp.where(fits, gate, 0.0)
        slot_ref[:, rows] = slot
        g_ovf = jnp.where(fits, 0.0, gate)
        govf_ref[rows, :] = jnp.concatenate(
            [g_ovf, jnp.zeros((LANES - N_EXPERTS, tm), F32)], axis=0).T
        any_ovf = jnp.max(jnp.max(g_ovf, axis=0, keepdims=True), axis=1, keepdims=True)
        flag_ref[h] = jnp.broadcast_to(jnp.where(any_ovf > 0.0, 1, 0),
                                       flag_ref.shape[1:]).astype(jnp.int32)
        slots.append(slot)
    s_iota = lax.broadcasted_iota(jnp.int32, (cap, tm), 0).astype(F32)
    for h in range(n_tiles):
        slot = slots[h]
        xb = x1b_ref[h * tm:(h + 1) * tm, :]
        for e in range(N_EXPERTS):
            pm_ref[h, e * cap:(e + 1) * cap, :] = jnp.where(
                s_iota == slot[e:e + 1, :], 1.0, 0.0).astype(BF16)
        per = PLACE_ROWS // cap
        for g in range(N_EXPERTS // per):
            xs = _dot(pm_ref[h, g * PLACE_ROWS:(g + 1) * PLACE_ROWS, :], xb).astype(BF16)
            xs_ref[g * per:(g + 1) * per, h * cap:(h + 1) * cap, :] = xs.reshape(per, cap, D_MODEL)


def _dispatch(x1b, wr_t, rb, tm, cap):
    t = x1b.shape[0]
    nt = t // tm
    ts = 2 if nt % 2 == 0 else 1
    col = pl.BlockSpec((N_EXPERTS, ts * tm), lambda i: (0, i))
    return pl.pallas_call(
        functools.partial(_dispatch_kernel, tm=tm, cap=cap),
        grid=(nt // ts,),
        in_specs=[pl.BlockSpec((ts * tm, D_MODEL), lambda i: (i, 0)), _full(wr_t.shape),
                  _full(rb.shape)],
        out_specs=(pl.BlockSpec((N_EXPERTS, ts * cap, D_MODEL), lambda i: (0, i, 0)), col, col,
                   pl.BlockSpec((ts * tm, LANES), lambda i: (i, 0)),
                   pl.BlockSpec((ts, 8, LANES), lambda i: (i, 0, 0))),
        out_shape=(jax.ShapeDtypeStruct((N_EXPERTS, nt * cap, D_MODEL), BF16),
                   jax.ShapeDtypeStruct((N_EXPERTS, t), F32),
                   jax.ShapeDtypeStruct((N_EXPERTS, t), F32),
                   jax.ShapeDtypeStruct((t, LANES), F32),
                   jax.ShapeDtypeStruct((nt, 8, LANES), jnp.int32)),
        scratch_shapes=[pltpu.VMEM((ts, N_EXPERTS * cap, tm), BF16)],
        compiler_params=_params(1),
    )(x1b, wr_t, rb)


def _expert_kernel(xs_ref, xs2_ref, wg_ref, wu_ref, wd_ref, ys_ref, ys2_ref, wgb_ref, wub_ref,
                   wdb_ref, *, chunk):
    def ffn(x):
        hid = _swiglu(x, wgb_ref[...], wub_ref[...])
        return _dot(hid.astype(BF16), wdb_ref[...]).astype(BF16)

    @pl.when(pl.program_id(1) == 0)
    def _():
        wgb_ref[...] = wg_ref[0].astype(BF16)
        wub_ref[...] = wu_ref[0].astype(BF16)
        wdb_ref[...] = wd_ref[0].astype(BF16)
        ys2_ref[0] = ffn(xs2_ref[0])

    def body(c, carry):
        r0 = pl.multiple_of(c * chunk, chunk)
        ys_ref[0, pl.ds(r0, chunk), :] = ffn(xs_ref[0, pl.ds(r0, chunk), :])
        return carry

    lax.fori_loop(0, xs_ref.shape[1] // chunk, body, 0)


def _experts(xs, xs2, weg, weu, wed):
    n_e, rows, _ = xs.shape
    rb = min(EXPERT_ROWS, rows)
    blk = pl.BlockSpec((1, rb, D_MODEL), lambda e, i: (e, i, 0))
    blk2 = pl.BlockSpec((1, xs2.shape[1], D_MODEL), lambda e, i: (e, 0, 0))
    return pl.pallas_call(
        functools.partial(_expert_kernel, chunk=min(EXPERT_CHUNK, rb)),
        grid=(n_e, rows // rb),
        in_specs=[blk, blk2,
                  pl.BlockSpec((1, D_MODEL, D_EXPERT), lambda e, i: (e, 0, 0)),
                  pl.BlockSpec((1, D_MODEL, D_EXPERT), lambda e, i: (e, 0, 0)),
                  pl.BlockSpec((1, D_EXPERT, D_MODEL), lambda e, i: (e, 0, 0))],
        out_specs=(blk, blk2),
        out_shape=(jax.ShapeDtypeStruct(xs.shape, BF16), jax.ShapeDtypeStruct(xs2.shape, BF16)),
        scratch_shapes=[pltpu.VMEM((D_MODEL, D_EXPERT), BF16), pltpu.VMEM((D_MODEL, D_EXPERT), BF16),
                        pltpu.VMEM((D_EXPERT, D_MODEL), BF16)],
        compiler_params=_params(2),
    )(xs, xs2, weg, weu, wed)


def _dense_routed_kernel(flag_ref, x1b_ref, gate_ref, weg_ref, weu_ref, wed_ref, out_ref):
    step = pl.program_id(1)

    @pl.when(step == 0)
    def _():
        out_ref[...] = jnp.zeros_like(out_ref)

    @pl.when(flag_ref[pl.program_id(0)] > 0)
    def _():
        xb = x1b_ref[...]
        gate = gate_ref[...]
        lane = lax.broadcasted_iota(jnp.int32, gate.shape, 1)
        for j in range(EXPERTS_PER_STEP):
            e = step * EXPERTS_PER_STEP + j
            g_col = jnp.sum(jnp.where(lane == e, gate, 0.0), axis=-1, keepdims=True)
            hid = _swiglu(xb, weg_ref[j], weu_ref[j])
            out_ref[...] += g_col * _dot(hid.astype(BF16), wed_ref[j])


def _dense_routed(flags, x1b, gate, weg, weu, wed, tm):
    t = x1b.shape[0]
    eb = EXPERTS_PER_STEP
    row = lambda n: pl.BlockSpec((tm, n), lambda i, s, f: (i, 0))
    wspec = lambda a, b: pl.BlockSpec((eb, a, b), lambda i, s, f: (jnp.where(f[i] > 0, s, 0), 0, 0))
    return pl.pallas_call(
        _dense_routed_kernel,
        grid_spec=pltpu.PrefetchScalarGridSpec(
            num_scalar_prefetch=1,
            grid=(t // tm, N_EXPERTS // eb),
            in_specs=[row(D_MODEL), row(LANES), wspec(D_MODEL, D_EXPERT), wspec(D_MODEL, D_EXPERT),
                      wspec(D_EXPERT, D_MODEL)],
            out_specs=row(D_MODEL)),
        out_shape=jax.ShapeDtypeStruct((t, D_MODEL), F32),
        compiler_params=_params(2),
    )(flags, x1b, gate, weg, weu, wed)


def _combine_kernel(gate_ref, slot_ref, ys_ref, base_ref, *rest, cap, has_overflow):
    if has_overflow:
        rovf_ref, g2_ref, b2_ref, y_ref, pw_ref = rest
    else:
        g2_ref, b2_ref, y_ref, pw_ref = rest
    tm = base_ref.shape[0]
    gate = gate_ref[...]
    slot = slot_ref[...]
    s_iota = lax.broadcasted_iota(jnp.int32, (cap, tm), 0).astype(F32)
    for e in range(N_EXPERTS):
        pw_ref[e * cap:(e + 1) * cap, :] = jnp.where(
            s_iota == slot[e:e + 1, :], gate[e:e + 1, :], 0.0).astype(BF16)
    routed = rovf_ref[...] if has_overflow else jnp.zeros((tm, D_MODEL), F32)
    per = PLACE_ROWS // cap
    for g in range(N_EXPERTS // per):
        ys = ys_ref[g * per:(g + 1) * per].reshape(PLACE_ROWS, D_MODEL)
        routed = routed + lax.dot_general(pw_ref[g * PLACE_ROWS:(g + 1) * PLACE_ROWS, :], ys,
                                          (((0,), (0,)), ((), ())), preferred_element_type=F32)
    y_ref[...] = _layer_norm(base_ref[...] + routed, g2_ref[...], b2_ref[...])


def _combine(gate_t, slot_t, ys, base, rovf, g2, b2, tm, cap):
    t = base.shape[0]
    col = pl.BlockSpec((N_EXPERTS, tm), lambda i: (0, i))
    row = pl.BlockSpec((tm, D_MODEL), lambda i: (i, 0))
    has_overflow = rovf is not None
    operands = (gate_t, slot_t, ys, base) + ((rovf,) if has_overflow else ()) + (g2, b2)
    return pl.pallas_call(
        functools.partial(_combine_kernel, cap=cap, has_overflow=has_overflow),
        grid=(t // tm,),
        in_specs=[col, col, pl.BlockSpec((N_EXPERTS, cap, D_MODEL), lambda i: (0, i, 0)), row]
        + ([row] if has_overflow else []) + [_full(g2.shape), _full(b2.shape)],
        out_specs=row,
        out_shape=jax.ShapeDtypeStruct((t, D_MODEL), F32),
        scratch_shapes=[pltpu.VMEM((N_EXPERTS * cap, tm), BF16)],
        compiler_params=_params(1),
    )(*operands)


def _moe_place(x1b, wr_t, rb, tm):
    cap = _capacity(tm)
    assert PLACE_ROWS % cap == 0 and (N_EXPERTS * cap) % PLACE_ROWS == 0
    return _dispatch(x1b, wr_t, rb, tm, cap)


def _moe_ln2(placed, ys, x1b, base, weg, weu, wed, g2, b2, tm):
    cap = _capacity(tm)
    _, gate_t, slot_t, gate_ovf, flags = placed
    flags = flags[:, 0, 0]

    def with_overflow():
        rovf = _dense_routed(flags, x1b, gate_ovf, weg.astype(BF16), weu.astype(BF16),
                             wed.astype(BF16), tm)
        return _combine(gate_t, slot_t, ys, base, rovf, g2, b2, tm, cap)

    return lax.cond(jnp.any(flags > 0), with_overflow,
                    lambda: _combine(gate_t, slot_t, ys, base, None, g2, b2, tm, cap))


def _rot_half_cols(w):
    half = w.shape[-1] // 2
    return jnp.concatenate([-w[..., half:], w[..., :half]], axis=-1)


def _block_diag2(a, b):
    za = jnp.zeros((a.shape[0], b.shape[1]), a.dtype)
    zb = jnp.zeros((b.shape[0], a.shape[1]), a.dtype)
    return jnp.concatenate([jnp.concatenate([a, za], 1), jnp.concatenate([zb, b], 1)], 0)


def _prep_weights(w_in, b_f, q_norm_g, w_q_up, kv_norm_g, w_kv_up):
    widths = (Q_LORA, KV_LORA, ROPE_DIM, H_FOX * HD_FOX, H_FOX * HD_FOX, H_FOX * HD_FOX, H_FOX,
              H_MEM * HD_MEM, N_BRANCH * D_MODEL)
    offs = np.cumsum((0,) + widths)
    w_ql, w_ckv, w_kr, w_fq, w_fk, w_fv, w_fl, w_mq, w_g = (
        w_in[:, offs[i]:offs[i + 1]] for i in range(len(widths)))
    rep = LANES // ROPE_DIM
    w_main = jnp.concatenate(
        [w_ql, w_ckv, jnp.tile(w_kr, (1, rep)), jnp.tile(_rot_half_cols(w_kr), (1, rep)),
         w_fq, w_fk, w_fv, w_mq, jnp.pad(w_fl, ((0, 0), (0, LANES - H_FOX)))], axis=1).astype(BF16)
    assert w_main.shape[1] == C_END
    wq3 = w_q_up.reshape(Q_LORA, H_MLA, NOPE_DIM + ROPE_DIM)
    wq_nope = wq3[:, :, :NOPE_DIM].reshape(Q_LORA, H_MLA * NOPE_DIM)
    wq_rope = wq3[:, :, NOPE_DIM:]
    w_q = jnp.concatenate(
        [wq_nope, wq_rope.reshape(Q_LORA, -1), _rot_half_cols(wq_rope).reshape(Q_LORA, -1)],
        axis=1).astype(BF16)
    wkv3 = w_kv_up.reshape(KV_LORA, H_MLA, NOPE_DIM + V_DIM)
    w_uk = wkv3[:, :, :NOPE_DIM]
    w_uv = wkv3[:, :, NOPE_DIM:]
    w_abs = jnp.stack([_block_diag2(w_uk[:, 2 * j].T, w_uk[:, 2 * j + 1].T)
                       for j in range(H_MLA // 2)]).astype(BF16)
    w_uvp = jnp.stack([_block_diag2(w_uv[:, 2 * j], w_uv[:, 2 * j + 1])
                       for j in range(H_MLA // 2)]).astype(BF16)
    b_fp = jnp.pad(b_f, (0, LANES - H_FOX)).reshape(1, LANES)
    return (w_main, w_q, w_abs, q_norm_g.reshape(1, -1), kv_norm_g.reshape(1, -1), b_fp,
            w_g.astype(BF16), w_uvp)


def _rope_tables(pos):
    half = ROPE_DIM // 2
    inv_freq = ROPE_THETA ** (-jnp.arange(half, dtype=F32) / half)
    ang = pos.astype(F32)[:, None] * inv_freq[None, :]
    rep = 2 * LANES // ROPE_DIM
    cos = jnp.tile(jnp.concatenate([jnp.cos(ang)] * 2, axis=-1), (1, rep))
    sin = jnp.tile(jnp.concatenate([jnp.sin(ang)] * 2, axis=-1), (1, rep))
    return cos, sin


def _round_up(n, m):
    return (n + m - 1) // m * m


def kernel(x_prompt, x_sample, mem_prompt, cache_mla_latent, cache_mla_krope, cache_fox_k,
           cache_fox_v, cache_fox_logf, cache_mem_k, cache_mem_v, ln1_g, ln1_b, w_in, b_f,
           q_norm_g, w_q_up, kv_norm_g, w_kv_up, w_mem_k, w_mem_v, w_br_mla, w_br_fox,
           w_br_mem, w_out, ln2_g, ln2_b, w_router, router_bias, w_e_gate, w_e_up, w_e_down,
           w_s_gate, w_s_up, w_s_down):
    assert w_in.shape[0] == DEPTH == 1
    batch, seq, _ = x_prompt.shape
    dec_batch, dec_seq, _ = x_sample.shape
    past = cache_mla_latent.shape[2]
    n_mem = mem_prompt.shape[1]
    l = 0

    (w_main, w_q, w_abs, q_g, kv_g, b_fp, w_gate, w_uvp) = _prep_weights(
        w_in[l], b_f[l], q_norm_g[l], w_q_up[l], kv_norm_g[l], w_kv_up[l])
    merge_w = (w_gate, w_uvp, w_br_mla[l].astype(BF16), w_br_fox[l].astype(BF16),
               w_br_mem[l].astype(BF16), w_out[l].astype(BF16), ln1_g[l].reshape(1, -1),
               ln1_b[l].reshape(1, -1), w_s_gate[l].astype(BF16), w_s_up[l].astype(BF16),
               w_s_down[l].astype(BF16))
    route_w = (w_router[l].T.astype(BF16), router_bias[l].reshape(-1, 1))
    expert_w = (w_e_gate[l], w_e_up[l], w_e_down[l])
    ln2_w = (ln2_g[l].reshape(1, -1), ln2_b[l].reshape(1, -1))

    tp = batch * seq
    xp = x_prompt.reshape(tp, D_MODEL)
    tm_p = min(512, seq)
    tq_p = min(256, seq)
    tk_p = min(512, seq)
    cos_p, sin_p = _rope_tables(jnp.arange(seq, dtype=jnp.int32))
    (qcat, kcat, lat, latb, kr, fq, fk, fv, fkb, fvb, mq, logf, logfp) = _proj(
        xp, w_main, w_q, w_abs, q_g, kv_g, b_fp, cos_p, sin_p, tm_p)
    tmem = batch * n_mem
    m_k, m_v, m_kb, m_vb = _memkv(mem_prompt.reshape(tmem, D_MODEL), w_mem_k[l].astype(BF16),
                                  w_mem_v[l].astype(BF16), min(512, tmem))
    c_p = _cumsum(logfp.reshape(batch, seq, LANES))
    c_keys = jnp.swapaxes(c_p[:, :, :H_FOX], 1, 2)
    olat = _mla_attention(qcat, kcat.reshape(batch, seq, -1), latb.reshape(batch, seq, -1),
                          batch, seq, tq_p, tk_p, seq, 0)
    ofox = _fox_attention(fq, fkb.reshape(batch, seq, -1), fvb.reshape(batch, seq, -1), c_p, c_keys,
                          batch, seq, tk_p, tk_p, seq, 0)
    omem = _mem_attention(mq, m_kb.reshape(batch, n_mem, -1), m_vb.reshape(batch, n_mem, -1),
                          batch, seq, tk_p)
    x1b_p, base_p = _merge(xp, olat, ofox, omem, *merge_w, tm_p)
    tm_moe = min(256, seq)
    placed_p = _moe_place(x1b_p, *route_w, tm_moe)

    ts = dec_batch * dec_seq
    xs = x_sample.reshape(ts, D_MODEL)
    cos_s, sin_s = _rope_tables(jnp.tile(past + jnp.arange(dec_seq, dtype=jnp.int32), dec_batch))
    (qcat_s, kcat_s, lat_s, latb_s, kr_s, fq_s, fk_s, fv_s, fkb_s, fvb_s, mq_s, logf_s,
     logfp_s) = _proj(xs, w_main, w_q, w_abs, q_g, kv_g, b_fp, cos_s, sin_s, ts)
    kv_len = past + dec_seq
    tk_s = 1024
    sk = _round_up(kv_len, tk_s)

    def with_cache(cache, new):
        new = new.reshape(dec_batch, dec_seq, -1)
        pad = jnp.zeros((dec_batch, sk - kv_len, new.shape[-1]), new.dtype)
        return jnp.concatenate([cache.astype(new.dtype), new, pad], axis=1)

    rep = LANES // ROPE_DIM
    kcat_all = with_cache(
        jnp.concatenate([cache_mla_latent[l], jnp.tile(cache_mla_krope[l], (1, 1, rep))], axis=-1),
        kcat_s)
    lat_all = with_cache(
        jnp.concatenate([cache_mla_latent[l], jnp.ones((dec_batch, past, LANES), F32)], axis=-1), latb_s)
    logf_all = with_cache(jnp.pad(cache_fox_logf[l], ((0, 0), (0, 0), (0, LANES - H_FOX))), logfp_s)
    c_s = _cumsum(logf_all)
    c_keys_s = jnp.swapaxes(c_s[:, :, :H_FOX], 1, 2)
    c_q_s = c_s[:, past:kv_len]
    olat_s = _mla_attention(qcat_s, kcat_all, lat_all, dec_batch, dec_seq, dec_seq, tk_s, kv_len, past)
    ofox_s = _fox_cached_attention(fq_s, cache_fox_k[l], cache_fox_v[l], fk_s, fv_s, c_q_s, c_keys_s,
                                   dec_seq)
    omem_s = _mem_attention(mq_s, cache_mem_k[l].reshape(dec_batch, n_mem, -1).astype(BF16),
                            cache_mem_v[l].reshape(dec_batch, n_mem, -1).astype(BF16),
                            dec_batch, dec_seq, dec_seq)
    x1b_s, base_s = _merge(xs, olat_s, ofox_s, omem_s, *merge_w, ts)
    placed_s = _moe_place(x1b_s, *route_w, ts)

    ys_p, ys_s = _experts(placed_p[0], placed_s[0], *expert_w)
    y_p = _moe_ln2(placed_p, ys_p, x1b_p, base_p, *expert_w, *ln2_w, tm_moe)
    y_s = _moe_ln2(placed_s, ys_s, x1b_s, base_s, *expert_w, *ln2_w, ts)
    y_p = y_p.reshape(batch, seq, D_MODEL)
    y_s = y_s.reshape(dec_batch, dec_seq, D_MODEL)

    def stack(a, b, s, *tail_shape):
        return a.reshape(1, b, s, *tail_shape)

    return (y_p, y_s,
            stack(lat, batch, seq, KV_LORA), stack(kr, batch, seq, ROPE_DIM),
            stack(fk, batch, seq, H_FOX, HD_FOX), stack(fv, batch, seq, H_FOX, HD_FOX),
            stack(logf, batch, seq, H_FOX),
            stack(m_k, batch, n_mem, H_MEM, HD_MEM), stack(m_v, batch, n_mem, H_MEM, HD_MEM),
            stack(lat_s, dec_batch, dec_seq, KV_LORA), stack(kr_s, dec_batch, dec_seq, ROPE_DIM),
            stack(fk_s, dec_batch, dec_seq, H_FOX, HD_FOX), stack(fv_s, dec_batch, dec_seq, H_FOX, HD_FOX),
            stack(logf_s, dec_batch, dec_seq, H_FOX))
```

```python
import functools

import jax
import jax.numpy as jnp
import numpy as np
from jax import lax
from jax.experimental import pallas as pl
from jax.experimental.pallas import tpu as pltpu

F32 = jnp.float32
BF16 = jnp.bfloat16

D_MODEL = 1024
CHUNK = 64
H_MLA = 8
Q_LORA = 256
KV_LORA = 128
NOPE_DIM = 64
ROPE_DIM = 32
V_DIM = 64
ROPE_THETA = 10000.0
MLA_SCALE = (NOPE_DIM + ROPE_DIM) ** -0.5
H_FOX = 8
HD_FOX = 64
FOX_SCALE = HD_FOX ** -0.5
H_MEM = 4
HD_MEM = 128
MEM_SCALE = HD_MEM ** -0.5
N_BRANCH = 3
N_EXPERTS = 64
TOP_K = 8
N_GROUPS = 8
GROUP_SIZE = N_EXPERTS // N_GROUPS
TOPK_GROUPS = 4
D_EXPERT = 256
ROUTED_SCALE = 2.5
DEPTH = 1
ALPHA = (2 * DEPTH) ** 0.25
NORM_EPS = 1e-5
NEG_INF = -1e30

LOG2E = 1.4426950408889634
LANES = 128
VMEM_LIMIT = 56 * 1024 * 1024

C_QLAT = 0
C_CKV = C_QLAT + Q_LORA
C_KR = C_CKV + KV_LORA
C_KRR = C_KR + LANES
C_FQ = C_KRR + LANES
C_FK = C_FQ + H_FOX * HD_FOX
C_FV = C_FK + H_FOX * HD_FOX
C_MQ = C_FV + H_FOX * HD_FOX
C_FL = C_MQ + H_MEM * HD_MEM
C_END = C_FL + LANES


def _params(n_grid_axes):
    return pltpu.CompilerParams(
        dimension_semantics=("arbitrary",) * n_grid_axes, vmem_limit_bytes=VMEM_LIMIT)


def _full(shape):
    return pl.BlockSpec(shape, lambda *_: (0,) * len(shape))


def _dot(a, b):
    return jnp.dot(a, b, preferred_element_type=F32)


def _dot_nt(a, b):
    return lax.dot_general(a, b, (((1,), (1,)), ((), ())), preferred_element_type=F32)


def _sigmoid(x):
    return 1.0 / (1.0 + jnp.exp(-x))


def _rms(x, g):
    ms = jnp.mean(x * x, axis=-1, keepdims=True)
    return x * lax.rsqrt(ms + NORM_EPS) * g


def _layer_norm(x, g, b):
    mu = jnp.mean(x, axis=-1, keepdims=True)
    xc = x - mu
    var = jnp.mean(xc * xc, axis=-1, keepdims=True)
    return xc * lax.rsqrt(var + NORM_EPS) * g + b


def _proj_kernel(x_ref, w_ref, wq_ref, wabs_ref, qg_ref, kvg_ref, bf_ref, cos_ref, sin_ref,
                 qcat_ref, kcat_ref, lat_ref, latb_ref, kr_ref, fq_ref, fk_ref, fv_ref,
                 fkb_ref, fvb_ref, mq_ref, logf_ref, logfp_ref):
    xb = x_ref[...].astype(BF16)
    tm = xb.shape[0]

    def proj(c0, width):
        return _dot(xb, w_ref[:, c0:c0 + width])

    cos = cos_ref[...]
    sin = sin_ref[...]
    qn = _rms(proj(C_QLAT, Q_LORA), qg_ref[...]).astype(BF16)
    q2 = _dot(qn, wq_ref[...])
    nq = H_MLA * NOPE_DIM
    nr = H_MLA * ROPE_DIM
    q_rope = (q2[:, nq:nq + nr] * cos + q2[:, nq + nr:nq + 2 * nr] * sin) * (MLA_SCALE * LOG2E)
    lane = lax.broadcasted_iota(jnp.int32, (tm, LANES), 1)
    heads_per_group = LANES // ROPE_DIM
    for j in range(H_MLA // 2):
        q_abs = _dot(q2[:, j * LANES:(j + 1) * LANES].astype(BF16), wabs_ref[j]) * (MLA_SCALE * LOG2E)
        for half in range(2):
            h = 2 * j + half
            grp = h // heads_per_group
            qr = q_rope[:, grp * LANES:(grp + 1) * LANES]
            qr = jnp.where(lane // ROPE_DIM == h % heads_per_group, qr, 0.0)
            qcat_ref[h, :, 0:LANES] = q_abs[:, half * LANES:(half + 1) * LANES].astype(BF16)
            qcat_ref[h, :, LANES:2 * LANES] = qr.astype(BF16)
    lat = _rms(proj(C_CKV, KV_LORA), kvg_ref[...])
    lat_ref[...] = lat
    ones = jnp.ones((tm, LANES), BF16)
    latb_ref[:, 0:KV_LORA] = lat.astype(BF16)
    latb_ref[:, KV_LORA:KV_LORA + LANES] = ones
    kr = proj(C_KR, LANES) * cos[:, :LANES] + proj(C_KRR, LANES) * sin[:, :LANES]
    kr_ref[...] = kr[:, :ROPE_DIM]
    kcat_ref[:, 0:KV_LORA] = lat.astype(BF16)
    kcat_ref[:, KV_LORA:KV_LORA + LANES] = kr.astype(BF16)
    fq_ref[...] = (proj(C_FQ, H_FOX * HD_FOX) * (FOX_SCALE * LOG2E)).astype(BF16)
    fk = proj(C_FK, H_FOX * HD_FOX)
    fk_ref[...] = fk
    fkb_ref[...] = fk.astype(BF16)
    fv = proj(C_FV, H_FOX * HD_FOX)
    fv_ref[...] = fv
    for j in range(H_FOX * HD_FOX // LANES):
        fvb_ref[:, 2 * j * LANES:(2 * j + 1) * LANES] = fv[:, j * LANES:(j + 1) * LANES].astype(BF16)
        fvb_ref[:, (2 * j + 1) * LANES:(2 * j + 2) * LANES] = ones
    mq_ref[...] = proj(C_MQ, H_MEM * HD_MEM).astype(BF16)
    z = proj(C_FL, LANES) + bf_ref[...]
    log_f = jnp.minimum(z, 0.0) - jnp.log1p(jnp.exp(-jnp.abs(z)))
    logfp_ref[...] = log_f
    logf_ref[...] = log_f[:, :H_FOX]


def _proj(x, w_main, w_q, w_abs, q_g, kv_g, b_f, cos_t, sin_t, tm):
    t = x.shape[0]
    n_tab = cos_t.shape[0] // tm
    row = lambda w: pl.BlockSpec((tm, w), lambda i: (i, 0))
    tab = pl.BlockSpec((tm, 2 * LANES), lambda i: (i % n_tab, 0))
    hw = H_FOX * HD_FOX
    out_shape = (
        jax.ShapeDtypeStruct((H_MLA, t, 2 * LANES), BF16),
        jax.ShapeDtypeStruct((t, 2 * LANES), BF16),
        jax.ShapeDtypeStruct((t, KV_LORA), F32),
        jax.ShapeDtypeStruct((t, KV_LORA + LANES), BF16),
        jax.ShapeDtypeStruct((t, ROPE_DIM), F32),
        jax.ShapeDtypeStruct((t, hw), BF16),
        jax.ShapeDtypeStruct((t, hw), F32),
        jax.ShapeDtypeStruct((t, hw), F32),
        jax.ShapeDtypeStruct((t, hw), BF16),
        jax.ShapeDtypeStruct((t, 2 * hw), BF16),
        jax.ShapeDtypeStruct((t, H_MEM * HD_MEM), BF16),
        jax.ShapeDtypeStruct((t, H_FOX), F32),
        jax.ShapeDtypeStruct((t, LANES), F32),
    )
    out_specs = (
        pl.BlockSpec((H_MLA, tm, 2 * LANES), lambda i: (0, i, 0)),
        row(2 * LANES), row(KV_LORA), row(KV_LORA + LANES), row(ROPE_DIM), row(hw), row(hw), row(hw),
        row(hw), row(2 * hw), row(H_MEM * HD_MEM), row(H_FOX), row(LANES),
    )
    return pl.pallas_call(
        _proj_kernel,
        grid=(t // tm,),
        in_specs=[row(D_MODEL), _full(w_main.shape), _full(w_q.shape), _full(w_abs.shape),
                  _full(q_g.shape), _full(kv_g.shape), _full(b_f.shape), tab, tab],
        out_specs=out_specs,
        out_shape=out_shape,
        compiler_params=_params(1),
    )(x, w_main, w_q, w_abs, q_g, kv_g, b_f, cos_t, sin_t)


def _memkv_kernel(x_ref, wk_ref, wv_ref, k_ref, v_ref, kb_ref, vb_ref):
    xb = x_ref[...].astype(BF16)
    k = _dot(xb, wk_ref[...])
    v = _dot(xb, wv_ref[...])
    k_ref[...] = k
    v_ref[...] = v
    kb_ref[...] = k.astype(BF16)
    vb_ref[...] = v.astype(BF16)


def _memkv(x, wk, wv, tm):
    t = x.shape[0]
    w = H_MEM * HD_MEM
    row = lambda n: pl.BlockSpec((tm, n), lambda i: (i, 0))
    return pl.pallas_call(
        _memkv_kernel,
        grid=(t // tm,),
        in_specs=[row(D_MODEL), _full(wk.shape), _full(wv.shape)],
        out_specs=(row(w), row(w), row(w), row(w)),
        out_shape=(jax.ShapeDtypeStruct((t, w), F32), jax.ShapeDtypeStruct((t, w), F32),
                   jax.ShapeDtypeStruct((t, w), BF16), jax.ShapeDtypeStruct((t, w), BF16)),
        compiler_params=_params(1),
    )(x, wk, wv)


CUM_BLOCK = 256


def _bf16_pieces(x):
    hi = x.astype(BF16)
    r1 = x - hi.astype(F32)
    mid = r1.astype(BF16)
    lo = (r1 - mid.astype(F32)).astype(BF16)
    return hi, mid, lo


def _cumsum_kernel(x_ref, c_ref):
    s = x_ref.shape[1]
    r = lax.broadcasted_iota(jnp.int32, (CUM_BLOCK, CUM_BLOCK), 0)
    c = lax.broadcasted_iota(jnp.int32, (CUM_BLOCK, CUM_BLOCK), 1)
    tri = jnp.where(r >= c, 1.0, 0.0).astype(BF16)

    def body(i, carry):
        r0 = pl.multiple_of(i * CUM_BLOCK, CUM_BLOCK)
        x = x_ref[0, pl.ds(r0, CUM_BLOCK), :]
        cs = sum(_dot(tri, piece) for piece in _bf16_pieces(x)) + carry
        c_ref[0, pl.ds(r0, CUM_BLOCK), :] = cs * LOG2E
        return cs[CUM_BLOCK - 1:CUM_BLOCK, :]

    lax.fori_loop(0, s // CUM_BLOCK, body, jnp.zeros((1, LANES), F32))


def _cumsum(x):
    b, s, _ = x.shape
    spec = pl.BlockSpec((1, s, LANES), lambda i: (i, 0, 0))
    return pl.pallas_call(
        _cumsum_kernel, grid=(b,), in_specs=[spec], out_specs=spec,
        out_shape=jax.ShapeDtypeStruct(x.shape, F32), compiler_params=_params(1),
    )(x)


def _online_softmax_step(s, v_ones, m_ref, acc_ref, shift=None):
    def lanes(x, width):
        return jnp.concatenate([x] * (width // LANES), axis=-1)

    m = m_ref[...]
    r = jnp.max(s, axis=-1, keepdims=True)
    m_new = jnp.maximum(m, r if shift is None else r + shift)
    alpha = jnp.exp2(m - m_new)
    p = jnp.exp2(s - lanes(m_new if shift is None else m_new - shift, s.shape[-1]))
    m_ref[...] = m_new
    acc_ref[...] = lanes(alpha, acc_ref.shape[-1]) * acc_ref[...] + _dot(p.astype(BF16), v_ones)


def _init_softmax_state(m_ref, acc_ref):
    m_ref[...] = jnp.full(m_ref.shape, NEG_INF, F32)
    acc_ref[...] = jnp.zeros(acc_ref.shape, F32)


def _normalised(acc):
    return acc[:, :LANES] / acc[:, LANES:]


def _kv_tile_counts(first_visible_end, last_visible_end, tk, kv_len, sk):
    n_full = jnp.minimum(first_visible_end // tk, kv_len // tk)
    n_any = jnp.minimum((jnp.minimum(last_visible_end, kv_len) + tk - 1) // tk, sk // tk)
    return n_full, n_any


def _kv_loop(n_full, n_any, consume):
    def body(kt, carry, masked):
        consume(kt, masked)
        return carry

    lax.fori_loop(0, n_full, functools.partial(body, masked=False), 0)
    lax.fori_loop(n_full, n_any, functools.partial(body, masked=True), 0)


def _kv_loop_pipelined(n_any, produce, consume):
    produce(0, 0)
    n_pairs = (n_any - 1) // 2

    def body(i, carry):
        kt = 2 * i
        produce(1, kt + 1)
        consume(0, kt, False)
        produce(0, kt + 2)
        consume(1, kt + 1, False)
        return carry

    lax.fori_loop(0, n_pairs, body, 0)
    kt0 = 2 * n_pairs

    @pl.when(kt0 == n_any - 1)
    def _():
        consume(0, kt0, True)

    @pl.when(kt0 < n_any - 1)
    def _():
        produce(1, kt0 + 1)
        consume(0, kt0, False)
        consume(1, kt0 + 1, True)


def _mla_kernel(q_ref, k_ref, v_ref, o_ref, m_ref, acc_ref, s_ref, *, tq, tk, kv_len, q_pos0):
    sk = k_ref.shape[1]
    rows = H_MLA * tq
    q_start = q_pos0 + pl.program_id(1) * tq
    _, n_any = _kv_tile_counts((q_start // CHUNK + 1) * CHUNK,
                               ((q_start + tq - 1) // CHUNK + 1) * CHUNK, tk, kv_len, sk)
    q = q_ref[...].reshape(rows, q_ref.shape[-1])
    q_chunk = (q_start + lax.broadcasted_iota(jnp.int32, (rows, 1), 0) % tq) // CHUNK

    def produce(slot, kt):
        s_ref[slot] = _dot_nt(q, k_ref[0, pl.ds(pl.multiple_of(kt * tk, tk), tk), :])

    def consume(slot, kt, masked):
        k0 = pl.multiple_of(kt * tk, tk)
        s = s_ref[slot]
        if masked:
            k_pos = k0 + lax.broadcasted_iota(jnp.int32, (1, tk), 1)
            mask = k_pos // CHUNK <= q_chunk
            if kv_len < sk:
                mask = jnp.logical_and(mask, k_pos < kv_len)
            s = jnp.where(mask, s, NEG_INF)
        _online_softmax_step(s, v_ref[0, pl.ds(k0, tk), :], m_ref, acc_ref)

    _init_softmax_state(m_ref, acc_ref)
    _kv_loop_pipelined(n_any, produce, consume)
    o = _normalised(acc_ref[...]).astype(BF16)
    for h in range(H_MLA):
        o_ref[:, h * KV_LORA:(h + 1) * KV_LORA] = o[h * tq:(h + 1) * tq, :]


def _mla_attention(qcat, kcat, latb, batch, sq, tq, tk, kv_len, q_pos0):
    sk = kcat.shape[1]
    nq = sq // tq
    assert tk % tq == 0 and q_pos0 % tq == 0 and tq % CHUNK in (0, tq)
    kern = functools.partial(_mla_kernel, tq=tq, tk=tk, kv_len=kv_len, q_pos0=q_pos0)
    return pl.pallas_call(
        kern,
        grid=(batch, nq),
        in_specs=[pl.BlockSpec((H_MLA, tq, 2 * LANES), lambda b, i: (0, b * nq + i, 0)),
                  pl.BlockSpec((1, sk, 2 * LANES), lambda b, i: (b, 0, 0)),
                  pl.BlockSpec((1, sk, 2 * LANES), lambda b, i: (b, 0, 0))],
        out_specs=pl.BlockSpec((tq, H_MLA * KV_LORA), lambda b, i: (b * nq + i, 0)),
        out_shape=jax.ShapeDtypeStruct((batch * sq, H_MLA * KV_LORA), BF16),
        scratch_shapes=[pltpu.VMEM((H_MLA * tq, LANES), F32),
                        pltpu.VMEM((H_MLA * tq, 2 * LANES), F32),
                        pltpu.VMEM((2, H_MLA * tq, tk), F32)],
        compiler_params=_params(2),
    )(qcat, kcat, latb)


def _fox_kernel(q_ref, k_ref, v_ref, cq_ref, ck_ref, o_ref, m_ref, acc_ref, cqb_ref, *, tq, tk,
                kv_len, q_pos0):
    sk = k_ref.shape[1]
    q_start = q_pos0 + pl.program_id(1) * tq
    n_full, n_any = _kv_tile_counts(q_start + 1, q_start + tq, tk, kv_len, sk)
    lane = lax.broadcasted_iota(jnp.int32, (tq, LANES), 1)
    n_pairs = H_FOX * HD_FOX // LANES
    q_pos = q_start + lax.broadcasted_iota(jnp.int32, (2 * tq, 1), 0) % tq
    q2 = []
    for j in range(n_pairs):
        q_pair = q_ref[:, j * LANES:(j + 1) * LANES]
        zero = jnp.zeros_like(q_pair)
        q2.append(jnp.concatenate([jnp.where(lane < HD_FOX, q_pair, zero),
                                   jnp.where(lane < HD_FOX, zero, q_pair)], axis=0))
        for half in range(2):
            col = jnp.sum(jnp.where(lane == 2 * j + half, cq_ref[0], 0.0), axis=-1, keepdims=True)
            cqb_ref[j, half * tq:(half + 1) * tq, :] = jnp.broadcast_to(col, (tq, LANES))

    def consume(kt, masked):
        k0 = pl.multiple_of(kt * tk, tk)
        if masked:
            k_pos = k0 + lax.broadcasted_iota(jnp.int32, (1, tk), 1)
            mask = k_pos <= q_pos
            if kv_len < sk:
                mask = jnp.logical_and(mask, k_pos < kv_len)
        for j in range(n_pairs):
            s = _dot_nt(q2[j], k_ref[0, pl.ds(k0, tk), j * LANES:(j + 1) * LANES])
            c_k = jnp.concatenate(
                [jnp.broadcast_to(ck_ref[0, 2 * j:2 * j + 1, pl.ds(k0, tk)], (tq, tk)),
                 jnp.broadcast_to(ck_ref[0, 2 * j + 1:2 * j + 2, pl.ds(k0, tk)], (tq, tk))], axis=0)
            s = s - c_k
            if masked:
                s = jnp.where(mask, s, NEG_INF)
            v_ones = v_ref[0, pl.ds(k0, tk), 2 * j * LANES:2 * (j + 1) * LANES]
            _online_softmax_step(s, v_ones, m_ref.at[j], acc_ref.at[j], shift=cqb_ref[j])

    _init_softmax_state(m_ref, acc_ref)
    _kv_loop(n_full, n_any, consume)
    for j in range(n_pairs):
        o = _normalised(acc_ref[j])
        o_ref[:, j * LANES:(j + 1) * LANES] = jnp.where(lane < HD_FOX, o[:tq], o[tq:]).astype(BF16)


def _fox_attention(fq, fkb, fvb, c_q, c_k, batch, sq, tq, tk, kv_len, q_pos0):
    sk = fkb.shape[1]
    nq = sq // tq
    w = H_FOX * HD_FOX
    kern = functools.partial(_fox_kernel, tq=tq, tk=tk, kv_len=kv_len, q_pos0=q_pos0)
    return pl.pallas_call(
        kern,
        grid=(batch, nq),
        in_specs=[pl.BlockSpec((tq, w), lambda b, i: (b * nq + i, 0)),
                  pl.BlockSpec((1, sk, w), lambda b, i: (b, 0, 0)),
                  pl.BlockSpec((1, sk, 2 * w), lambda b, i: (b, 0, 0)),
                  pl.BlockSpec((1, tq, LANES), lambda b, i: (b, i, 0)),
                  pl.BlockSpec((1, H_FOX, sk), lambda b, i: (b, 0, 0))],
        out_specs=pl.BlockSpec((tq, w), lambda b, i: (b * nq + i, 0)),
        out_shape=jax.ShapeDtypeStruct((batch * sq, w), BF16),
        scratch_shapes=[pltpu.VMEM((w // LANES, 2 * tq, LANES), F32),
                        pltpu.VMEM((w // LANES, 2 * tq, 2 * LANES), F32),
                        pltpu.VMEM((w // LANES, 2 * tq, LANES), F32)],
        compiler_params=_params(2),
    )(fq, fkb, fvb, c_q, c_k)


def _fox_cached_kernel(q_ref, ck_ref, cv_ref, kn_ref, vn_ref, cq_ref, crow_ref, o_ref, *, past):
    tq = q_ref.shape[0]
    ck_ref = ck_ref.reshape(1, past * H_FOX, HD_FOX)
    cv_ref = cv_ref.reshape(1, past * H_FOX, HD_FOX)
    pad = jnp.zeros((LANES - tq, HD_FOX), BF16)
    q_idx = lax.broadcasted_iota(jnp.int32, (tq, LANES), 0)
    k_idx = lax.broadcasted_iota(jnp.int32, (tq, LANES), 1)
    visible = k_idx <= q_idx
    for h in range(H_FOX):
        cols = slice(h * HD_FOX, (h + 1) * HD_FOX)
        q = q_ref[:, cols]
        c_q = cq_ref[0, :, h:h + 1]
        k_new = jnp.concatenate([kn_ref[:, cols].astype(BF16), pad], axis=0)
        v_new = jnp.concatenate([vn_ref[:, cols].astype(BF16), pad], axis=0)
        head_rows = pl.ds(h, past, stride=H_FOX)
        s_old = (_dot_nt(q, ck_ref[0, head_rows, :].astype(BF16))
                 + (c_q - crow_ref[0, h:h + 1, 0:past]))
        s_new = _dot_nt(q, k_new) + (c_q - crow_ref[0, h:h + 1, past:past + LANES])
        s_new = jnp.where(visible, s_new, NEG_INF)
        m = jnp.maximum(jnp.max(s_old, axis=-1, keepdims=True), jnp.max(s_new, axis=-1, keepdims=True))
        p_old = jnp.exp2(s_old - m)
        p_new = jnp.exp2(s_new - m)
        l = jnp.sum(p_old, axis=-1, keepdims=True) + jnp.sum(p_new, axis=-1, keepdims=True)
        o = (_dot(p_old.astype(BF16), cv_ref[0, head_rows, :].astype(BF16))
             + _dot(p_new.astype(BF16), v_new))
        o_ref[:, cols] = (o / l).astype(BF16)


def _fox_cached_attention(fq, cache_k, cache_v, k_new, v_new, c_q, c_row, dec_seq, layer):
    batch, past = cache_k.shape[1:3]
    w = H_FOX * HD_FOX
    assert past % LANES == 0 and dec_seq <= LANES and c_row.shape[2] >= past + LANES
    new = pl.BlockSpec((dec_seq, w), lambda b: (b, 0))
    old = pl.BlockSpec((None, 1, past, H_FOX, HD_FOX), lambda b: (layer, b, 0, 0, 0))
    return pl.pallas_call(
        functools.partial(_fox_cached_kernel, past=past),
        grid=(batch,),
        in_specs=[new, old, old, new, new,
                  pl.BlockSpec((1, dec_seq, LANES), lambda b: (b, 0, 0)),
                  pl.BlockSpec((1, H_FOX, c_row.shape[2]), lambda b: (b, 0, 0))],
        out_specs=new,
        out_shape=jax.ShapeDtypeStruct((batch * dec_seq, w), BF16),
        compiler_params=_params(1),
    )(fq, cache_k, cache_v, k_new, v_new, c_q, c_row)


def _mem_kernel(q_ref, k_ref, v_ref, o_ref):
    for h in range(H_MEM):
        cols = slice(h * HD_MEM, (h + 1) * HD_MEM)
        s = _dot_nt(q_ref[:, cols], k_ref[0, :, cols]) * (MEM_SCALE * LOG2E)
        m = jnp.max(s, axis=-1, keepdims=True)
        p = jnp.exp2(s - m)
        l = jnp.sum(p, axis=-1, keepdims=True)
        o = _dot(p.astype(BF16), v_ref[0, :, cols]) / l
        o_ref[:, cols] = o.astype(BF16)


def _mem_attention(mq, mkb, mvb, batch, sq, tq):
    nq = sq // tq
    n_mem = mkb.shape[1]
    w = H_MEM * HD_MEM
    return pl.pallas_call(
        _mem_kernel,
        grid=(batch, nq),
        in_specs=[pl.BlockSpec((tq, w), lambda b, i: (b * nq + i, 0)),
                  pl.BlockSpec((1, n_mem, w), lambda b, i: (b, 0, 0)),
                  pl.BlockSpec((1, n_mem, w), lambda b, i: (b, 0, 0))],
        out_specs=pl.BlockSpec((tq, w), lambda b, i: (b * nq + i, 0)),
        out_shape=jax.ShapeDtypeStruct((batch * sq, w), BF16),
        compiler_params=_params(2),
    )(mq, mkb, mvb)


def _route(x1b, wr_t, bias):
    tm = x1b.shape[0]
    scores = _sigmoid(_dot_nt(wr_t, x1b))
    sel = scores + bias
    sub = lax.broadcasted_iota(jnp.int32, (GROUP_SIZE, tm), 0)
    gs = []
    for g in range(N_GROUPS):
        blk = sel[g * GROUP_SIZE:(g + 1) * GROUP_SIZE, :]
        m1 = jnp.max(blk, axis=0, keepdims=True)
        first = jnp.min(jnp.where(blk == m1, sub, GROUP_SIZE), axis=0, keepdims=True)
        m2 = jnp.max(jnp.where(sub == first, -jnp.inf, blk), axis=0, keepdims=True)
        gs.append(m1 + m2)
    e_idx = lax.broadcasted_iota(jnp.int32, (N_EXPERTS, tm), 0)
    allowed = jnp.zeros((N_EXPERTS, tm), jnp.bool_)
    for g in range(N_GROUPS):
        rank = jnp.zeros((1, tm), jnp.int32)
        for o in range(N_GROUPS):
            if o == g:
                continue
            beats = (gs[o] >= gs[g]) if o < g else (gs[o] > gs[g])
            rank = rank + beats.astype(jnp.int32)
        keep = rank < TOPK_GROUPS
        allowed = jnp.logical_or(allowed, jnp.logical_and(e_idx // GROUP_SIZE == g, keep))
    cand = jnp.where(allowed, sel, -jnp.inf)
    chosen = jnp.zeros((N_EXPERTS, tm), jnp.bool_)
    for _ in range(TOP_K):
        m = jnp.max(cand, axis=0, keepdims=True)
        first = jnp.min(jnp.where(cand == m, e_idx, N_EXPERTS), axis=0, keepdims=True)
        pick = e_idx == first
        chosen = jnp.logical_or(chosen, pick)
        cand = jnp.where(pick, -jnp.inf, cand)
    w = jnp.where(chosen, scores, 0.0)
    return w / jnp.sum(w, axis=0, keepdims=True) * ROUTED_SCALE, chosen


def _swiglu(xb, wg, wu):
    a = _dot(xb, wg)
    return (a * _sigmoid(a)) * _dot(xb, wu)


def _merge_kernel(x_ref, olat_ref, ofox_ref, omem_ref, wg_ref, wuv_ref, wbm_ref, wbf_ref, wbc_ref,
                  wo_ref, g1_ref, b1_ref, wsg_ref, wsu_ref, wsd_ref, x1b_ref, base_ref):
    x = x_ref[...]
    xb = x.astype(BF16)
    o_mla = jnp.concatenate(
        [_dot(olat_ref[:, j * 2 * KV_LORA:(j + 1) * 2 * KV_LORA], wuv_ref[j]).astype(BF16)
         for j in range(H_MLA // 2)], axis=-1)

    def gate(n):
        return _sigmoid(_dot(xb, wg_ref[:, n * D_MODEL:(n + 1) * D_MODEL]))

    y = gate(0) * _dot(o_mla, wbm_ref[...])
    y = y + gate(1) * _dot(ofox_ref[...], wbf_ref[...])
    y = y + gate(2) * _dot(omem_ref[...], wbc_ref[...])
    mix = _dot(y.astype(BF16), wo_ref[...])
    x1 = _layer_norm(ALPHA * x + mix, g1_ref[...], b1_ref[...])
    x1b = x1.astype(BF16)
    x1b_ref[...] = x1b
    shared = _dot(_swiglu(x1b, wsg_ref[...], wsu_ref[...]).astype(BF16), wsd_ref[...])
    base_ref[...] = ALPHA * x1 + shared


def _merge(x, olat, ofox, omem, wg, wuv, wbm, wbf, wbc, wo, g1, b1, wsg, wsu, wsd, tm):
    t = x.shape[0]
    row = lambda n: pl.BlockSpec((tm, n), lambda i: (i, 0))
    ws = (wg, wuv, wbm, wbf, wbc, wo, g1, b1, wsg, wsu, wsd)
    return pl.pallas_call(
        _merge_kernel,
        grid=(t // tm,),
        in_specs=[row(D_MODEL), row(H_MLA * KV_LORA), row(H_FOX * HD_FOX), row(H_MEM * HD_MEM)]
        + [_full(w.shape) for w in ws],
        out_specs=(row(D_MODEL), row(D_MODEL)),
        out_shape=(jax.ShapeDtypeStruct((t, D_MODEL), BF16), jax.ShapeDtypeStruct((t, D_MODEL), F32)),
        compiler_params=_params(1),
    )(x, olat, ofox, omem, *ws)


BF16_ROWS = 16
PLACE_ROWS = 512
EXPERT_ROWS = 4096
EXPERT_CHUNK = 512
EXPERTS_PER_STEP = 4


def _capacity(tm):
    return min(tm, _round_up(2 * tm * TOP_K // N_EXPERTS, BF16_ROWS))


def _dispatch_kernel(x1b_ref, wr_ref, rb_ref, xs_ref, gate_ref, slot_ref, govf_ref, flag_ref,
                     pm_ref, *, tm, cap):
    n_tiles = x1b_ref.shape[0] // tm
    slots = []
    for h in range(n_tiles):
        rows = slice(h * tm, (h + 1) * tm)
        xb = x1b_ref[rows, :]
        gate, chosen = _route(xb, wr_ref[...], rb_ref[...])
        r = lax.broadcasted_iota(jnp.int32, (tm, tm), 0)
        c = lax.broadcasted_iota(jnp.int32, (tm, tm), 1)
        earlier = jnp.where(r < c, 1.0, 0.0).astype(BF16)
        rank = _dot(jnp.where(chosen, 1.0, 0.0).astype(BF16), earlier)
        fits = jnp.logical_and(chosen, rank < cap)
        slot = jnp.where(fits, rank, -1.0)
        gate_ref[:, rows] = jnp.where(fits, gate, 0.0)
        slot_ref[:, rows] = slot
        g_ovf = jnp.where(fits, 0.0, gate)
        govf_ref[rows, :] = jnp.concatenate(
            [g_ovf, jnp.zeros((LANES - N_EXPERTS, tm), F32)], axis=0).T
        any_ovf = jnp.max(jnp.max(g_ovf, axis=0, keepdims=True), axis=1, keepdims=True)
        flag_ref[h] = jnp.broadcast_to(jnp.where(any_ovf > 0.0, 1, 0),
                                       flag_ref.shape[1:]).astype(jnp.int32)
        slots.append(slot)
    s_iota = lax.broadcasted_iota(jnp.int32, (cap, tm), 0).astype(F32)
    for h in range(n_tiles):
        slot = slots[h]
        xb = x1b_ref[h * tm:(h + 1) * tm, :]
        for e in range(N_EXPERTS):
            pm_ref[h, e * cap:(e + 1) * cap, :] = jnp.where(
                s_iota == slot[e:e + 1, :], 1.0, 0.0).astype(BF16)
        per = PLACE_ROWS // cap
        for g in range(N_EXPERTS // per):
            xs = _dot(pm_ref[h, g * PLACE_ROWS:(g + 1) * PLACE_ROWS, :], xb).astype(BF16)
            xs_ref[g * per:(g + 1) * per, h * cap:(h + 1) * cap, :] = xs.reshape(per, cap, D_MODEL)


def _dispatch(x1b, wr_t, rb, tm, cap):
    t = x1b.shape[0]
    nt = t // tm
    ts = 2 if nt % 2 == 0 else 1
    col = pl.BlockSpec((N_EXPERTS, ts * tm), lambda i: (0, i))
    return pl.pallas_call(
        functools.partial(_dispatch_kernel, tm=tm, cap=cap),
        grid=(nt // ts,),
        in_specs=[pl.BlockSpec((ts * tm, D_MODEL), lambda i: (i, 0)), _full(wr_t.shape),
                  _full(rb.shape)],
        out_specs=(pl.BlockSpec((N_EXPERTS, ts * cap, D_MODEL), lambda i: (0, i, 0)), col, col,
                   pl.BlockSpec((ts * tm, LANES), lambda i: (i, 0)),
                   pl.BlockSpec((ts, 8, LANES), lambda i: (i, 0, 0))),
        out_shape=(jax.ShapeDtypeStruct((N_EXPERTS, nt * cap, D_MODEL), BF16),
                   jax.ShapeDtypeStruct((N_EXPERTS, t), F32),
                   jax.ShapeDtypeStruct((N_EXPERTS, t), F32),
                   jax.ShapeDtypeStruct((t, LANES), F32),
                   jax.ShapeDtypeStruct((nt, 8, LANES), jnp.int32)),
        scratch_shapes=[pltpu.VMEM((ts, N_EXPERTS * cap, tm), BF16)],
        compiler_params=_params(1),
    )(x1b, wr_t, rb)


def _expert_kernel(xs_ref, xs2_ref, wg_ref, wu_ref, wd_ref, ys_ref, ys2_ref, wgb_ref, wub_ref,
                   wdb_ref, *, chunk):
    def ffn(x):
        hid = _swiglu(x, wgb_ref[...], wub_ref[...])
        return _dot(hid.astype(BF16), wdb_ref[...]).astype(BF16)

    @pl.when(pl.program_id(1) == 0)
    def _():
        wgb_ref[...] = wg_ref[0].astype(BF16)
        wub_ref[...] = wu_ref[0].astype(BF16)
        wdb_ref[...] = wd_ref[0].astype(BF16)
        ys2_ref[0] = ffn(xs2_ref[0])

    def body(c, carry):
        r0 = pl.multiple_of(c * chunk, chunk)
        ys_ref[0, pl.ds(r0, chunk), :] = ffn(xs_ref[0, pl.ds(r0, chunk), :])
        return carry

    lax.fori_loop(0, xs_ref.shape[1] // chunk, body, 0)


def _experts(xs, xs2, weg, weu, wed):
    n_e, rows, _ = xs.shape
    rb = min(EXPERT_ROWS, rows)
    blk = pl.BlockSpec((1, rb, D_MODEL), lambda e, i: (e, i, 0))
    blk2 = pl.BlockSpec((1, xs2.shape[1], D_MODEL), lambda e, i: (e, 0, 0))
    return pl.pallas_call(
        functools.partial(_expert_kernel, chunk=min(EXPERT_CHUNK, rb)),
        grid=(n_e, rows // rb),
        in_specs=[blk, blk2,
                  pl.BlockSpec((1, D_MODEL, D_EXPERT), lambda e, i: (e, 0, 0)),
                  pl.BlockSpec((1, D_MODEL, D_EXPERT), lambda e, i: (e, 0, 0)),
                  pl.BlockSpec((1, D_EXPERT, D_MODEL), lambda e, i: (e, 0, 0))],
        out_specs=(blk, blk2),
        out_shape=(jax.ShapeDtypeStruct(xs.shape, BF16), jax.ShapeDtypeStruct(xs2.shape, BF16)),
        scratch_shapes=[pltpu.VMEM((D_MODEL, D_EXPERT), BF16), pltpu.VMEM((D_MODEL, D_EXPERT), BF16),
                        pltpu.VMEM((D_EXPERT, D_MODEL), BF16)],
        compiler_params=_params(2),
    )(xs, xs2, weg, weu, wed)


def _dense_routed_kernel(flag_ref, x1b_ref, gate_ref, weg_ref, weu_ref, wed_ref, out_ref):
    step = pl.program_id(1)

    @pl.when(step == 0)
    def _():
        out_ref[...] = jnp.zeros_like(out_ref)

    @pl.when(flag_ref[pl.program_id(0)] > 0)
    def _():
        xb = x1b_ref[...]
        gate = gate_ref[...]
        lane = lax.broadcasted_iota(jnp.int32, gate.shape, 1)
        for j in range(EXPERTS_PER_STEP):
            e = step * EXPERTS_PER_STEP + j
            g_col = jnp.sum(jnp.where(lane == e, gate, 0.0), axis=-1, keepdims=True)
            hid = _swiglu(xb, weg_ref[j], weu_ref[j])
            out_ref[...] += g_col * _dot(hid.astype(BF16), wed_ref[j])


def _dense_routed(flags, x1b, gate, weg, weu, wed, tm):
    t = x1b.shape[0]
    eb = EXPERTS_PER_STEP
    row = lambda n: pl.BlockSpec((tm, n), lambda i, s, f: (i, 0))
    wspec = lambda a, b: pl.BlockSpec((eb, a, b), lambda i, s, f: (jnp.where(f[i] > 0, s, 0), 0, 0))
    return pl.pallas_call(
        _dense_routed_kernel,
        grid_spec=pltpu.PrefetchScalarGridSpec(
            num_scalar_prefetch=1,
            grid=(t // tm, N_EXPERTS // eb),
            in_specs=[row(D_MODEL), row(LANES), wspec(D_MODEL, D_EXPERT), wspec(D_MODEL, D_EXPERT),
                      wspec(D_EXPERT, D_MODEL)],
            out_specs=row(D_MODEL)),
        out_shape=jax.ShapeDtypeStruct((t, D_MODEL), F32),
        compiler_params=_params(2),
    )(flags, x1b, gate, weg, weu, wed)


def _combine_kernel(gate_ref, slot_ref, ys_ref, base_ref, *rest, cap, has_overflow):
    if has_overflow:
        rovf_ref, g2_ref, b2_ref, y_ref, pw_ref = rest
    else:
        g2_ref, b2_ref, y_ref, pw_ref = rest
    tm = base_ref.shape[0]
    gate = gate_ref[...]
    slot = slot_ref[...]
    s_iota = lax.broadcasted_iota(jnp.int32, (cap, tm), 0).astype(F32)
    for e in range(N_EXPERTS):
        pw_ref[e * cap:(e + 1) * cap, :] = jnp.where(
            s_iota == slot[e:e + 1, :], gate[e:e + 1, :], 0.0).astype(BF16)
    routed = rovf_ref[...] if has_overflow else jnp.zeros((tm, D_MODEL), F32)
    per = PLACE_ROWS // cap
    for g in range(N_EXPERTS // per):
        ys = ys_ref[g * per:(g + 1) * per].reshape(PLACE_ROWS, D_MODEL)
        routed = routed + lax.dot_general(pw_ref[g * PLACE_ROWS:(g + 1) * PLACE_ROWS, :], ys,
                                          (((0,), (0,)), ((), ())), preferred_element_type=F32)
    y_ref[...] = _layer_norm(base_ref[...] + routed, g2_ref[...], b2_ref[...])


def _combine(gate_t, slot_t, ys, base, rovf, g2, b2, tm, cap):
    t = base.shape[0]
    col = pl.BlockSpec((N_EXPERTS, tm), lambda i: (0, i))
    row = pl.BlockSpec((tm, D_MODEL), lambda i: (i, 0))
    has_overflow = rovf is not None
    operands = (gate_t, slot_t, ys, base) + ((rovf,) if has_overflow else ()) + (g2, b2)
    return pl.pallas_call(
        functools.partial(_combine_kernel, cap=cap, has_overflow=has_overflow),
        grid=(t // tm,),
        in_specs=[col, col, pl.BlockSpec((N_EXPERTS, cap, D_MODEL), lambda i: (0, i, 0)), row]
        + ([row] if has_overflow else []) + [_full(g2.shape), _full(b2.shape)],
        out_specs=row,
        out_shape=jax.ShapeDtypeStruct((t, D_MODEL), F32),
        scratch_shapes=[pltpu.VMEM((N_EXPERTS * cap, tm), BF16)],
        compiler_params=_params(1),
    )(*operands)


def _moe_place(x1b, wr_t, rb, tm):
    cap = _capacity(tm)
    assert PLACE_ROWS % cap == 0 and (N_EXPERTS * cap) % PLACE_ROWS == 0
    return _dispatch(x1b, wr_t, rb, tm, cap)


def _moe_ln2(placed, ys, x1b, base, weg, weu, wed, g2, b2, tm):
    cap = _capacity(tm)
    _, gate_t, slot_t, gate_ovf, flags = placed
    flags = flags[:, 0, 0]

    def with_overflow():
        rovf = _dense_routed(flags, x1b, gate_ovf, weg.astype(BF16), weu.astype(BF16),
                             wed.astype(BF16), tm)
        return _combine(gate_t, slot_t, ys, base, rovf, g2, b2, tm, cap)

    return lax.cond(jnp.any(flags > 0), with_overflow,
                    lambda: _combine(gate_t, slot_t, ys, base, None, g2, b2, tm, cap))


def _rot_half_cols(w):
    half = w.shape[-1] // 2
    return jnp.concatenate([-w[..., half:], w[..., :half]], axis=-1)


def _block_diag2(a, b):
    za = jnp.zeros((a.shape[0], b.shape[1]), a.dtype)
    zb = jnp.zeros((b.shape[0], a.shape[1]), a.dtype)
    return jnp.concatenate([jnp.concatenate([a, za], 1), jnp.concatenate([zb, b], 1)], 0)


def _prep_weights(w_in, b_f, q_norm_g, w_q_up, kv_norm_g, w_kv_up):
    widths = (Q_LORA, KV_LORA, ROPE_DIM, H_FOX * HD_FOX, H_FOX * HD_FOX, H_FOX * HD_FOX, H_FOX,
              H_MEM * HD_MEM, N_BRANCH * D_MODEL)
    offs = np.cumsum((0,) + widths)
    w_ql, w_ckv, w_kr, w_fq, w_fk, w_fv, w_fl, w_mq, w_g = (
        w_in[:, offs[i]:offs[i + 1]] for i in range(len(widths)))
    rep = LANES // ROPE_DIM
    w_main = jnp.concatenate(
        [w_ql, w_ckv, jnp.tile(w_kr, (1, rep)), jnp.tile(_rot_half_cols(w_kr), (1, rep)),
         w_fq, w_fk, w_fv, w_mq, jnp.pad(w_fl, ((0, 0), (0, LANES - H_FOX)))], axis=1).astype(BF16)
    assert w_main.shape[1] == C_END
    wq3 = w_q_up.reshape(Q_LORA, H_MLA, NOPE_DIM + ROPE_DIM)
    wq_nope = wq3[:, :, :NOPE_DIM].reshape(Q_LORA, H_MLA * NOPE_DIM)
    wq_rope = wq3[:, :, NOPE_DIM:]
    w_q = jnp.concatenate(
        [wq_nope, wq_rope.reshape(Q_LORA, -1), _rot_half_cols(wq_rope).reshape(Q_LORA, -1)],
        axis=1).astype(BF16)
    wkv3 = w_kv_up.reshape(KV_LORA, H_MLA, NOPE_DIM + V_DIM)
    w_uk = wkv3[:, :, :NOPE_DIM]
    w_uv = wkv3[:, :, NOPE_DIM:]
    w_abs = jnp.stack([_block_diag2(w_uk[:, 2 * j].T, w_uk[:, 2 * j + 1].T)
                       for j in range(H_MLA // 2)]).astype(BF16)
    w_uvp = jnp.stack([_block_diag2(w_uv[:, 2 * j], w_uv[:, 2 * j + 1])
                       for j in range(H_MLA // 2)]).astype(BF16)
    b_fp = jnp.pad(b_f, (0, LANES - H_FOX)).reshape(1, LANES)
    return (w_main, w_q, w_abs, q_norm_g.reshape(1, -1), kv_norm_g.reshape(1, -1), b_fp,
            w_g.astype(BF16), w_uvp)


def _rope_tables(pos):
    half = ROPE_DIM // 2
    inv_freq = ROPE_THETA ** (-jnp.arange(half, dtype=F32) / half)
    ang = pos.astype(F32)[:, None] * inv_freq[None, :]
    rep = 2 * LANES // ROPE_DIM
    cos = jnp.tile(jnp.concatenate([jnp.cos(ang)] * 2, axis=-1), (1, rep))
    sin = jnp.tile(jnp.concatenate([jnp.sin(ang)] * 2, axis=-1), (1, rep))
    return cos, sin


def _round_up(n, m):
    return (n + m - 1) // m * m


def kernel(x_prompt, x_sample, mem_prompt, cache_mla_latent, cache_mla_krope, cache_fox_k,
           cache_fox_v, cache_fox_logf, cache_mem_k, cache_mem_v, ln1_g, ln1_b, w_in, b_f,
           q_norm_g, w_q_up, kv_norm_g, w_kv_up, w_mem_k, w_mem_v, w_br_mla, w_br_fox,
           w_br_mem, w_out, ln2_g, ln2_b, w_router, router_bias, w_e_gate, w_e_up, w_e_down,
           w_s_gate, w_s_up, w_s_down):
    assert w_in.shape[0] == DEPTH == 1
    batch, seq, _ = x_prompt.shape
    dec_batch, dec_seq, _ = x_sample.shape
    past = cache_mla_latent.shape[2]
    n_mem = mem_prompt.shape[1]
    l = 0

    (w_main, w_q, w_abs, q_g, kv_g, b_fp, w_gate, w_uvp) = _prep_weights(
        w_in[l], b_f[l], q_norm_g[l], w_q_up[l], kv_norm_g[l], w_kv_up[l])
    merge_w = (w_gate, w_uvp, w_br_mla[l].astype(BF16), w_br_fox[l].astype(BF16),
               w_br_mem[l].astype(BF16), w_out[l].astype(BF16), ln1_g[l].reshape(1, -1),
               ln1_b[l].reshape(1, -1), w_s_gate[l].astype(BF16), w_s_up[l].astype(BF16),
               w_s_down[l].astype(BF16))
    route_w = (w_router[l].T.astype(BF16), router_bias[l].reshape(-1, 1))
    expert_w = (w_e_gate[l], w_e_up[l], w_e_down[l])
    ln2_w = (ln2_g[l].reshape(1, -1), ln2_b[l].reshape(1, -1))

    tp = batch * seq
    xp = x_prompt.reshape(tp, D_MODEL)
    tm_p = min(512, seq)
    tq_p = min(256, seq)
    tk_p = min(512, seq)
    cos_p, sin_p = _rope_tables(jnp.arange(seq, dtype=jnp.int32))
    (qcat, kcat, lat, latb, kr, fq, fk, fv, fkb, fvb, mq, logf, logfp) = _proj(
        xp, w_main, w_q, w_abs, q_g, kv_g, b_fp, cos_p, sin_p, tm_p)
    tmem = batch * n_mem
    m_k, m_v, m_kb, m_vb = _memkv(mem_prompt.reshape(tmem, D_MODEL), w_mem_k[l].astype(BF16),
                                  w_mem_v[l].astype(BF16), min(512, tmem))
    c_p = _cumsum(logfp.reshape(batch, seq, LANES))
    c_keys = jnp.swapaxes(c_p[:, :, :H_FOX], 1, 2)
    olat = _mla_attention(qcat, kcat.reshape(batch, seq, -1), latb.reshape(batch, seq, -1),
                          batch, seq, tq_p, tk_p, seq, 0)
    ofox = _fox_attention(fq, fkb.reshape(batch, seq, -1), fvb.reshape(batch, seq, -1), c_p, c_keys,
                          batch, seq, tk_p, tk_p, seq, 0)
    omem = _mem_attention(mq, m_kb.reshape(batch, n_mem, -1), m_vb.reshape(batch, n_mem, -1),
                          batch, seq, tk_p)
    x1b_p, base_p = _merge(xp, olat, ofox, omem, *merge_w, tm_p)
    tm_moe = min(256, seq)
    placed_p = _moe_place(x1b_p, *route_w, tm_moe)

    ts = dec_batch * dec_seq
    xs = x_sample.reshape(ts, D_MODEL)
    cos_s, sin_s = _rope_tables(jnp.tile(past + jnp.arange(dec_seq, dtype=jnp.int32), dec_batch))
    (qcat_s, kcat_s, lat_s, latb_s, kr_s, fq_s, fk_s, fv_s, fkb_s, fvb_s, mq_s, logf_s,
     logfp_s) = _proj(xs, w_main, w_q, w_abs, q_g, kv_g, b_fp, cos_s, sin_s, ts)
    kv_len = past + dec_seq
    tk_s = 1024
    sk = _round_up(kv_len, tk_s)

    def with_cache(cache, new):
        new = new.reshape(dec_batch, dec_seq, -1)
        pad = jnp.zeros((dec_batch, sk - kv_len, new.shape[-1]), new.dtype)
        return jnp.concatenate([cache.astype(new.dtype), new, pad], axis=1)

    rep = LANES // ROPE_DIM
    kcat_all = with_cache(
        jnp.concatenate([cache_mla_latent[l], jnp.tile(cache_mla_krope[l], (1, 1, rep))], axis=-1),
        kcat_s)
    lat_all = with_cache(
        jnp.concatenate([cache_mla_latent[l], jnp.ones((dec_batch, past, LANES), F32)], axis=-1), latb_s)
    logf_all = with_cache(jnp.pad(cache_fox_logf[l], ((0, 0), (0, 0), (0, LANES - H_FOX))), logfp_s)
    c_s = _cumsum(logf_all)
    c_keys_s = jnp.swapaxes(c_s[:, :, :H_FOX], 1, 2)
    c_q_s = c_s[:, past:kv_len]
    olat_s = _mla_attention(qcat_s, kcat_all, lat_all, dec_batch, dec_seq, dec_seq, tk_s, kv_len, past)
    ofox_s = _fox_cached_attention(fq_s, cache_fox_k, cache_fox_v, fk_s, fv_s, c_q_s, c_keys_s,
                                   dec_seq, l)
    omem_s = _mem_attention(mq_s, cache_mem_k[l].reshape(dec_batch, n_mem, -1).astype(BF16),
                            cache_mem_v[l].reshape(dec_batch, n_mem, -1).astype(BF16),
                            dec_batch, dec_seq, dec_seq)
    x1b_s, base_s = _merge(xs, olat_s, ofox_s, omem_s, *merge_w, ts)
    placed_s = _moe_place(x1b_s, *route_w, ts)

    ys_p, ys_s = _experts(placed_p[0], placed_s[0], *expert_w)
    y_p = _moe_ln2(placed_p, ys_p, x1b_p, base_p, *expert_w, *ln2_w, tm_moe)
    y_s = _moe_ln2(placed_s, ys_s, x1b_s, base_s, *expert_w, *ln2_w, ts)
    y_p = y_p.reshape(batch, seq, D_MODEL)
    y_s = y_s.reshape(dec_batch, dec_seq, D_MODEL)

    def stack(a, b, s, *tail_shape):
        return a.reshape(1, b, s, *tail_shape)

    return (y_p, y_s,
            stack(lat, batch, seq, KV_LORA), stack(kr, batch, seq, ROPE_DIM),
            stack(fk, batch, seq, H_FOX, HD_FOX), stack(fv, batch, seq, H_FOX, HD_FOX),
            stack(logf, batch, seq, H_FOX),
            stack(m_k, batch, n_mem, H_MEM, HD_MEM), stack(m_v, batch, n_mem, H_MEM, HD_MEM),
            stack(lat_s, dec_batch, dec_seq, KV_LORA), stack(kr_s, dec_batch, dec_seq, ROPE_DIM),
            stack(fk_s, dec_batch, dec_seq, H_FOX, HD_FOX), stack(fv_s, dec_batch, dec_seq, H_FOX, HD_FOX),
            stack(logf_s, dec_batch, dec_seq, H_FOX))
```
